```python
import jax, jax.numpy as jnp
from jax import lax
import numpy as np

D_MODEL = 1024
BATCH = 4
SEQ = 4096
DEPTH = 1

D_MIX = D_MODEL
D_LRU = D_MIX // 2
D_SGU = D_MIX - D_LRU
LRU_HEADS = 8
LRU_HEAD_DIM = D_LRU // LRU_HEADS
CONV_WIDTH = 4
LRU_C = 8.0
SGU_HEADS = 8
SGU_HEAD_DIM = D_SGU // SGU_HEADS
CHUNK = 128
N_EXPERTS = 32
TOP_K = 4
D_EXPERT = D_MODEL
SWIGLU_ALPHA = 1.702
SWIGLU_LIMIT = 7.0
EXPERT_BLOCK = 128
EPS = 1e-6

kernel_name = 'hymba_style_rglru_sgu_moe_adaln_layer'


def rmsnorm(x, g):
    xf = x.astype(jnp.float32)
    y = xf * lax.rsqrt(jnp.mean(xf * xf, axis=-1, keepdims=True) + EPS)
    return (y * g.astype(jnp.float32)).astype(x.dtype)


def layernorm(x, g, b):
    xf = x.astype(jnp.float32)
    mu = jnp.mean(xf, axis=-1, keepdims=True)
    var = jnp.mean(jnp.square(xf - mu), axis=-1, keepdims=True)
    y = (xf - mu) * lax.rsqrt(var + EPS)
    return (y * g.astype(jnp.float32) + b.astype(jnp.float32)).astype(x.dtype)


def rg_lru_group(xa, ya, conv_w, conv_b, wr, br, wi, bi, lam):
    bsz, s, ch = xa.shape
    xc = lax.conv_general_dilated(
        xa, conv_w[:, None, :].astype(xa.dtype), window_strides=(1,),
        padding=[(CONV_WIDTH - 1, 0)], dimension_numbers=('NWC', 'WIO', 'NWC'),
        feature_group_count=ch) + conv_b.astype(xa.dtype)
    xh = xc.reshape(bsz, s, LRU_HEADS, LRU_HEAD_DIM)
    rec = jnp.einsum('bshi,hio->bsho', xh, wr.astype(xa.dtype)).reshape(bsz, s, ch)
    inp = jnp.einsum('bshi,hio->bsho', xh, wi.astype(xa.dtype)).reshape(bsz, s, ch)
    r_gate = jax.nn.sigmoid(rec.astype(jnp.float32) + br.astype(jnp.float32))
    i_gate = jax.nn.sigmoid(inp.astype(jnp.float32) + bi.astype(jnp.float32))
    log_a = -LRU_C * r_gate * jax.nn.softplus(-lam.astype(jnp.float32))
    a = jnp.exp(log_a)
    mult = jnp.sqrt(-jnp.expm1(2.0 * log_a))
    bterm = mult * i_gate * xc.astype(jnp.float32)

    def combine(left, right):
        a1, b1 = left
        a2, b2 = right
        return a1 * a2, a2 * b1 + b2

    _, h = lax.associative_scan(combine, (a, bterm), axis=1)
    return (h * ya.astype(jnp.float32)).astype(xa.dtype)


def sgu_group(u, v, ln_g, ln_b, w_s, b_s):
    bsz, s, ch = v.shape
    v = layernorm(v, ln_g, ln_b)
    vh = v.reshape(bsz, s // CHUNK, CHUNK, SGU_HEADS, SGU_HEAD_DIM)
    causal = jnp.tril(jnp.ones((CHUNK, CHUNK), dtype=bool))
    wm = jnp.where(causal[None], w_s, 0.0).astype(v.dtype)
    mixed = jnp.einsum('hts,bnshe->bnthe', wm, vh) + b_s.T.astype(v.dtype)[None, None, :, :, None]
    return u * mixed.reshape(bsz, s, ch)


def moe(h, router_w, router_b, w_gu, b_gu, w_down, b_down):
    bsz, s, d = h.shape
    n_tok = bsz * s
    hf = h.reshape(n_tok, d)
    logits = (hf @ router_w).astype(jnp.float32) + router_b.astype(jnp.float32)
    top_logit, top_e = lax.top_k(logits, TOP_K)
    gates = jax.nn.softmax(top_logit, axis=-1)
    n_assign = n_tok * TOP_K
    flat_e = top_e.reshape(n_assign).astype(jnp.int32)
    order = jnp.argsort(flat_e)
    sorted_e = flat_e[order]
    sorted_tok = (order // TOP_K).astype(jnp.int32)
    sorted_gate = gates.reshape(n_assign)[order]
    counts = jnp.bincount(flat_e, length=N_EXPERTS).astype(jnp.int32)
    padded = (counts + EXPERT_BLOCK - 1) // EXPERT_BLOCK * EXPERT_BLOCK
    pad_end = jnp.cumsum(padded)
    pad_start = pad_end - padded
    grp_start = jnp.cumsum(counts) - counts
    dest = pad_start[sorted_e] + jnp.arange(n_assign, dtype=jnp.int32) - grp_start[sorted_e]
    n_rows = n_assign + N_EXPERTS * EXPERT_BLOCK
    n_blocks = n_rows // EXPERT_BLOCK
    row_tok = jnp.full((n_rows,), n_tok, jnp.int32).at[dest].set(sorted_tok)
    row_gate = jnp.zeros((n_rows,), jnp.float32).at[dest].set(sorted_gate)
    block_start = jnp.arange(n_blocks, dtype=jnp.int32) * EXPERT_BLOCK
    block_e = jnp.minimum(jnp.searchsorted(pad_end, block_start, side='right'), N_EXPERTS - 1)
    hf_pad = jnp.concatenate([hf, jnp.zeros((1, d), hf.dtype)], axis=0)
    xs = hf_pad[row_tok].reshape(n_blocks, EXPERT_BLOCK, d)

    def expert_block(args):
        xb, e = args
        gu = xb @ w_gu[e] + b_gu[e]
        g, lin = jnp.split(gu, 2, axis=-1)
        g = jnp.minimum(g, SWIGLU_LIMIT)
        lin = jnp.clip(lin, -SWIGLU_LIMIT, SWIGLU_LIMIT)
        act = g * jax.nn.sigmoid(SWIGLU_ALPHA * g) * (lin + 1.0)
        return act @ w_down[e] + b_down[e]

    ys = lax.map(expert_block, (xs, block_e)).reshape(n_rows, d)
    ys = ys.astype(jnp.float32) * row_gate[:, None]
    out = jnp.zeros((n_tok + 1, d), jnp.float32).at[row_tok].add(ys)[:n_tok]
    return out.reshape(bsz, s, d).astype(h.dtype)


def setup_inputs(seed: int = 0) -> dict:
    key = jax.random.key(seed)
    ks = jax.random.split(key, 32)
    f32 = jnp.float32
    L = DEPTH

    def nrm(k, shape, scale):
        return jax.random.normal(k, shape, f32) * scale

    def gain(k, shape):
        return 1.0 + 0.1 * jax.random.normal(k, shape, f32)

    a0 = jax.random.uniform(ks[12], (L, D_LRU), f32, 0.9, 0.999)
    a_base = a0 ** (1.0 / LRU_C)
    lru_lambda = jnp.log(a_base) - jnp.log1p(-a_base)
    return {
        'x': nrm(ks[0], (BATCH, SEQ, D_MODEL), 1.0),
        'c': nrm(ks[1], (BATCH, D_MODEL), 1.0),
        'ada_w': nrm(ks[2], (L, D_MODEL, 6 * D_MODEL), 0.5 * D_MODEL ** -0.5),
        'ada_b': nrm(ks[3], (L, 6 * D_MODEL), 0.02),
        'norm1_g': gain(ks[4], (L, D_MODEL)),
        'w_in': nrm(ks[5], (L, D_MODEL, 2 * D_MIX), D_MODEL ** -0.5),
        'conv_w': nrm(ks[6], (L, CONV_WIDTH, D_LRU), CONV_WIDTH ** -0.5),
        'conv_b': nrm(ks[7], (L, D_LRU), 0.02),
        'lru_wr': nrm(ks[8], (L, LRU_HEADS, LRU_HEAD_DIM, LRU_HEAD_DIM), LRU_HEAD_DIM ** -0.5),
        'lru_br': nrm(ks[9], (L, D_LRU), 0.1),
        'lru_wi': nrm(ks[10], (L, LRU_HEADS, LRU_HEAD_DIM, LRU_HEAD_DIM), LRU_HEAD_DIM ** -0.5),
        'lru_bi': nrm(ks[11], (L, D_LRU), 0.1),
        'lru_lambda': lru_lambda,
        'sgu_ln_g': gain(ks[13], (L, D_SGU)),
        'sgu_ln_b': nrm(ks[14], (L, D_SGU), 0.02),
        'sgu_w': nrm(ks[15], (L, SGU_HEADS, CHUNK, CHUNK), 0.5 * CHUNK ** -0.5),
        'sgu_b': 1.0 + nrm(ks[16], (L, SGU_HEADS, CHUNK), 0.1),
        'gnorm_lru_g': gain(ks[17], (L, D_LRU)),
        'gnorm_sgu_g': gain(ks[18], (L, D_SGU)),
        'w_out': nrm(ks[19], (L, D_MIX, D_MODEL), D_MIX ** -0.5),
        'norm2_g': gain(ks[20], (L, D_MODEL)),
        'router_w': nrm(ks[21], (L, D_MODEL, N_EXPERTS), D_MODEL ** -0.5),
        'router_b': nrm(ks[22], (L, N_EXPERTS), 0.01),
        'exp_w_gu': nrm(ks[23], (L, N_EXPERTS, D_MODEL, 2 * D_EXPERT), D_MODEL ** -0.5),
        'exp_b_gu': nrm(ks[24], (L, N_EXPERTS, 2 * D_EXPERT), 0.02),
        'exp_w_down': nrm(ks[25], (L, N_EXPERTS, D_EXPERT, D_MODEL), D_EXPERT ** -0.5),
        'exp_b_down': nrm(ks[26], (L, N_EXPERTS, D_MODEL), 0.02),
        'final_g': gain(ks[27], (D_MODEL,)),
    }


def reference(x, c, ada_w, ada_b, norm1_g, w_in, conv_w, conv_b, lru_wr, lru_br, lru_wi, lru_bi,
              lru_lambda, sgu_ln_g, sgu_ln_b, sgu_w, sgu_b, gnorm_lru_g, gnorm_sgu_g, w_out,
              norm2_g, router_w, router_b, exp_w_gu, exp_b_gu, exp_w_down, exp_b_down, final_g):
    c_act = jax.nn.silu(c)
    h = x
    for l in range(DEPTH):
        mod = c_act @ ada_w[l] + ada_b[l]
        sh1, sc1, g1, sh2, sc2, g2 = jnp.split(mod[:, None, :], 6, axis=-1)
        z = rmsnorm(h, norm1_g[l]) * (1.0 + sc1) + sh1
        proj = z @ w_in[l]
        xa, ya, u, v = jnp.split(proj, [D_LRU, 2 * D_LRU, 2 * D_LRU + D_SGU], axis=-1)
        o_lru = rg_lru_group(xa, jax.nn.gelu(ya), conv_w[l], conv_b[l], lru_wr[l], lru_br[l],
                             lru_wi[l], lru_bi[l], lru_lambda[l])
        o_sgu = sgu_group(jax.nn.gelu(u), jax.nn.gelu(v), sgu_ln_g[l], sgu_ln_b[l], sgu_w[l], sgu_b[l])
        heads = jnp.concatenate([rmsnorm(o_lru, gnorm_lru_g[l]), rmsnorm(o_sgu, gnorm_sgu_g[l])], axis=-1)
        h = h + g1 * (heads @ w_out[l])
        z2 = rmsnorm(h, norm2_g[l]) * (1.0 + sc2) + sh2
        h = h + g2 * moe(z2, router_w[l], router_b[l], exp_w_gu[l], exp_b_gu[l], exp_w_down[l], exp_b_down[l])
    return rmsnorm(h, final_g)
```

```python
import functools

import jax
import jax.numpy as jnp
from jax import lax
from jax.experimental import pallas as pl
from jax.experimental.pallas import tpu as pltpu

F32 = jnp.float32
BF16 = jnp.bfloat16
I32 = jnp.int32
U32 = jnp.uint32

D_MODEL = 1024
D_LRU = 512
D_SGU = 512
LRU_HEADS = 8
LRU_HEAD_DIM = D_LRU // LRU_HEADS
CONV_WIDTH = 4
LRU_C = 8.0
SGU_HEADS = 8
SGU_HEAD_DIM = D_SGU // SGU_HEADS
CHUNK = 128
N_EXPERTS = 32
TOP_K = 4
SWIGLU_ALPHA = 1.702
SWIGLU_LIMIT = 7.0
EPS = 1e-6

SEQ_TILE = 512
ROW_BLOCK = 256
FINAL_TILE = 512
VMEM_LIMIT_BYTES = 56 * 1024 * 1024


def _sigmoid(x):
    return 1.0 / (1.0 + jnp.exp(-x))


def _gelu_tanh(x):
    return 0.5 * x * (1.0 + jnp.tanh(0.7978845608028654 * (x + 0.044715 * (x * x * x))))


def _rms(x, g):
    ms = jnp.mean(x * x, axis=-1, keepdims=True)
    return x * lax.rsqrt(ms + EPS) * g


def _adaln_kernel(c_ref, w_ref, b_ref, o_ref):
    c = c_ref[...]
    ca = c * _sigmoid(c)
    o_ref[...] = jnp.dot(ca.astype(BF16), w_ref[...].astype(BF16), preferred_element_type=F32) + b_ref[...]


def _adaln_call(c, w, b):
    bsz, d = c.shape
    n_out = w.shape[1]
    return pl.pallas_call(
        _adaln_kernel,
        grid=(n_out // d,),
        in_specs=[
            pl.BlockSpec((bsz, d), lambda j: (0, 0)),
            pl.BlockSpec((d, d), lambda j: (0, j)),
            pl.BlockSpec((1, d), lambda j: (0, j)),
        ],
        out_specs=pl.BlockSpec((bsz, d), lambda j: (0, j)),
        out_shape=jax.ShapeDtypeStruct((bsz, n_out), F32),
        name="adaln",
    )(c, w, b.reshape(1, n_out))


def _linear_scan(a, b):
    t = a.shape[0]
    row = lax.broadcasted_iota(I32, (t, 1), 0)
    d = 1
    while d < t:
        a_s = pltpu.roll(a, d, 0)
        b_s = pltpu.roll(b, d, 0)
        m = row >= d
        b = jnp.where(m, a * b_s + b, b)
        a = jnp.where(m, a * a_s, a)
        d *= 2
    return a, b


def _mixer_kernel(x_ref, mod_ref, n1g_ref, win_ref, convw_ref, convb_ref, wgate_ref, bgate_ref, lam_ref,
                  lng_ref, lnb_ref, sguw_ref, sgub_ref, gl_ref, gs_ref, wout_ref, n2g_ref, rwt_ref, rb_ref,
                  h1_ref, z2p_ref, tope_ref, gate_ref, rank_ref, cnt_ref,
                  xa_tail, h_carry):
    t = x_ref.shape[0]
    bi = pl.program_id(0)
    ji = pl.program_id(1)

    @pl.when(ji == 0)
    def _():
        xa_tail[...] = jnp.zeros_like(xa_tail)
        h_carry[...] = jnp.zeros_like(h_carry)

    @pl.when((bi == 0) & (ji == 0))
    def _():
        cnt_ref[...] = jnp.zeros_like(cnt_ref)

    mod = mod_ref[...]
    sh1, sc1, g1 = mod[0:1], mod[1:2], mod[2:3]
    sh2, sc2 = mod[3:4], mod[4:5]

    x = x_ref[...]
    z = _rms(x, n1g_ref[...]) * (1.0 + sc1) + sh1
    proj = jnp.dot(z.astype(BF16), win_ref[...], preferred_element_type=F32)
    xa = proj[:, 0:D_LRU]
    ya = proj[:, D_LRU:2 * D_LRU]
    u = proj[:, 2 * D_LRU:2 * D_LRU + D_SGU]
    v = proj[:, 2 * D_LRU + D_SGU:]

    tail = xa_tail[...]
    row8 = lax.broadcasted_iota(I32, (8, 1), 0)
    xc = xa * convw_ref[CONV_WIDTH - 1:CONV_WIDTH, :] + convb_ref[...]
    for sft in range(1, CONV_WIDTH):
        rolled = pltpu.roll(xa, sft, 0)
        head = jnp.where(row8 < sft, pltpu.roll(tail, sft, 0), rolled[0:8])
        shifted = jnp.concatenate([head, rolled[8:]], axis=0)
        xc = xc + shifted * convw_ref[CONV_WIDTH - 1 - sft:CONV_WIDTH - sft, :]
    xa_tail[...] = xa[t - 8:t]

    gates = jnp.dot(xc.astype(BF16), wgate_ref[...], preferred_element_type=F32) + bgate_ref[...]
    r_gate = _sigmoid(gates[:, 0:D_LRU])
    i_gate = _sigmoid(gates[:, D_LRU:])
    nlam = -lam_ref[...]
    softplus = jnp.maximum(nlam, 0.0) + jnp.log1p(jnp.exp(-jnp.abs(nlam)))
    log_a = (-LRU_C) * r_gate * softplus
    a = jnp.exp(log_a)
    mult = jnp.sqrt(-jnp.tanh(log_a) * (a * a + 1.0))
    bterm = mult * i_gate * xc
    a_cum, h_loc = _linear_scan(a, bterm)
    h = h_loc + a_cum * h_carry[...]
    h_carry[...] = h[t - 1:t]
    o_lru = _rms(h * _gelu_tanh(ya), gl_ref[...])

    ug = _gelu_tanh(u)
    vg = _gelu_tanh(v)
    mu = jnp.mean(vg, axis=-1, keepdims=True)
    vcen = vg - mu
    var = jnp.mean(vcen * vcen, axis=-1, keepdims=True)
    vn = (vcen * lax.rsqrt(var + EPS) * lng_ref[...] + lnb_ref[...]).astype(BF16)
    ri = lax.broadcasted_iota(I32, (CHUNK, CHUNK), 0)
    ci = lax.broadcasted_iota(I32, (CHUNK, CHUNK), 1)
    causal = ri >= ci
    lane = lax.broadcasted_iota(I32, (1, 2 * SGU_HEAD_DIM), 1)
    first_half = lane < SGU_HEAD_DIM
    pair_w = []
    for p in range(SGU_HEADS // 2):
        w0 = jnp.where(causal, sguw_ref[2 * p], 0.0).astype(BF16)
        w1 = jnp.where(causal, sguw_ref[2 * p + 1], 0.0).astype(BF16)
        pair_w.append(jnp.concatenate([w0, w1], axis=1))
    chunks = []
    zero = jnp.zeros((), BF16)
    for n in range(t // CHUNK):
        cols = []
        for p in range(SGU_HEADS // 2):
            blk = vn[n * CHUNK:(n + 1) * CHUNK, p * 128:(p + 1) * 128]
            rhs = jnp.concatenate([jnp.where(first_half, blk, zero), jnp.where(first_half, zero, blk)], axis=0)
            cols.append(jnp.dot(pair_w[p], rhs, preferred_element_type=F32))
        chunks.append(jnp.concatenate(cols, axis=1) + sgub_ref[...])
    mixed = jnp.concatenate(chunks, axis=0)
    o_sgu = _rms(ug * mixed, gs_ref[...])

    heads = jnp.concatenate([o_lru, o_sgu], axis=1).astype(BF16)
    h1 = x + g1 * jnp.dot(heads, wout_ref[...], preferred_element_type=F32)
    h1_ref[...] = h1

    z2 = _rms(h1, n2g_ref[...]) * (1.0 + sc2) + sh2
    zb = z2.astype(BF16).astype(F32)
    bits = lax.bitcast_convert_type(zb, U32)
    half = D_MODEL // 2
    z2p_ref[...] = (bits[:, 0:half] >> 16) | (bits[:, half:] & jnp.uint32(0xFFFF0000))

    logits = lax.dot_general(rwt_ref[...], z2, (((1,), (1,)), ((), ())),
                             precision=lax.Precision.HIGHEST, preferred_element_type=F32) + rb_ref[...]
    eidx = lax.broadcasted_iota(I32, (N_EXPERTS, t), 0)
    work = logits
    sel = []
    tops = []
    for k in range(TOP_K):
        m = jnp.max(work, axis=0, keepdims=True)
        idx = jnp.min(jnp.where(work == m, eidx, N_EXPERTS), axis=0, keepdims=True)
        onehot = eidx == idx
        sel.append(onehot)
        tops.append(m)
        tope_ref[k:k + 1, :] = idx
        work = jnp.where(onehot, -jnp.inf, work)
    exps = [jnp.exp(tk - tops[0]) for tk in tops]
    denom = exps[0] + exps[1] + exps[2] + exps[3]
    for k in range(TOP_K):
        gate_ref[k:k + 1, :] = exps[k] / denom

    chosen = jnp.where(sel[0] | sel[1] | sel[2] | sel[3], 1.0, 0.0)
    si = lax.broadcasted_iota(I32, (t, t), 0)
    ti = lax.broadcasted_iota(I32, (t, t), 1)
    before = jnp.where(si < ti, 1.0, 0.0).astype(BF16)
    excl = jnp.dot(chosen.astype(BF16), before, preferred_element_type=F32)
    base = cnt_ref[:, 0:1].astype(F32)
    pos = excl + base
    for k in range(TOP_K):
        rank_ref[k:k + 1, :] = jnp.sum(jnp.where(sel[k], pos, 0.0), axis=0, keepdims=True).astype(I32)
    cnt_ref[...] = cnt_ref[...] + jnp.sum(chosen, axis=1, keepdims=True).astype(I32)


def _mixer_call(x, mod3, n1g, win, convw, convb, wgate, bgate, lam, lng, lnb, sguw, sgub_full, gl, gs, wout,
                n2g, rwt, rb):
    bsz, seq, d = x.shape
    t = SEQ_TILE
    n_tok = bsz * seq
    tiles = seq // t

    def const(shape):
        return pl.BlockSpec(shape, lambda b, j: (0,) * len(shape))

    tok_map = lambda b, j: (b * tiles + j, 0)
    rt_map = lambda b, j: (0, b * tiles + j)
    return pl.pallas_call(
        _mixer_kernel,
        grid=(bsz, tiles),
        in_specs=[
            pl.BlockSpec((None, t, d), lambda b, j: (b, j, 0)),
            pl.BlockSpec((None, 6, d), lambda b, j: (b, 0, 0)),
            const((1, d)),
            const((d, 2 * d)),
            const((CONV_WIDTH, D_LRU)),
            const((1, D_LRU)),
            const((D_LRU, 2 * D_LRU)),
            const((1, 2 * D_LRU)),
            const((1, D_LRU)),
            const((1, D_SGU)),
            const((1, D_SGU)),
            const((SGU_HEADS, CHUNK, CHUNK)),
            const((CHUNK, D_SGU)),
            const((1, D_LRU)),
            const((1, D_SGU)),
            const((d, d)),
            const((1, d)),
            const((N_EXPERTS, d)),
            const((N_EXPERTS, 1)),
        ],
        out_specs=[
            pl.BlockSpec((t, d), tok_map),
            pl.BlockSpec((t, d // 2), tok_map),
            pl.BlockSpec((TOP_K, t), rt_map),
            pl.BlockSpec((TOP_K, t), rt_map),
            pl.BlockSpec((TOP_K, t), rt_map),
            pl.BlockSpec((N_EXPERTS, 128), lambda b, j: (0, 0)),
        ],
        out_shape=[
            jax.ShapeDtypeStruct((n_tok, d), F32),
            jax.ShapeDtypeStruct((n_tok, d // 2), U32),
            jax.ShapeDtypeStruct((TOP_K, n_tok), I32),
            jax.ShapeDtypeStruct((TOP_K, n_tok), F32),
            jax.ShapeDtypeStruct((TOP_K, n_tok), I32),
            jax.ShapeDtypeStruct((N_EXPERTS, 128), I32),
        ],
        scratch_shapes=[pltpu.VMEM((8, D_LRU), F32), pltpu.VMEM((1, D_LRU), F32)],
        compiler_params=pltpu.CompilerParams(
            dimension_semantics=("arbitrary", "arbitrary"), vmem_limit_bytes=VMEM_LIMIT_BYTES),
        name="mixer_router",
    )(x, mod3, n1g, win, convw, convb, wgate, bgate, lam, lng, lnb, sguw, sgub_full, gl, gs, wout, n2g, rwt, rb)


def _expert_kernel(be_ref, first_ref, nval_ref, nb_ref,
                   tokc_ref, tokn_ref, ydst_ref, gate_ref, z2p_hbm, wgu_ref, bgu_ref, wd_ref, bd_ref,
                   ytok_hbm,
                   xbuf, ybuf, wgu_bf, wd_bf, gsem, ssem):
    r = xbuf.shape[1]
    b = pl.program_id(0)
    nb = nb_ref[0]
    slot = b & 1

    def gather_row(tok_ref, dst_slot, i):
        return pltpu.make_async_copy(z2p_hbm.at[pl.ds(tok_ref[0, i], 1), :],
                                     xbuf.at[dst_slot, pl.ds(i, 1), :], gsem.at[dst_slot])

    def scatter_row(src_slot, i):
        return pltpu.make_async_copy(ybuf.at[src_slot, pl.ds(i, 1), :],
                                     ytok_hbm.at[pl.ds(ydst_ref[0, i], 1), :], ssem.at[src_slot])

    def issue_gather(tok_ref, dst_slot):
        def body(i, c):
            gather_row(tok_ref, dst_slot, i).start()
            return c
        lax.fori_loop(0, r, body, 0, unroll=8)

    def wait_gather(s):
        pltpu.make_async_copy(z2p_hbm.at[pl.ds(0, r), :], xbuf.at[s], gsem.at[s]).wait()

    def wait_scatter(s, n):
        n8 = pl.multiple_of((n >> 3) << 3, 8)

        @pl.when(n8 > 0)
        def _():
            pltpu.make_async_copy(ybuf.at[s, pl.ds(0, n8), :], ytok_hbm.at[pl.ds(0, n8), :], ssem.at[s]).wait()

        def body(i, c):
            pltpu.make_async_copy(ybuf.at[s, pl.ds(0, 1), :], ytok_hbm.at[pl.ds(0, 1), :], ssem.at[s]).wait()
            return c
        lax.fori_loop(0, n & 7, body, 0)

    @pl.when(b == 0)
    def _():
        issue_gather(tokc_ref, 0)

    @pl.when(b < nb)
    def _():
        wait_gather(slot)

        @pl.when(b + 1 < nb)
        def _():
            issue_gather(tokn_ref, 1 - slot)

        @pl.when(first_ref[b] == 1)
        def _():
            wgu_bf[...] = wgu_ref[...].astype(BF16)
            wd_bf[...] = wd_ref[...].astype(BF16)

        xw = xbuf[slot]
        lo = lax.bitcast_convert_type(xw << 16, F32).astype(BF16)
        hi = lax.bitcast_convert_type(xw & jnp.uint32(0xFFFF0000), F32).astype(BF16)
        xb = jnp.concatenate([lo, hi], axis=1)
        gu = jnp.dot(xb, wgu_bf[...], preferred_element_type=F32) + bgu_ref[...]
        g = jnp.minimum(gu[:, 0:D_MODEL], SWIGLU_LIMIT)
        lin = jnp.clip(gu[:, D_MODEL:], -SWIGLU_LIMIT, SWIGLU_LIMIT)
        act = g * _sigmoid(SWIGLU_ALPHA * g) * (lin + 1.0)
        y = jnp.dot(act.astype(BF16), wd_bf[...], preferred_element_type=F32) + bd_ref[...]

        @pl.when(b >= 2)
        def _():
            wait_scatter(slot, nval_ref[jnp.maximum(b - 2, 0)])

        ybuf[slot] = y * gate_ref[...]

        def body(i, c):
            scatter_row(slot, i).start()
            return c
        lax.fori_loop(0, nval_ref[b], body, 0)

        @pl.when(b == nb - 1)
        def _():
            wait_scatter(slot, nval_ref[b])

            @pl.when(nb >= 2)
            def _():
                wait_scatter(1 - slot, nval_ref[jnp.maximum(b - 1, 0)])


def _expert_call(block_e, first, nval, nb_used, row_tok, row_ydst, row_gate, z2p, w_gu, b_gu, w_down, b_down,
                 n_ytok):
    nblk, _, r = row_tok.shape
    d = D_MODEL
    grid_spec = pltpu.PrefetchScalarGridSpec(
        num_scalar_prefetch=4,
        grid=(nblk,),
        in_specs=[
            pl.BlockSpec((None, 1, r), lambda b, be, fi, nv, nb: (b, 0, 0), memory_space=pltpu.SMEM),
            pl.BlockSpec((None, 1, r), lambda b, be, fi, nv, nb: (jnp.minimum(b + 1, nblk - 1), 0, 0),
                         memory_space=pltpu.SMEM),
            pl.BlockSpec((None, 1, r), lambda b, be, fi, nv, nb: (b, 0, 0), memory_space=pltpu.SMEM),
            pl.BlockSpec((None, r, 1), lambda b, be, fi, nv, nb: (b, 0, 0)),
            pl.BlockSpec(memory_space=pl.ANY),
            pl.BlockSpec((None, d, 2 * d), lambda b, be, fi, nv, nb: (be[b], 0, 0)),
            pl.BlockSpec((None, 1, 2 * d), lambda b, be, fi, nv, nb: (be[b], 0, 0)),
            pl.BlockSpec((None, d, d), lambda b, be, fi, nv, nb: (be[b], 0, 0)),
            pl.BlockSpec((None, 1, d), lambda b, be, fi, nv, nb: (be[b], 0, 0)),
        ],
        out_specs=pl.BlockSpec(memory_space=pl.ANY),
        scratch_shapes=[
            pltpu.VMEM((2, r, d // 2), U32),
            pltpu.VMEM((2, r, d), F32),
            pltpu.VMEM((d, 2 * d), BF16),
            pltpu.VMEM((d, d), BF16),
            pltpu.SemaphoreType.DMA((2,)),
            pltpu.SemaphoreType.DMA((2,)),
        ],
    )
    return pl.pallas_call(
        _expert_kernel,
        grid_spec=grid_spec,
        out_shape=jax.ShapeDtypeStruct((n_ytok, d), F32),
        compiler_params=pltpu.CompilerParams(
            dimension_semantics=("arbitrary",), vmem_limit_bytes=VMEM_LIMIT_BYTES),
        name="experts",
    )(block_e, first, nval, nb_used, row_tok, row_tok, row_ydst, row_gate, z2p, w_gu, b_gu, w_down, b_down)


def _final_kernel(h1_ref, mod_ref, y0_ref, y1_ref, y2_ref, y3_ref, fg_ref, o_ref):
    g2 = mod_ref[5:6, :]
    moe = (y0_ref[...] + y1_ref[...]) + (y2_ref[...] + y3_ref[...])
    o_ref[...] = _rms(h1_ref[...] + g2 * moe, fg_ref[...])


def _final_call(h1, mod3, ytok, final_g, bsz, seq):
    n_tok, d = h1.shape
    t = FINAL_TILE
    tiles = n_tok // t
    per_batch = seq // t
    y_specs = [pl.BlockSpec((t, d), functools.partial(lambda i, k: (k * tiles + i, 0), k=k)) for k in range(TOP_K)]
    out = pl.pallas_call(
        _final_kernel,
        grid=(tiles,),
        in_specs=[
            pl.BlockSpec((t, d), lambda i: (i, 0)),
            pl.BlockSpec((None, 6, d), lambda i: (i // per_batch, 0, 0)),
            *y_specs,
            pl.BlockSpec((1, d), lambda i: (0, 0)),
        ],
        out_specs=pl.BlockSpec((t, d), lambda i: (i, 0)),
        out_shape=jax.ShapeDtypeStruct((n_tok, d), F32),
        compiler_params=pltpu.CompilerParams(
            dimension_semantics=("arbitrary",), vmem_limit_bytes=VMEM_LIMIT_BYTES),
        name="combine_final",
    )(h1, mod3, ytok, ytok, ytok, ytok, final_g.reshape(1, d))
    return out.reshape(bsz, seq, d)


def _block_diag(w):
    h, i, o = w.shape
    eye = jnp.eye(h, dtype=w.dtype)
    return (w[:, :, None, :] * eye[:, None, :, None]).reshape(h * i, h * o)


def _route_tables(top_e, rank, gate, counts):
    r = ROW_BLOCK
    n_tok = top_e.shape[1]
    n_assign = TOP_K * n_tok
    nblk = n_assign // r + N_EXPERTS
    n_rows = nblk * r
    padded = (counts + r - 1) // r * r
    pad_end = jnp.cumsum(padded)
    pad_start = pad_end - padded
    dest = pad_start[top_e] + rank
    tok = jnp.broadcast_to(jnp.arange(n_tok, dtype=I32)[None, :], dest.shape)
    ydst = tok + jnp.arange(TOP_K, dtype=I32)[:, None] * n_tok
    flat = dest.reshape(-1)
    row_tok = jnp.zeros((n_rows,), I32).at[flat].set(tok.reshape(-1), unique_indices=True)
    row_ydst = jnp.zeros((n_rows,), I32).at[flat].set(ydst.reshape(-1), unique_indices=True)
    row_gate = jnp.zeros((n_rows,), F32).at[flat].set(gate.reshape(-1), unique_indices=True)
    nb_used = pad_end[-1] // r
    blk = jnp.arange(nblk, dtype=I32)
    be = jnp.minimum(jnp.searchsorted(pad_end, blk * r, side='right'), N_EXPERTS - 1).astype(I32)
    nval = jnp.where(blk < nb_used, jnp.clip(pad_start[be] + counts[be] - blk * r, 0, r), 0).astype(I32)
    last_e = be[jnp.maximum(nb_used - 1, 0)]
    be = jnp.where(blk < nb_used, be, last_e)
    first = jnp.concatenate([jnp.ones((1,), I32), (be[1:] != be[:-1]).astype(I32)])
    return (be, first, nval, nb_used.reshape(1).astype(I32), row_tok.reshape(nblk, 1, r),
            row_ydst.reshape(nblk, 1, r), row_gate.reshape(nblk, r, 1), n_assign)


def kernel(x, c, ada_w, ada_b, norm1_g, w_in, conv_w, conv_b, lru_wr, lru_br, lru_wi, lru_bi, lru_lambda, sgu_ln_g, sgu_ln_b, sgu_w, sgu_b, gnorm_lru_g, gnorm_sgu_g, w_out, norm2_g, router_w, router_b, exp_w_gu, exp_b_gu, exp_w_down, exp_b_down, final_g):
    bsz, seq, d = x.shape
    depth = ada_w.shape[0]
    assert depth == 1 and d == D_MODEL and seq % SEQ_TILE == 0 and (bsz * seq) % FINAL_TILE == 0
    l = 0
    mod = _adaln_call(c, ada_w[l], ada_b[l])
    mod3 = mod.reshape(bsz, 6, d)

    row = lambda v: v.reshape(1, -1)
    wgate = jnp.concatenate([_block_diag(lru_wr[l]), _block_diag(lru_wi[l])], axis=1).astype(BF16)
    bgate = jnp.concatenate([lru_br[l], lru_bi[l]]).reshape(1, -1)
    sgub_full = jnp.repeat(sgu_b[l].T, SGU_HEAD_DIM, axis=1)
    h1, z2p, top_e, gate, rank, cnt = _mixer_call(
        x, mod3, row(norm1_g[l]), w_in[l].astype(BF16), conv_w[l], row(conv_b[l]), wgate, bgate,
        row(lru_lambda[l]), row(sgu_ln_g[l]), row(sgu_ln_b[l]), sgu_w[l], sgub_full,
        row(gnorm_lru_g[l]), row(gnorm_sgu_g[l]), w_out[l].astype(BF16), row(norm2_g[l]),
        router_w[l].T, router_b[l].reshape(-1, 1))

    be, first, nval, nb_used, row_tok, row_ydst, row_gate, n_ytok = _route_tables(top_e, rank, gate, cnt[:, 0])
    ytok = _expert_call(be, first, nval, nb_used, row_tok, row_ydst, row_gate, z2p,
                        exp_w_gu[l], exp_b_gu[l].reshape(N_EXPERTS, 1, -1),
                        exp_w_down[l], exp_b_down[l].reshape(N_EXPERTS, 1, -1), n_ytok)
    return _final_call(h1, mod3, ytok, final_g, bsz, seq)
```

```python
import jax
import jax.numpy as jnp
from jax import lax
from jax.experimental import pallas as pl
from jax.experimental.pallas import tpu as pltpu

F32 = jnp.float32
BF16 = jnp.bfloat16
I32 = jnp.int32

D_MODEL = 1024
D_LRU = 512
D_SGU = 512
LRU_HEADS = 8
CONV_WIDTH = 4
LRU_C = 8.0
SGU_HEADS = 8
SGU_HEAD_DIM = D_SGU // SGU_HEADS
CHUNK = 128
N_EXPERTS = 32
TOP_K = 4
SWIGLU_ALPHA = 1.702
SWIGLU_LIMIT = 7.0
EPS = 1e-6

LANES = 128
SEQ_TILE = 512
ROW_ALIGN = 8
TILE_ROWS = TOP_K * SEQ_TILE + ROW_ALIGN * N_EXPERTS
PAYLOAD = LANES
ROW_W = D_MODEL + PAYLOAD
ROW_BLOCK = 256
VMEM_LIMIT_BYTES = 56 * 1024 * 1024


def _sigmoid(x):
    return 1.0 / (1.0 + jnp.exp(-x))


def _gelu_tanh(x):
    return 0.5 * x * (1.0 + jnp.tanh(0.7978845608028654 * (x + 0.044715 * (x * x * x))))


def _rms(x, g):
    ms = jnp.mean(x * x, axis=-1, keepdims=True)
    return x * lax.rsqrt(ms + EPS) * g


def _perm_matrix(pos_rows, n_rows, n_cols):
    prow = lax.broadcasted_iota(I32, (n_rows, n_cols), 0)
    acc = jnp.zeros((n_rows, n_cols), F32)
    for pk in pos_rows:
        acc = jnp.where(prow == pk, 1.0, acc)
    return acc.astype(BF16)


def _adaln_kernel(c_ref, w_ref, b_ref, o_ref):
    c = c_ref[...]
    ca = c * _sigmoid(c)
    o_ref[...] = jnp.dot(ca.astype(BF16), w_ref[...].astype(BF16), preferred_element_type=F32) + b_ref[...]


def _adaln_call(c, w, b):
    bsz, d = c.shape
    n_out = w.shape[1]
    return pl.pallas_call(
        _adaln_kernel,
        grid=(n_out // d,),
        in_specs=[
            pl.BlockSpec((bsz, d), lambda j: (0, 0)),
            pl.BlockSpec((d, d), lambda j: (0, j)),
            pl.BlockSpec((1, d), lambda j: (0, j)),
        ],
        out_specs=pl.BlockSpec((bsz, d), lambda j: (0, j)),
        out_shape=jax.ShapeDtypeStruct((bsz, n_out), F32),
        name="adaln",
    )(c, w, b.reshape(1, n_out))


def _linear_scan(a, b):
    t = a.shape[0]
    row = lax.broadcasted_iota(I32, (t, 1), 0)
    d = 1
    while d < t:
        a_s = pltpu.roll(a, d, 0)
        b_s = pltpu.roll(b, d, 0)
        m = row >= d
        b = jnp.where(m, a * b_s + b, b)
        a = jnp.where(m, a * a_s, a)
        d *= 2
    return a, b


def _mixer_kernel(x_ref, mod_ref, n1g_ref, win_ref, convw_ref, convb_ref, wgate_ref, bgate_ref, lam_ref,
                  lng_ref, lnb_ref, sguw_ref, sgub_ref, gl_ref, gs_ref, wout_ref, n2g_ref, rwt_ref, rb_ref,
                  h1_ref, xs_ref, pos_ref, cnt_ref,
                  xa_tail, h_carry):
    t = x_ref.shape[0]
    ji = pl.program_id(1)

    @pl.when(ji == 0)
    def _():
        xa_tail[...] = jnp.zeros_like(xa_tail)
        h_carry[...] = jnp.zeros_like(h_carry)

    mod = mod_ref[...]
    sh1, sc1, g1 = mod[0:1], mod[1:2], mod[2:3]
    sh2, sc2 = mod[3:4], mod[4:5]

    x = x_ref[...]
    z = _rms(x, n1g_ref[...]) * (1.0 + sc1) + sh1
    proj = jnp.dot(z.astype(BF16), win_ref[...], preferred_element_type=F32)
    xa = proj[:, 0:D_LRU]
    ya = proj[:, D_LRU:2 * D_LRU]
    u = proj[:, 2 * D_LRU:2 * D_LRU + D_SGU]
    v = proj[:, 2 * D_LRU + D_SGU:]

    tail = xa_tail[...]
    row8 = lax.broadcasted_iota(I32, (8, 1), 0)
    xc = xa * convw_ref[CONV_WIDTH - 1:CONV_WIDTH, :] + convb_ref[...]
    for sft in range(1, CONV_WIDTH):
        rolled = pltpu.roll(xa, sft, 0)
        head = jnp.where(row8 < sft, pltpu.roll(tail, sft, 0), rolled[0:8])
        shifted = jnp.concatenate([head, rolled[8:]], axis=0)
        xc = xc + shifted * convw_ref[CONV_WIDTH - 1 - sft:CONV_WIDTH - sft, :]
    xa_tail[...] = xa[t - 8:t]

    gates = jnp.dot(xc.astype(BF16), wgate_ref[...], preferred_element_type=F32) + bgate_ref[...]
    r_gate = _sigmoid(gates[:, 0:D_LRU])
    i_gate = _sigmoid(gates[:, D_LRU:])
    nlam = -lam_ref[...]
    softplus = jnp.maximum(nlam, 0.0) + jnp.log1p(jnp.exp(-jnp.abs(nlam)))
    log_a = (-LRU_C) * r_gate * softplus
    a = jnp.exp(log_a)
    om = -jnp.tanh(log_a) * (a * a + 1.0)
    mult = jnp.where(om > 0.0, om * lax.rsqrt(om), 0.0)
    bterm = mult * i_gate * xc
    a_cum, h_loc = _linear_scan(a, bterm)
    h = h_loc + a_cum * h_carry[...]
    h_carry[...] = h[t - 1:t]
    o_lru = _rms(h * _gelu_tanh(ya), gl_ref[...])

    ug = _gelu_tanh(u)
    vg = _gelu_tanh(v)
    mu = jnp.mean(vg, axis=-1, keepdims=True)
    vcen = vg - mu
    var = jnp.mean(vcen * vcen, axis=-1, keepdims=True)
    vn = (vcen * lax.rsqrt(var + EPS) * lng_ref[...] + lnb_ref[...]).astype(BF16)
    ri = lax.broadcasted_iota(I32, (CHUNK, CHUNK), 0)
    ci = lax.broadcasted_iota(I32, (CHUNK, CHUNK), 1)
    causal = ri >= ci
    lane = lax.broadcasted_iota(I32, (1, 2 * SGU_HEAD_DIM), 1)
    first_half = lane < SGU_HEAD_DIM
    pair_w = []
    for p in range(SGU_HEADS // 2):
        w0 = jnp.where(causal, sguw_ref[2 * p], 0.0).astype(BF16)
        w1 = jnp.where(causal, sguw_ref[2 * p + 1], 0.0).astype(BF16)
        pair_w.append(jnp.concatenate([w0, w1], axis=1))
    chunks = []
    zero = jnp.zeros((), BF16)
    for n in range(t // CHUNK):
        cols = []
        for p in range(SGU_HEADS // 2):
            blk = vn[n * CHUNK:(n + 1) * CHUNK, p * LANES:(p + 1) * LANES]
            rhs = jnp.concatenate([jnp.where(first_half, blk, zero), jnp.where(first_half, zero, blk)], axis=0)
            cols.append(jnp.dot(pair_w[p], rhs, preferred_element_type=F32))
        chunks.append(jnp.concatenate(cols, axis=1) + sgub_ref[...])
    mixed = jnp.concatenate(chunks, axis=0)
    o_sgu = _rms(ug * mixed, gs_ref[...])

    heads = jnp.concatenate([o_lru, o_sgu], axis=1).astype(BF16)
    h1 = x + g1 * jnp.dot(heads, wout_ref[...], preferred_element_type=F32)
    h1_ref[...] = h1

    z2 = _rms(h1, n2g_ref[...]) * (1.0 + sc2) + sh2
    logits = lax.dot_general(rwt_ref[...], z2, (((1,), (1,)), ((), ())),
                             precision=lax.Precision.HIGHEST, preferred_element_type=F32) + rb_ref[...]
    eidx = lax.broadcasted_iota(I32, (N_EXPERTS, t), 0)
    work = logits
    sel = []
    tops = []
    for k in range(TOP_K):
        m = jnp.max(work, axis=0, keepdims=True)
        idx = jnp.min(jnp.where(work == m, eidx, N_EXPERTS), axis=0, keepdims=True)
        onehot = eidx == idx
        sel.append(onehot)
        tops.append(m)
        work = jnp.where(onehot, -jnp.inf, work)
    exps = [jnp.exp(tk - tops[0]) for tk in tops]
    denom = exps[0] + exps[1] + exps[2] + exps[3]
    chosen = jnp.zeros((N_EXPERTS, t), F32)
    gsel = jnp.zeros((N_EXPERTS, t), F32)
    for k in range(TOP_K):
        chosen = jnp.where(sel[k], 1.0, chosen)
        gsel = jnp.where(sel[k], exps[k] / denom, gsel)

    chosen_b = chosen.astype(BF16)
    si = lax.broadcasted_iota(I32, (t, t), 0)
    ti = lax.broadcasted_iota(I32, (t, t), 1)
    before = jnp.where(si < ti, 1.0, 0.0).astype(BF16)
    excl = jnp.dot(chosen_b, before, preferred_element_type=F32)
    cnt_col = jnp.sum(chosen, axis=1, keepdims=True).astype(I32)
    cnt_ref[...] = jnp.broadcast_to(cnt_col, cnt_ref.shape)
    run_len = ((cnt_col + (ROW_ALIGN - 1)) // ROW_ALIGN * ROW_ALIGN).astype(F32)
    er = lax.broadcasted_iota(I32, (N_EXPERTS, N_EXPERTS), 0)
    ec = lax.broadcasted_iota(I32, (N_EXPERTS, N_EXPERTS), 1)
    lower = jnp.where(ec < er, 1.0, 0.0).astype(BF16)
    run_start = jnp.dot(lower, jnp.broadcast_to(run_len, (N_EXPERTS, LANES)).astype(BF16),
                        preferred_element_type=F32)[:, 0:1]
    posmat = excl + run_start
    pos_rows = []
    for k in range(TOP_K):
        pk = jnp.sum(jnp.where(sel[k], posmat, 0.0), axis=0, keepdims=True).astype(I32)
        pos_ref[k:k + 1, :] = pk
        pos_rows.append(pk)

    perm = _perm_matrix(pos_rows, TILE_ROWS, t)
    xs_ref[:, 0:D_MODEL] = jnp.dot(perm, z2.astype(BF16), preferred_element_type=F32)
    g_hi = gsel.astype(BF16)
    rem = gsel - g_hi.astype(F32)
    g_mid = rem.astype(BF16)
    g_lo = (rem - g_mid.astype(F32)).astype(BF16)
    gp = jnp.concatenate([g_hi, g_mid, g_lo, jnp.zeros((PAYLOAD - 3 * N_EXPERTS, t), BF16)], axis=0)
    xs_ref[:, D_MODEL:] = lax.dot_general(perm, gp, (((1,), (1,)), ((), ())), preferred_element_type=F32)


def _mixer_call(x, mod3, n1g, win, convw, convb, wgate, bgate, lam, lng, lnb, sguw, sgub_full, gl, gs, wout,
                n2g, rwt, rb):
    bsz, seq, d = x.shape
    t = SEQ_TILE
    n_tok = bsz * seq
    tiles = seq // t

    def const(shape):
        return pl.BlockSpec(shape, lambda b, j: (0,) * len(shape))

    return pl.pallas_call(
        _mixer_kernel,
        grid=(bsz, tiles),
        in_specs=[
            pl.BlockSpec((None, t, d), lambda b, j: (b, j, 0)),
            pl.BlockSpec((None, 6, d), lambda b, j: (b, 0, 0)),
            const((1, d)),
            const((d, 2 * d)),
            const((CONV_WIDTH, D_LRU)),
            const((1, D_LRU)),
            const((D_LRU, 2 * D_LRU)),
            const((1, 2 * D_LRU)),
            const((1, D_LRU)),
            const((1, D_SGU)),
            const((1, D_SGU)),
            const((SGU_HEADS, CHUNK, CHUNK)),
            const((CHUNK, D_SGU)),
            const((1, D_LRU)),
            const((1, D_SGU)),
            const((d, d)),
            const((1, d)),
            const((N_EXPERTS, d)),
            const((N_EXPERTS, 1)),
        ],
        out_specs=[
            pl.BlockSpec((t, d), lambda b, j: (b * tiles + j, 0)),
            pl.BlockSpec((TILE_ROWS, ROW_W), lambda b, j: (b * tiles + j, 0)),
            pl.BlockSpec((TOP_K, t), lambda b, j: (0, b * tiles + j)),
            pl.BlockSpec((None, N_EXPERTS, LANES), lambda b, j: (b * tiles + j, 0, 0)),
        ],
        out_shape=[
            jax.ShapeDtypeStruct((n_tok, d), F32),
            jax.ShapeDtypeStruct((n_tok // t * TILE_ROWS, ROW_W), F32),
            jax.ShapeDtypeStruct((TOP_K, n_tok), I32),
            jax.ShapeDtypeStruct((n_tok // t, N_EXPERTS, LANES), I32),
        ],
        scratch_shapes=[pltpu.VMEM((8, D_LRU), F32), pltpu.VMEM((1, D_LRU), F32)],
        compiler_params=pltpu.CompilerParams(
            dimension_semantics=("arbitrary", "arbitrary"), vmem_limit_bytes=VMEM_LIMIT_BYTES),
        name="mixer_router",
    )(x, mod3, n1g, win, convw, convb, wgate, bgate, lam, lng, lnb, sguw, sgub_full, gl, gs, wout, n2g, rwt, rb)


_PIECES = tuple(1 << s for s in range(ROW_BLOCK.bit_length() - 1, ROW_ALIGN.bit_length() - 2, -1))


def _expert_kernel(be_ref, first_ref, nval_ref, jrow_ref, ilo_ref, ihi_ref, nb_ref, cnt_ref, cum_ref, base_ref,
                   tailrow_ref, taillen_ref,
                   xs_hbm, wgu_ref, bgu_ref, wd_ref, bd_ref,
                   ys_hbm,
                   xbuf, ybuf, zbuf, wgu_bf, wd_bf, gsem, ssem, zsem):
    r = xbuf.shape[1]
    n_tiles = tailrow_ref.shape[0]
    b = pl.program_id(0)
    nb = nb_ref[0]
    slot = b & 1

    def tail_copy(row, size):
        return pltpu.make_async_copy(zbuf.at[pl.ds(0, size), :], ys_hbm.at[pl.ds(row, size), :], zsem.at[0])

    def for_each_tail_piece(fn):
        def tile_body(i, c):
            ln = taillen_ref[i]
            for size in _PIECES:
                off = ln & ~(2 * size - 1)

                @pl.when((ln & size) != 0)
                def _():
                    fn(pl.multiple_of(tailrow_ref[i] + off, ROW_ALIGN), size)
            return c
        lax.fori_loop(0, n_tiles, tile_body, 0)

    def for_each_piece(blk, fn):
        e = be_ref[blk]
        j0 = jrow_ref[blk]

        def tile_body(i, c):
            idx = i * N_EXPERTS + e
            cu = cum_ref[idx]
            lo = jnp.maximum(cu, j0)
            hi = jnp.minimum(cu + cnt_ref[idx], j0 + r)
            ln = hi - lo

            @pl.when(ln > 0)
            def _():
                src = base_ref[idx] + (lo - cu)
                dst = lo - j0
                for size in _PIECES:
                    off = ln & ~(2 * size - 1)

                    @pl.when((ln & size) != 0)
                    def _():
                        fn(pl.multiple_of(src + off, ROW_ALIGN), pl.multiple_of(dst + off, ROW_ALIGN), size)
            return c
        lax.fori_loop(ilo_ref[blk], ihi_ref[blk] + 1, tile_body, 0)

    def gather_piece(dst_slot):
        def fn(hbm_row, buf_row, size):
            pltpu.make_async_copy(xs_hbm.at[pl.ds(hbm_row, size), :],
                                  xbuf.at[dst_slot, pl.ds(buf_row, size), :], gsem.at[dst_slot]).start()
        return fn

    def scatter_piece(src_slot):
        def fn(hbm_row, buf_row, size):
            pltpu.make_async_copy(ybuf.at[src_slot, pl.ds(buf_row, size), :],
                                  ys_hbm.at[pl.ds(hbm_row, size), :], ssem.at[src_slot]).start()
        return fn

    def wait_rows(n, make_copy):
        for size in _PIECES:
            @pl.when((n & size) != 0)
            def _():
                make_copy(size).wait()

    def wait_gather(s, n):
        wait_rows(n, lambda size: pltpu.make_async_copy(
            xs_hbm.at[pl.ds(0, size), :], xbuf.at[s, pl.ds(0, size), :], gsem.at[s]))

    def wait_scatter(s, n):
        wait_rows(n, lambda size: pltpu.make_async_copy(
            ybuf.at[s, pl.ds(0, size), :], ys_hbm.at[pl.ds(0, size), :], ssem.at[s]))

    @pl.when(b == 0)
    def _():
        xbuf[...] = jnp.zeros_like(xbuf)
        for_each_piece(0, gather_piece(0))
        zbuf[...] = jnp.zeros_like(zbuf)
        for_each_tail_piece(lambda row, size: tail_copy(row, size).start())

    @pl.when(b < nb)
    def _():
        wait_gather(slot, nval_ref[b])

        @pl.when(b + 1 < nb)
        def _():
            for_each_piece(b + 1, gather_piece(1 - slot))

        @pl.when(first_ref[b] == 1)
        def _():
            wgu_bf[...] = wgu_ref[...].astype(BF16)
            wd_bf[...] = wd_ref[...].astype(BF16)

        xw = xbuf[slot]
        xb = xw[:, 0:D_MODEL].astype(BF16)
        lane = lax.broadcasted_iota(I32, (1, PAYLOAD), 1)
        gate = jnp.sum(jnp.where((lane & (N_EXPERTS - 1)) == be_ref[b], xw[:, D_MODEL:], 0.0),
                       axis=1, keepdims=True)
        gu = jnp.dot(xb, wgu_bf[...], preferred_element_type=F32) + bgu_ref[...]
        g = jnp.minimum(gu[:, 0:D_MODEL], SWIGLU_LIMIT)
        lin = jnp.clip(gu[:, D_MODEL:], -SWIGLU_LIMIT, SWIGLU_LIMIT)
        act = g * _sigmoid(SWIGLU_ALPHA * g) * (lin + 1.0)
        y = jnp.dot(act.astype(BF16), wd_bf[...], preferred_element_type=F32) + bd_ref[...]

        @pl.when(b >= 2)
        def _():
            wait_scatter(slot, nval_ref[jnp.maximum(b - 2, 0)])

        ybuf[slot] = y * gate
        for_each_piece(b, scatter_piece(slot))

        @pl.when(b == nb - 1)
        def _():
            wait_scatter(slot, nval_ref[b])

            @pl.when(nb >= 2)
            def _():
                wait_scatter(1 - slot, nval_ref[jnp.maximum(b - 1, 0)])

            for_each_tail_piece(lambda row, size: tail_copy(0, size).wait())


def _expert_call(tables, xs, w_gu, b_gu, w_down, b_down):
    nblk = tables[0].shape[0]
    r = ROW_BLOCK
    d = D_MODEL
    n_pre = len(tables)
    wmap = lambda b, be, *_: (be[b], 0, 0)
    grid_spec = pltpu.PrefetchScalarGridSpec(
        num_scalar_prefetch=n_pre,
        grid=(nblk,),
        in_specs=[
            pl.BlockSpec(memory_space=pl.ANY),
            pl.BlockSpec((None, d, 2 * d), wmap),
            pl.BlockSpec((None, 1, 2 * d), wmap),
            pl.BlockSpec((None, d, d), wmap),
            pl.BlockSpec((None, 1, d), wmap),
        ],
        out_specs=pl.BlockSpec(memory_space=pl.ANY),
        scratch_shapes=[
            pltpu.VMEM((2, r, ROW_W), F32),
            pltpu.VMEM((2, r, d), F32),
            pltpu.VMEM((ROW_ALIGN * N_EXPERTS, d), F32),
            pltpu.VMEM((d, 2 * d), BF16),
            pltpu.VMEM((d, d), BF16),
            pltpu.SemaphoreType.DMA((2,)),
            pltpu.SemaphoreType.DMA((2,)),
            pltpu.SemaphoreType.DMA((1,)),
        ],
    )
    return pl.pallas_call(
        _expert_kernel,
        grid_spec=grid_spec,
        out_shape=jax.ShapeDtypeStruct((xs.shape[0], d), F32),
        compiler_params=pltpu.CompilerParams(
            dimension_semantics=("arbitrary",), vmem_limit_bytes=VMEM_LIMIT_BYTES),
        name="experts",
    )(*tables, xs, w_gu, b_gu, w_down, b_down)


def _final_kernel(h1_ref, mod_ref, ys_ref, pos_ref, fg_ref, o_ref):
    t = h1_ref.shape[0]
    g2 = mod_ref[5:6, :]
    pos_rows = [pos_ref[k:k + 1, :] for k in range(TOP_K)]
    perm = _perm_matrix(pos_rows, TILE_ROWS, t)
    moe = lax.dot_general(perm, ys_ref[...].astype(BF16), (((0,), (0,)), ((), ())), preferred_element_type=F32)
    o_ref[...] = _rms(h1_ref[...] + g2 * moe, fg_ref[...])


def _final_call(h1, mod3, ys, pos, final_g, bsz, seq):
    n_tok, d = h1.shape
    t = SEQ_TILE
    per_batch = seq // t
    out = pl.pallas_call(
        _final_kernel,
        grid=(n_tok // t,),
        in_specs=[
            pl.BlockSpec((t, d), lambda i: (i, 0)),
            pl.BlockSpec((None, 6, d), lambda i: (i // per_batch, 0, 0)),
            pl.BlockSpec((TILE_ROWS, d), lambda i: (i, 0)),
            pl.BlockSpec((TOP_K, t), lambda i: (0, i)),
            pl.BlockSpec((1, d), lambda i: (0, 0)),
        ],
        out_specs=pl.BlockSpec((t, d), lambda i: (i, 0)),
        out_shape=jax.ShapeDtypeStruct((n_tok, d), F32),
        compiler_params=pltpu.CompilerParams(
            dimension_semantics=("arbitrary",), vmem_limit_bytes=VMEM_LIMIT_BYTES),
        name="combine_final",
    )(h1, mod3, ys, pos, final_g.reshape(1, d))
    return out.reshape(bsz, seq, d)


def _block_diag(w):
    h, i, o = w.shape
    eye = jnp.eye(h, dtype=w.dtype)
    return (w[:, :, None, :] * eye[:, None, :, None]).reshape(h * i, h * o)


def _route_tables(cnt):
    r = ROW_BLOCK
    n_tiles = cnt.shape[0]
    nblk = n_tiles * TILE_ROWS // r + N_EXPERTS
    cnt = (cnt + (ROW_ALIGN - 1)) // ROW_ALIGN * ROW_ALIGN
    total = jnp.sum(cnt, axis=0)
    cum = jnp.cumsum(cnt, axis=0) - cnt
    seg_off = jnp.cumsum(cnt, axis=1) - cnt
    base = seg_off + jnp.arange(n_tiles, dtype=I32)[:, None] * TILE_ROWS
    nblk_e = (total + r - 1) // r
    blk_end = jnp.cumsum(nblk_e)
    blk_start = blk_end - nblk_e
    nb_used = blk_end[-1]
    blk = jnp.arange(nblk, dtype=I32)
    be = jnp.minimum(jnp.sum((blk[:, None] >= blk_end[None, :]).astype(I32), axis=1), N_EXPERTS - 1)
    used = blk < nb_used
    be = jnp.where(used, be, be[jnp.maximum(nb_used - 1, 0)])
    jrow = jnp.where(used, (blk - blk_start[be]) * r, 0)
    nval = jnp.where(used, jnp.clip(total[be] - jrow, 0, r), 0)
    first = jnp.concatenate([jnp.ones((1,), I32), (be[1:] != be[:-1]).astype(I32)])
    run_start = cum[:, be].T
    run_end = run_start + cnt[:, be].T
    ilo = jnp.sum((run_end <= jrow[:, None]).astype(I32), axis=1)
    ihi = n_tiles - 1 - jnp.sum((run_start >= (jrow + r)[:, None]).astype(I32), axis=1)
    used_rows = jnp.sum(cnt, axis=1)
    tail_row = jnp.arange(n_tiles, dtype=I32) * TILE_ROWS + used_rows
    tail_len = TILE_ROWS - used_rows
    i32 = lambda v: v.astype(I32)
    return (i32(be), i32(first), i32(nval), i32(jrow), i32(jnp.minimum(ilo, n_tiles - 1)), i32(ihi),
            i32(nb_used).reshape(1), i32(cnt).reshape(-1), i32(cum).reshape(-1), i32(base).reshape(-1),
            i32(tail_row), i32(tail_len))


def kernel(x, c, ada_w, ada_b, norm1_g, w_in, conv_w, conv_b, lru_wr, lru_br, lru_wi, lru_bi, lru_lambda, sgu_ln_g, sgu_ln_b, sgu_w, sgu_b, gnorm_lru_g, gnorm_sgu_g, w_out, norm2_g, router_w, router_b, exp_w_gu, exp_b_gu, exp_w_down, exp_b_down, final_g):
    bsz, seq, d = x.shape
    depth = ada_w.shape[0]
    assert depth == 1 and d == D_MODEL and seq % SEQ_TILE == 0 and ROW_BLOCK >= ROW_ALIGN * N_EXPERTS
    l = 0
    mod = _adaln_call(c, ada_w[l], ada_b[l])
    mod3 = mod.reshape(bsz, 6, d)

    row = lambda v: v.reshape(1, -1)
    wgate = jnp.concatenate([_block_diag(lru_wr[l]), _block_diag(lru_wi[l])], axis=1).astype(BF16)
    bgate = jnp.concatenate([lru_br[l], lru_bi[l]]).reshape(1, -1)
    sgub_full = jnp.repeat(sgu_b[l].T, SGU_HEAD_DIM, axis=1)
    h1, xs, pos, cnt = _mixer_call(
        x, mod3, row(norm1_g[l]), w_in[l].astype(BF16), conv_w[l], row(conv_b[l]), wgate, bgate,
        row(lru_lambda[l]), row(sgu_ln_g[l]), row(sgu_ln_b[l]), sgu_w[l], sgub_full,
        row(gnorm_lru_g[l]), row(gnorm_sgu_g[l]), w_out[l].astype(BF16), row(norm2_g[l]),
        router_w[l].T, router_b[l].reshape(-1, 1))

    tables = _route_tables(cnt[:, :, 0])
    ys = _expert_call(tables, xs, exp_w_gu[l], exp_b_gu[l].reshape(N_EXPERTS, 1, -1),
                      exp_w_down[l], exp_b_down[l].reshape(N_EXPERTS, 1, -1))
    return _final_call(h1, mod3, ys, pos, final_g, bsz, seq)
```

```python
import jax
import jax.numpy as jnp
from jax import lax
from jax.experimental import pallas as pl
from jax.experimental.pallas import tpu as pltpu

F32 = jnp.float32
BF16 = jnp.bfloat16
I32 = jnp.int32

D_MODEL = 1024
D_LRU = 512
D_SGU = 512
LRU_HEADS = 8
CONV_WIDTH = 4
LRU_C = 8.0
SGU_HEADS = 8
SGU_HEAD_DIM = D_SGU // SGU_HEADS
CHUNK = 128
N_EXPERTS = 32
TOP_K = 4
SWIGLU_ALPHA = 1.702
SWIGLU_LIMIT = 7.0
EPS = 1e-6

LANES = 128
SEQ_TILE = 512
ROW_ALIGN = 8
TILE_ROWS = TOP_K * SEQ_TILE + ROW_ALIGN * N_EXPERTS
PAYLOAD = LANES
ROW_W = D_MODEL + PAYLOAD
ROW_BLOCK = 512
VMEM_LIMIT_BYTES = 56 * 1024 * 1024


def _sigmoid(x):
    return 1.0 / (1.0 + jnp.exp(-x))


def _gelu_tanh(x):
    return 0.5 * x * (1.0 + jnp.tanh(0.7978845608028654 * (x + 0.044715 * (x * x * x))))


def _rms(x, g):
    ms = jnp.mean(x * x, axis=-1, keepdims=True)
    return x * lax.rsqrt(ms + EPS) * g


def _perm_matrix(pos_rows, n_rows, n_cols):
    prow = lax.broadcasted_iota(I32, (n_rows, n_cols), 0)
    acc = jnp.zeros((n_rows, n_cols), F32)
    for pk in pos_rows:
        acc = jnp.where(prow == pk, 1.0, acc)
    return acc.astype(BF16)


def _adaln_kernel(c_ref, w_ref, b_ref, o_ref):
    c = c_ref[...]
    ca = c * _sigmoid(c)
    o_ref[...] = jnp.dot(ca.astype(BF16), w_ref[...].astype(BF16), preferred_element_type=F32) + b_ref[...]


def _adaln_call(c, w, b):
    bsz, d = c.shape
    n_out = w.shape[1]
    return pl.pallas_call(
        _adaln_kernel,
        grid=(n_out // d,),
        in_specs=[
            pl.BlockSpec((bsz, d), lambda j: (0, 0)),
            pl.BlockSpec((d, d), lambda j: (0, j)),
            pl.BlockSpec((1, d), lambda j: (0, j)),
        ],
        out_specs=pl.BlockSpec((bsz, d), lambda j: (0, j)),
        out_shape=jax.ShapeDtypeStruct((bsz, n_out), F32),
        name="adaln",
    )(c, w, b.reshape(1, n_out))


def _linear_scan(a, b):
    t = a.shape[0]
    row = lax.broadcasted_iota(I32, (t, 1), 0)
    d = 1
    while d < t:
        a_s = pltpu.roll(a, d, 0)
        b_s = pltpu.roll(b, d, 0)
        m = row >= d
        b = jnp.where(m, a * b_s + b, b)
        a = jnp.where(m, a * a_s, a)
        d *= 2
    return a, b


def _mixer_kernel(x_ref, mod_ref, n1g_ref, win_ref, convw_ref, convb_ref, wgate_ref, bgate_ref, lam_ref,
                  lng_ref, lnb_ref, sguw_ref, sgub_ref, gl_ref, gs_ref, wout_ref, n2g_ref, rwt_ref, rb_ref,
                  h1_ref, xs_ref, pos_ref, cnt_ref,
                  xa_tail, h_carry):
    t = x_ref.shape[0]
    ji = pl.program_id(1)

    @pl.when(ji == 0)
    def _():
        xa_tail[...] = jnp.zeros_like(xa_tail)
        h_carry[...] = jnp.zeros_like(h_carry)

    mod = mod_ref[...]
    sh1, sc1, g1 = mod[0:1], mod[1:2], mod[2:3]
    sh2, sc2 = mod[3:4], mod[4:5]

    x = x_ref[...]
    z = _rms(x, n1g_ref[...]) * (1.0 + sc1) + sh1
    proj = jnp.dot(z.astype(BF16), win_ref[...], preferred_element_type=F32)
    xa = proj[:, 0:D_LRU]
    ya = proj[:, D_LRU:2 * D_LRU]
    u = proj[:, 2 * D_LRU:2 * D_LRU + D_SGU]
    v = proj[:, 2 * D_LRU + D_SGU:]

    tail = xa_tail[...]
    row8 = lax.broadcasted_iota(I32, (8, 1), 0)
    xc = xa * convw_ref[CONV_WIDTH - 1:CONV_WIDTH, :] + convb_ref[...]
    for sft in range(1, CONV_WIDTH):
        rolled = pltpu.roll(xa, sft, 0)
        head = jnp.where(row8 < sft, pltpu.roll(tail, sft, 0), rolled[0:8])
        shifted = jnp.concatenate([head, rolled[8:]], axis=0)
        xc = xc + shifted * convw_ref[CONV_WIDTH - 1 - sft:CONV_WIDTH - sft, :]
    xa_tail[...] = xa[t - 8:t]

    gates = jnp.dot(xc.astype(BF16), wgate_ref[...], preferred_element_type=F32) + bgate_ref[...]
    r_gate = _sigmoid(gates[:, 0:D_LRU])
    i_gate = _sigmoid(gates[:, D_LRU:])
    nlam = -lam_ref[...]
    softplus = jnp.maximum(nlam, 0.0) + jnp.log1p(jnp.exp(-jnp.abs(nlam)))
    log_a = (-LRU_C) * r_gate * softplus
    a = jnp.exp(log_a)
    om = -jnp.tanh(log_a) * (a * a + 1.0)
    mult = jnp.where(om > 0.0, om * lax.rsqrt(om), 0.0)
    bterm = mult * i_gate * xc
    a_cum, h_loc = _linear_scan(a, bterm)
    h = h_loc + a_cum * h_carry[...]
    h_carry[...] = h[t - 1:t]
    o_lru = _rms(h * _gelu_tanh(ya), gl_ref[...])

    ug = _gelu_tanh(u)
    vg = _gelu_tanh(v)
    mu = jnp.mean(vg, axis=-1, keepdims=True)
    vcen = vg - mu
    var = jnp.mean(vcen * vcen, axis=-1, keepdims=True)
    vn = (vcen * lax.rsqrt(var + EPS) * lng_ref[...] + lnb_ref[...]).astype(BF16)
    ri = lax.broadcasted_iota(I32, (CHUNK, CHUNK), 0)
    ci = lax.broadcasted_iota(I32, (CHUNK, CHUNK), 1)
    causal = ri >= ci
    lane = lax.broadcasted_iota(I32, (1, 2 * SGU_HEAD_DIM), 1)
    first_half = lane < SGU_HEAD_DIM
    pair_w = []
    for p in range(SGU_HEADS // 2):
        w0 = jnp.where(causal, sguw_ref[2 * p], 0.0).astype(BF16)
        w1 = jnp.where(causal, sguw_ref[2 * p + 1], 0.0).astype(BF16)
        pair_w.append(jnp.concatenate([w0, w1], axis=1))
    chunks = []
    zero = jnp.zeros((), BF16)
    for n in range(t // CHUNK):
        cols = []
        for p in range(SGU_HEADS // 2):
            blk = vn[n * CHUNK:(n + 1) * CHUNK, p * LANES:(p + 1) * LANES]
            rhs = jnp.concatenate([jnp.where(first_half, blk, zero), jnp.where(first_half, zero, blk)], axis=0)
            cols.append(jnp.dot(pair_w[p], rhs, preferred_element_type=F32))
        chunks.append(jnp.concatenate(cols, axis=1) + sgub_ref[...])
    mixed = jnp.concatenate(chunks, axis=0)
    o_sgu = _rms(ug * mixed, gs_ref[...])

    heads = jnp.concatenate([o_lru, o_sgu], axis=1).astype(BF16)
    h1 = x + g1 * jnp.dot(heads, wout_ref[...], preferred_element_type=F32)
    h1_ref[...] = h1

    z2 = _rms(h1, n2g_ref[...]) * (1.0 + sc2) + sh2
    logits = lax.dot_general(rwt_ref[...], z2, (((1,), (1,)), ((), ())),
                             precision=lax.Precision.HIGHEST, preferred_element_type=F32) + rb_ref[...]
    eidx = lax.broadcasted_iota(I32, (N_EXPERTS, t), 0)
    work = logits
    sel = []
    tops = []
    for k in range(TOP_K):
        m = jnp.max(work, axis=0, keepdims=True)
        idx = jnp.min(jnp.where(work == m, eidx, N_EXPERTS), axis=0, keepdims=True)
        onehot = eidx == idx
        sel.append(onehot)
        tops.append(m)
        work = jnp.where(onehot, -jnp.inf, work)
    exps = [jnp.exp(tk - tops[0]) for tk in tops]
    denom = exps[0] + exps[1] + exps[2] + exps[3]
    chosen = jnp.zeros((N_EXPERTS, t), F32)
    gsel = jnp.zeros((N_EXPERTS, t), F32)
    for k in range(TOP_K):
        chosen = jnp.where(sel[k], 1.0, chosen)
        gsel = jnp.where(sel[k], exps[k] / denom, gsel)

    chosen_b = chosen.astype(BF16)
    si = lax.broadcasted_iota(I32, (t, t), 0)
    ti = lax.broadcasted_iota(I32, (t, t), 1)
    before = jnp.where(si < ti, 1.0, 0.0).astype(BF16)
    excl = jnp.dot(chosen_b, before, preferred_element_type=F32)
    cnt_col = jnp.sum(chosen, axis=1, keepdims=True).astype(I32)
    cnt_ref[...] = jnp.broadcast_to(cnt_col, cnt_ref.shape)
    run_len = ((cnt_col + (ROW_ALIGN - 1)) // ROW_ALIGN * ROW_ALIGN).astype(F32)
    er = lax.broadcasted_iota(I32, (N_EXPERTS, N_EXPERTS), 0)
    ec = lax.broadcasted_iota(I32, (N_EXPERTS, N_EXPERTS), 1)
    lower = jnp.where(ec < er, 1.0, 0.0).astype(BF16)
    run_start = jnp.dot(lower, jnp.broadcast_to(run_len, (N_EXPERTS, LANES)).astype(BF16),
                        preferred_element_type=F32)[:, 0:1]
    posmat = excl + run_start
    pos_rows = []
    for k in range(TOP_K):
        pk = jnp.sum(jnp.where(sel[k], posmat, 0.0), axis=0, keepdims=True).astype(I32)
        pos_ref[k:k + 1, :] = pk
        pos_rows.append(pk)

    perm = _perm_matrix(pos_rows, TILE_ROWS, t)
    xs_ref[:, 0:D_MODEL] = jnp.dot(perm, z2.astype(BF16), preferred_element_type=F32)
    g_hi = gsel.astype(BF16)
    rem = gsel - g_hi.astype(F32)
    g_mid = rem.astype(BF16)
    g_lo = (rem - g_mid.astype(F32)).astype(BF16)
    gp = jnp.concatenate([g_hi, g_mid, g_lo, jnp.zeros((PAYLOAD - 3 * N_EXPERTS, t), BF16)], axis=0)
    xs_ref[:, D_MODEL:] = lax.dot_general(perm, gp, (((1,), (1,)), ((), ())), preferred_element_type=F32)


def _mixer_call(x, mod3, n1g, win, convw, convb, wgate, bgate, lam, lng, lnb, sguw, sgub_full, gl, gs, wout,
                n2g, rwt, rb):
    bsz, seq, d = x.shape
    t = SEQ_TILE
    n_tok = bsz * seq
    tiles = seq // t

    def const(shape):
        return pl.BlockSpec(shape, lambda b, j: (0,) * len(shape))

    return pl.pallas_call(
        _mixer_kernel,
        grid=(bsz, tiles),
        in_specs=[
            pl.BlockSpec((None, t, d), lambda b, j: (b, j, 0)),
            pl.BlockSpec((None, 6, d), lambda b, j: (b, 0, 0)),
            const((1, d)),
            const((d, 2 * d)),
            const((CONV_WIDTH, D_LRU)),
            const((1, D_LRU)),
            const((D_LRU, 2 * D_LRU)),
            const((1, 2 * D_LRU)),
            const((1, D_LRU)),
            const((1, D_SGU)),
            const((1, D_SGU)),
            const((SGU_HEADS, CHUNK, CHUNK)),
            const((CHUNK, D_SGU)),
            const((1, D_LRU)),
            const((1, D_SGU)),
            const((d, d)),
            const((1, d)),
            const((N_EXPERTS, d)),
            const((N_EXPERTS, 1)),
        ],
        out_specs=[
            pl.BlockSpec((t, d), lambda b, j: (b * tiles + j, 0)),
            pl.BlockSpec((TILE_ROWS, ROW_W), lambda b, j: (b * tiles + j, 0)),
            pl.BlockSpec((TOP_K, t), lambda b, j: (0, b * tiles + j)),
            pl.BlockSpec((None, N_EXPERTS, LANES), lambda b, j: (b * tiles + j, 0, 0)),
        ],
        out_shape=[
            jax.ShapeDtypeStruct((n_tok, d), F32),
            jax.ShapeDtypeStruct((n_tok // t * TILE_ROWS, ROW_W), F32),
            jax.ShapeDtypeStruct((TOP_K, n_tok), I32),
            jax.ShapeDtypeStruct((n_tok // t, N_EXPERTS, LANES), I32),
        ],
        scratch_shapes=[pltpu.VMEM((8, D_LRU), F32), pltpu.VMEM((1, D_LRU), F32)],
        compiler_params=pltpu.CompilerParams(
            dimension_semantics=("arbitrary", "arbitrary"), vmem_limit_bytes=VMEM_LIMIT_BYTES),
        name="mixer_router",
    )(x, mod3, n1g, win, convw, convb, wgate, bgate, lam, lng, lnb, sguw, sgub_full, gl, gs, wout, n2g, rwt, rb)


_PIECES = tuple(1 << s for s in range(ROW_BLOCK.bit_length() - 1, ROW_ALIGN.bit_length() - 2, -1))


def _expert_kernel(be_ref, first_ref, nval_ref, jrow_ref, ilo_ref, ihi_ref, nb_ref, cnt_ref, cum_ref, base_ref,
                   tailrow_ref, taillen_ref,
                   xs_hbm, wgu_ref, bgu_ref, wd_ref, bd_ref,
                   ys_hbm,
                   xbuf, ybuf, zbuf, wgu_bf, wd_bf, gsem, ssem, zsem):
    r = xbuf.shape[1]
    n_tiles = tailrow_ref.shape[0]
    b = pl.program_id(0)
    nb = nb_ref[0]
    slot = b & 1

    def tail_copy(row, size):
        return pltpu.make_async_copy(zbuf.at[pl.ds(0, size), :], ys_hbm.at[pl.ds(row, size), :], zsem.at[0])

    def for_each_tail_piece(fn):
        def tile_body(i, c):
            ln = taillen_ref[i]
            for size in _PIECES:
                off = ln & ~(2 * size - 1)

                @pl.when((ln & size) != 0)
                def _():
                    fn(pl.multiple_of(tailrow_ref[i] + off, ROW_ALIGN), size)
            return c
        lax.fori_loop(0, n_tiles, tile_body, 0)

    def for_each_piece(blk, fn):
        e = be_ref[blk]
        j0 = jrow_ref[blk]

        def tile_body(i, c):
            idx = i * N_EXPERTS + e
            cu = cum_ref[idx]
            lo = jnp.maximum(cu, j0)
            hi = jnp.minimum(cu + cnt_ref[idx], j0 + r)
            ln = hi - lo

            @pl.when(ln > 0)
            def _():
                src = base_ref[idx] + (lo - cu)
                dst = lo - j0
                for size in _PIECES:
                    off = ln & ~(2 * size - 1)

                    @pl.when((ln & size) != 0)
                    def _():
                        fn(pl.multiple_of(src + off, ROW_ALIGN), pl.multiple_of(dst + off, ROW_ALIGN), size)
            return c
        lax.fori_loop(ilo_ref[blk], ihi_ref[blk] + 1, tile_body, 0)

    def gather_piece(dst_slot):
        def fn(hbm_row, buf_row, size):
            pltpu.make_async_copy(xs_hbm.at[pl.ds(hbm_row, size), :],
                                  xbuf.at[dst_slot, pl.ds(buf_row, size), :], gsem.at[dst_slot]).start()
        return fn

    def scatter_piece(src_slot):
        def fn(hbm_row, buf_row, size):
            pltpu.make_async_copy(ybuf.at[src_slot, pl.ds(buf_row, size), :],
                                  ys_hbm.at[pl.ds(hbm_row, size), :], ssem.at[src_slot]).start()
        return fn

    def wait_rows(n, make_copy):
        for size in _PIECES:
            @pl.when((n & size) != 0)
            def _():
                make_copy(size).wait()

    def wait_gather(s, n):
        wait_rows(n, lambda size: pltpu.make_async_copy(
            xs_hbm.at[pl.ds(0, size), :], xbuf.at[s, pl.ds(0, size), :], gsem.at[s]))

    def wait_scatter(s, n):
        wait_rows(n, lambda size: pltpu.make_async_copy(
            ybuf.at[s, pl.ds(0, size), :], ys_hbm.at[pl.ds(0, size), :], ssem.at[s]))

    @pl.when(b == 0)
    def _():
        xbuf[...] = jnp.zeros_like(xbuf)
        for_each_piece(0, gather_piece(0))
        zbuf[...] = jnp.zeros_like(zbuf)
        for_each_tail_piece(lambda row, size: tail_copy(row, size).start())

    @pl.when(b < nb)
    def _():
        wait_gather(slot, nval_ref[b])

        @pl.when(b + 1 < nb)
        def _():
            for_each_piece(b + 1, gather_piece(1 - slot))

        @pl.when(first_ref[b] == 1)
        def _():
            wgu_bf[...] = wgu_ref[...].astype(BF16)
            wd_bf[...] = wd_ref[...].astype(BF16)

        @pl.when(b >= 2)
        def _():
            wait_scatter(slot, nval_ref[jnp.maximum(b - 2, 0)])

        def expert_rows(m):
            xw = xbuf[slot, 0:m]
            xb = xw[:, 0:D_MODEL].astype(BF16)
            lane = lax.broadcasted_iota(I32, (1, PAYLOAD), 1)
            gate = jnp.sum(jnp.where((lane & (N_EXPERTS - 1)) == be_ref[b], xw[:, D_MODEL:], 0.0),
                           axis=1, keepdims=True)
            gu = jnp.dot(xb, wgu_bf[...], preferred_element_type=F32) + bgu_ref[...]
            g = jnp.minimum(gu[:, 0:D_MODEL], SWIGLU_LIMIT)
            lin = jnp.clip(gu[:, D_MODEL:], -SWIGLU_LIMIT, SWIGLU_LIMIT)
            act = g * _sigmoid(SWIGLU_ALPHA * g) * (lin + 1.0)
            y = jnp.dot(act.astype(BF16), wd_bf[...], preferred_element_type=F32) + bd_ref[...]
            ybuf[slot, 0:m] = y * gate

        @pl.when(nval_ref[b] > r // 2)
        def _():
            expert_rows(r)

        @pl.when(nval_ref[b] <= r // 2)
        def _():
            expert_rows(r // 2)

        for_each_piece(b, scatter_piece(slot))

        @pl.when(b == nb - 1)
        def _():
            wait_scatter(slot, nval_ref[b])

            @pl.when(nb >= 2)
            def _():
                wait_scatter(1 - slot, nval_ref[jnp.maximum(b - 1, 0)])

            for_each_tail_piece(lambda row, size: tail_copy(0, size).wait())


def _expert_call(tables, xs, w_gu, b_gu, w_down, b_down):
    nblk = tables[0].shape[0]
    r = ROW_BLOCK
    d = D_MODEL
    n_pre = len(tables)
    wmap = lambda b, be, *_: (be[b], 0, 0)
    grid_spec = pltpu.PrefetchScalarGridSpec(
        num_scalar_prefetch=n_pre,
        grid=(nblk,),
        in_specs=[
            pl.BlockSpec(memory_space=pl.ANY),
            pl.BlockSpec((None, d, 2 * d), wmap),
            pl.BlockSpec((None, 1, 2 * d), wmap),
            pl.BlockSpec((None, d, d), wmap),
            pl.BlockSpec((None, 1, d), wmap),
        ],
        out_specs=pl.BlockSpec(memory_space=pl.ANY),
        scratch_shapes=[
            pltpu.VMEM((2, r, ROW_W), F32),
            pltpu.VMEM((2, r, d), F32),
            pltpu.VMEM((ROW_ALIGN * N_EXPERTS, d), F32),
            pltpu.VMEM((d, 2 * d), BF16),
            pltpu.VMEM((d, d), BF16),
            pltpu.SemaphoreType.DMA((2,)),
            pltpu.SemaphoreType.DMA((2,)),
            pltpu.SemaphoreType.DMA((1,)),
        ],
    )
    return pl.pallas_call(
        _expert_kernel,
        grid_spec=grid_spec,
        out_shape=jax.ShapeDtypeStruct((xs.shape[0], d), F32),
        compiler_params=pltpu.CompilerParams(
            dimension_semantics=("arbitrary",), vmem_limit_bytes=VMEM_LIMIT_BYTES),
        name="experts",
    )(*tables, xs, w_gu, b_gu, w_down, b_down)


def _final_kernel(h1_ref, mod_ref, ys_ref, pos_ref, fg_ref, o_ref):
    t = h1_ref.shape[0]
    g2 = mod_ref[5:6, :]
    pos_rows = [pos_ref[k:k + 1, :] for k in range(TOP_K)]
    perm = _perm_matrix(pos_rows, TILE_ROWS, t)
    moe = lax.dot_general(perm, ys_ref[...].astype(BF16), (((0,), (0,)), ((), ())), preferred_element_type=F32)
    o_ref[...] = _rms(h1_ref[...] + g2 * moe, fg_ref[...])


def _final_call(h1, mod3, ys, pos, final_g, bsz, seq):
    n_tok, d = h1.shape
    t = SEQ_TILE
    per_batch = seq // t
    out = pl.pallas_call(
        _final_kernel,
        grid=(n_tok // t,),
        in_specs=[
            pl.BlockSpec((t, d), lambda i: (i, 0)),
            pl.BlockSpec((None, 6, d), lambda i: (i // per_batch, 0, 0)),
            pl.BlockSpec((TILE_ROWS, d), lambda i: (i, 0)),
            pl.BlockSpec((TOP_K, t), lambda i: (0, i)),
            pl.BlockSpec((1, d), lambda i: (0, 0)),
        ],
        out_specs=pl.BlockSpec((t, d), lambda i: (i, 0)),
        out_shape=jax.ShapeDtypeStruct((n_tok, d), F32),
        compiler_params=pltpu.CompilerParams(
            dimension_semantics=("arbitrary",), vmem_limit_bytes=VMEM_LIMIT_BYTES),
        name="combine_final",
    )(h1, mod3, ys, pos, final_g.reshape(1, d))
    return out.reshape(bsz, seq, d)


def _block_diag(w):
    h, i, o = w.shape
    eye = jnp.eye(h, dtype=w.dtype)
    return (w[:, :, None, :] * eye[:, None, :, None]).reshape(h * i, h * o)


def _route_tables(cnt):
    r = ROW_BLOCK
    n_tiles = cnt.shape[0]
    nblk = n_tiles * TILE_ROWS // r + N_EXPERTS
    cnt = (cnt + (ROW_ALIGN - 1)) // ROW_ALIGN * ROW_ALIGN
    total = jnp.sum(cnt, axis=0)
    cum = jnp.cumsum(cnt, axis=0) - cnt
    seg_off = jnp.cumsum(cnt, axis=1) - cnt
    base = seg_off + jnp.arange(n_tiles, dtype=I32)[:, None] * TILE_ROWS
    nblk_e = (total + r - 1) // r
    blk_end = jnp.cumsum(nblk_e)
    blk_start = blk_end - nblk_e
    nb_used = blk_end[-1]
    blk = jnp.arange(nblk, dtype=I32)
    be = jnp.minimum(jnp.sum((blk[:, None] >= blk_end[None, :]).astype(I32), axis=1), N_EXPERTS - 1)
    used = blk < nb_used
    be = jnp.where(used, be, be[jnp.maximum(nb_used - 1, 0)])
    jrow = jnp.where(used, (blk - blk_start[be]) * r, 0)
    nval = jnp.where(used, jnp.clip(total[be] - jrow, 0, r), 0)
    first = jnp.concatenate([jnp.ones((1,), I32), (be[1:] != be[:-1]).astype(I32)])
    run_start = cum[:, be].T
    run_end = run_start + cnt[:, be].T
    ilo = jnp.sum((run_end <= jrow[:, None]).astype(I32), axis=1)
    ihi = n_tiles - 1 - jnp.sum((run_start >= (jrow + r)[:, None]).astype(I32), axis=1)
    used_rows = jnp.sum(cnt, axis=1)
    tail_row = jnp.arange(n_tiles, dtype=I32) * TILE_ROWS + used_rows
    tail_len = TILE_ROWS - used_rows
    i32 = lambda v: v.astype(I32)
    return (i32(be), i32(first), i32(nval), i32(jrow), i32(jnp.minimum(ilo, n_tiles - 1)), i32(ihi),
            i32(nb_used).reshape(1), i32(cnt).reshape(-1), i32(cum).reshape(-1), i32(base).reshape(-1),
            i32(tail_row), i32(tail_len))


def kernel(x, c, ada_w, ada_b, norm1_g, w_in, conv_w, conv_b, lru_wr, lru_br, lru_wi, lru_bi, lru_lambda, sgu_ln_g, sgu_ln_b, sgu_w, sgu_b, gnorm_lru_g, gnorm_sgu_g, w_out, norm2_g, router_w, router_b, exp_w_gu, exp_b_gu, exp_w_down, exp_b_down, final_g):
    bsz, seq, d = x.shape
    depth = ada_w.shape[0]
    assert depth == 1 and d == D_MODEL and seq % SEQ_TILE == 0 and ROW_BLOCK >= ROW_ALIGN * N_EXPERTS
    l = 0
    mod = _adaln_call(c, ada_w[l], ada_b[l])
    mod3 = mod.reshape(bsz, 6, d)

    row = lambda v: v.reshape(1, -1)
    wgate = jnp.concatenate([_block_diag(lru_wr[l]), _block_diag(lru_wi[l])], axis=1).astype(BF16)
    bgate = jnp.concatenate([lru_br[l], lru_bi[l]]).reshape(1, -1)
    sgub_full = jnp.repeat(sgu_b[l].T, SGU_HEAD_DIM, axis=1)
    h1, xs, pos, cnt = _mixer_call(
        x, mod3, row(norm1_g[l]), w_in[l].astype(BF16), conv_w[l], row(conv_b[l]), wgate, bgate,
        row(lru_lambda[l]), row(sgu_ln_g[l]), row(sgu_ln_b[l]), sgu_w[l], sgub_full,
        row(gnorm_lru_g[l]), row(gnorm_sgu_g[l]), w_out[l].astype(BF16), row(norm2_g[l]),
        router_w[l].T, router_b[l].reshape(-1, 1))

    tables = _route_tables(cnt[:, :, 0])
    ys = _expert_call(tables, xs, exp_w_gu[l], exp_b_gu[l].reshape(N_EXPERTS, 1, -1),
                      exp_w_down[l], exp_b_down[l].reshape(N_EXPERTS, 1, -1))
    return _final_call(h1, mod3, ys, pos, final_g, bsz, seq)
```

```python
import jax
import jax.numpy as jnp
from jax import lax
from jax.experimental import pallas as pl
from jax.experimental.pallas import tpu as pltpu

F32 = jnp.float32
BF16 = jnp.bfloat16
I32 = jnp.int32

D_MODEL = 1024
D_LRU = 512
D_SGU = 512
LRU_HEADS = 8
CONV_WIDTH = 4
LRU_C = 8.0
SGU_HEADS = 8
SGU_HEAD_DIM = D_SGU // SGU_HEADS
CHUNK = 128
N_EXPERTS = 32
TOP_K = 4
SWIGLU_ALPHA = 1.702
SWIGLU_LIMIT = 7.0
EPS = 1e-6

LANES = 128
SEQ_TILE = 512
ROW_ALIGN = 8
TILE_ROWS = TOP_K * SEQ_TILE + ROW_ALIGN * N_EXPERTS
PAYLOAD = LANES
ROW_W = D_MODEL + PAYLOAD
ROW_BLOCK = 512
VMEM_LIMIT_BYTES = 56 * 1024 * 1024


def _sigmoid(x):
    return 1.0 / (1.0 + jnp.exp(-x))


def _gelu_tanh(x):
    return 0.5 * x * (1.0 + jnp.tanh(0.7978845608028654 * (x + 0.044715 * (x * x * x))))


def _rms(x, g):
    ms = jnp.mean(x * x, axis=-1, keepdims=True)
    return x * lax.rsqrt(ms + EPS) * g


def _perm_matrix(pos_rows, n_rows, n_cols):
    prow = lax.broadcasted_iota(I32, (n_rows, n_cols), 0)
    acc = jnp.zeros((n_rows, n_cols), F32)
    for pk in pos_rows:
        acc = jnp.where(prow == pk, 1.0, acc)
    return acc.astype(BF16)


def _adaln_kernel(c_ref, w_ref, b_ref, o_ref):
    c = c_ref[...]
    ca = c * _sigmoid(c)
    o_ref[...] = jnp.dot(ca.astype(BF16), w_ref[...].astype(BF16), preferred_element_type=F32) + b_ref[...]


def _adaln_call(c, w, b):
    bsz, d = c.shape
    n_out = w.shape[1]
    return pl.pallas_call(
        _adaln_kernel,
        grid=(n_out // d,),
        in_specs=[
            pl.BlockSpec((bsz, d), lambda j: (0, 0)),
            pl.BlockSpec((d, d), lambda j: (0, j)),
            pl.BlockSpec((1, d), lambda j: (0, j)),
        ],
        out_specs=pl.BlockSpec((bsz, d), lambda j: (0, j)),
        out_shape=jax.ShapeDtypeStruct((bsz, n_out), F32),
        name="adaln",
    )(c, w, b.reshape(1, n_out))


def _linear_scan(a, b):
    t = a.shape[0]
    row = lax.broadcasted_iota(I32, (t, 1), 0)
    d = 1
    while d < t:
        a_s = pltpu.roll(a, d, 0)
        b_s = pltpu.roll(b, d, 0)
        m = row >= d
        b = jnp.where(m, a * b_s + b, b)
        a = jnp.where(m, a * a_s, a)
        d *= 2
    return a, b


def _mixer_kernel(x_ref, mod_ref, n1g_ref, win_ref, convw_ref, convb_ref, wgate_ref, bgate_ref, lam_ref,
                  lng_ref, lnb_ref, sguw_ref, sgub_ref, gl_ref, gs_ref, wout_ref, n2g_ref, rwt_ref, rb_ref,
                  h1_ref, xs_ref, pos_ref, cnt_ref,
                  xa_tail, h_carry):
    t = x_ref.shape[0]
    ji = pl.program_id(1)

    @pl.when(ji == 0)
    def _():
        xa_tail[...] = jnp.zeros_like(xa_tail)
        h_carry[...] = jnp.zeros_like(h_carry)

    mod = mod_ref[...]
    sh1, sc1, g1 = mod[0:1], mod[1:2], mod[2:3]
    sh2, sc2 = mod[3:4], mod[4:5]

    x = x_ref[...]
    z = _rms(x, n1g_ref[...]) * (1.0 + sc1) + sh1
    proj = jnp.dot(z.astype(BF16), win_ref[...], preferred_element_type=F32)
    xa = proj[:, 0:D_LRU]
    ya = proj[:, D_LRU:2 * D_LRU]
    u = proj[:, 2 * D_LRU:2 * D_LRU + D_SGU]
    v = proj[:, 2 * D_LRU + D_SGU:]

    tail = xa_tail[...]
    row8 = lax.broadcasted_iota(I32, (8, 1), 0)
    xc = xa * convw_ref[CONV_WIDTH - 1:CONV_WIDTH, :] + convb_ref[...]
    for sft in range(1, CONV_WIDTH):
        rolled = pltpu.roll(xa, sft, 0)
        head = jnp.where(row8 < sft, pltpu.roll(tail, sft, 0), rolled[0:8])
        shifted = jnp.concatenate([head, rolled[8:]], axis=0)
        xc = xc + shifted * convw_ref[CONV_WIDTH - 1 - sft:CONV_WIDTH - sft, :]
    xa_tail[...] = xa[t - 8:t]

    gates = jnp.dot(xc.astype(BF16), wgate_ref[...], preferred_element_type=F32) + bgate_ref[...]
    r_gate = _sigmoid(gates[:, 0:D_LRU])
    i_gate = _sigmoid(gates[:, D_LRU:])
    nlam = -lam_ref[...]
    softplus = jnp.maximum(nlam, 0.0) + jnp.log1p(jnp.exp(-jnp.abs(nlam)))
    log_a = (-LRU_C) * r_gate * softplus
    a = jnp.exp(log_a)
    om = -jnp.tanh(log_a) * (a * a + 1.0)
    mult = jnp.where(om > 0.0, om * lax.rsqrt(om), 0.0)
    bterm = mult * i_gate * xc
    a_cum, h_loc = _linear_scan(a, bterm)
    h = h_loc + a_cum * h_carry[...]
    h_carry[...] = h[t - 1:t]
    o_lru = _rms(h * _gelu_tanh(ya), gl_ref[...])

    ug = _gelu_tanh(u)
    vg = _gelu_tanh(v)
    mu = jnp.mean(vg, axis=-1, keepdims=True)
    vcen = vg - mu
    var = jnp.mean(vcen * vcen, axis=-1, keepdims=True)
    vn = (vcen * lax.rsqrt(var + EPS) * lng_ref[...] + lnb_ref[...]).astype(BF16)
    ri = lax.broadcasted_iota(I32, (CHUNK, CHUNK), 0)
    ci = lax.broadcasted_iota(I32, (CHUNK, CHUNK), 1)
    causal = ri >= ci
    lane = lax.broadcasted_iota(I32, (1, 2 * SGU_HEAD_DIM), 1)
    first_half = lane < SGU_HEAD_DIM
    pair_w = []
    for p in range(SGU_HEADS // 2):
        w0 = jnp.where(causal, sguw_ref[2 * p], 0.0).astype(BF16)
        w1 = jnp.where(causal, sguw_ref[2 * p + 1], 0.0).astype(BF16)
        pair_w.append(jnp.concatenate([w0, w1], axis=1))
    chunks = []
    zero = jnp.zeros((), BF16)
    for n in range(t // CHUNK):
        cols = []
        for p in range(SGU_HEADS // 2):
            blk = vn[n * CHUNK:(n + 1) * CHUNK, p * LANES:(p + 1) * LANES]
            rhs = jnp.concatenate([jnp.where(first_half, blk, zero), jnp.where(first_half, zero, blk)], axis=0)
            cols.append(jnp.dot(pair_w[p], rhs, preferred_element_type=F32))
        chunks.append(jnp.concatenate(cols, axis=1) + sgub_ref[...])
    mixed = jnp.concatenate(chunks, axis=0)
    o_sgu = _rms(ug * mixed, gs_ref[...])

    heads = jnp.concatenate([o_lru, o_sgu], axis=1).astype(BF16)
    h1 = x + g1 * jnp.dot(heads, wout_ref[...], preferred_element_type=F32)
    h1_ref[...] = h1

    z2 = _rms(h1, n2g_ref[...]) * (1.0 + sc2) + sh2
    logits = lax.dot_general(rwt_ref[...], z2, (((1,), (1,)), ((), ())),
                             precision=lax.Precision.HIGHEST, preferred_element_type=F32) + rb_ref[...]
    eidx = lax.broadcasted_iota(I32, (N_EXPERTS, t), 0)
    work = logits
    sel = []
    tops = []
    for k in range(TOP_K):
        m = jnp.max(work, axis=0, keepdims=True)
        idx = jnp.min(jnp.where(work == m, eidx, N_EXPERTS), axis=0, keepdims=True)
        onehot = eidx == idx
        sel.append(onehot)
        tops.append(m)
        work = jnp.where(onehot, -jnp.inf, work)
    exps = [jnp.exp(tk - tops[0]) for tk in tops]
    denom = exps[0] + exps[1] + exps[2] + exps[3]
    chosen = jnp.zeros((N_EXPERTS, t), F32)
    gsel = jnp.zeros((N_EXPERTS, t), F32)
    for k in range(TOP_K):
        chosen = jnp.where(sel[k], 1.0, chosen)
        gsel = jnp.where(sel[k], exps[k] / denom, gsel)

    chosen_b = chosen.astype(BF16)
    si = lax.broadcasted_iota(I32, (t, t), 0)
    ti = lax.broadcasted_iota(I32, (t, t), 1)
    before = jnp.where(si < ti, 1.0, 0.0).astype(BF16)
    excl = jnp.dot(chosen_b, before, preferred_element_type=F32)
    cnt_col = jnp.sum(chosen, axis=1, keepdims=True).astype(I32)
    cnt_ref[...] = jnp.broadcast_to(cnt_col, cnt_ref.shape)
    run_len = ((cnt_col + (ROW_ALIGN - 1)) // ROW_ALIGN * ROW_ALIGN).astype(F32)
    er = lax.broadcasted_iota(I32, (N_EXPERTS, N_EXPERTS), 0)
    ec = lax.broadcasted_iota(I32, (N_EXPERTS, N_EXPERTS), 1)
    lower = jnp.where(ec < er, 1.0, 0.0).astype(BF16)
    run_start = jnp.dot(lower, jnp.broadcast_to(run_len, (N_EXPERTS, LANES)).astype(BF16),
                        preferred_element_type=F32)[:, 0:1]
    posmat = excl + run_start
    pos_rows = []
    for k in range(TOP_K):
        pk = jnp.sum(jnp.where(sel[k], posmat, 0.0), axis=0, keepdims=True).astype(I32)
        pos_ref[k:k + 1, :] = pk
        pos_rows.append(pk)

    perm = _perm_matrix(pos_rows, TILE_ROWS, t)
    xs_ref[:, 0:D_MODEL] = jnp.dot(perm, z2.astype(BF16), preferred_element_type=F32)
    g_hi = gsel.astype(BF16)
    rem = gsel - g_hi.astype(F32)
    g_mid = rem.astype(BF16)
    g_lo = (rem - g_mid.astype(F32)).astype(BF16)
    gp = jnp.concatenate([g_hi, g_mid, g_lo, jnp.zeros((PAYLOAD - 3 * N_EXPERTS, t), BF16)], axis=0)
    xs_ref[:, D_MODEL:] = lax.dot_general(perm, gp, (((1,), (1,)), ((), ())), preferred_element_type=F32)


def _mixer_call(x, mod3, n1g, win, convw, convb, wgate, bgate, lam, lng, lnb, sguw, sgub_full, gl, gs, wout,
                n2g, rwt, rb):
    bsz, seq, d = x.shape
    t = SEQ_TILE
    n_tok = bsz * seq
    tiles = seq // t

    def const(shape):
        return pl.BlockSpec(shape, lambda b, j: (0,) * len(shape))

    return pl.pallas_call(
        _mixer_kernel,
        grid=(bsz, tiles),
        in_specs=[
            pl.BlockSpec((None, t, d), lambda b, j: (b, j, 0)),
            pl.BlockSpec((None, 6, d), lambda b, j: (b, 0, 0)),
            const((1, d)),
            const((d, 2 * d)),
            const((CONV_WIDTH, D_LRU)),
            const((1, D_LRU)),
            const((D_LRU, 2 * D_LRU)),
            const((1, 2 * D_LRU)),
            const((1, D_LRU)),
            const((1, D_SGU)),
            const((1, D_SGU)),
            const((SGU_HEADS, CHUNK, CHUNK)),
            const((CHUNK, D_SGU)),
            const((1, D_LRU)),
            const((1, D_SGU)),
            const((d, d)),
            const((1, d)),
            const((N_EXPERTS, d)),
            const((N_EXPERTS, 1)),
        ],
        out_specs=[
            pl.BlockSpec((t, d), lambda b, j: (b * tiles + j, 0)),
            pl.BlockSpec((TILE_ROWS, ROW_W), lambda b, j: (b * tiles + j, 0)),
            pl.BlockSpec((TOP_K, t), lambda b, j: (0, b * tiles + j)),
            pl.BlockSpec((None, N_EXPERTS, LANES), lambda b, j: (b * tiles + j, 0, 0)),
        ],
        out_shape=[
            jax.ShapeDtypeStruct((n_tok, d), F32),
            jax.ShapeDtypeStruct((n_tok // t * TILE_ROWS, ROW_W), F32),
            jax.ShapeDtypeStruct((TOP_K, n_tok), I32),
            jax.ShapeDtypeStruct((n_tok // t, N_EXPERTS, LANES), I32),
        ],
        scratch_shapes=[pltpu.VMEM((8, D_LRU), F32), pltpu.VMEM((1, D_LRU), F32)],
        compiler_params=pltpu.CompilerParams(
            dimension_semantics=("arbitrary", "arbitrary"), vmem_limit_bytes=VMEM_LIMIT_BYTES),
        name="mixer_router",
    )(x, mod3, n1g, win, convw, convb, wgate, bgate, lam, lng, lnb, sguw, sgub_full, gl, gs, wout, n2g, rwt, rb)


def _expert_kernel(be_ref, first_ref, nval_ref, jrow_ref, ilo_ref, ihi_ref, nexte_ref, nb_ref,
                   cnt_ref, cum_ref, base_ref, tailrow_ref, taillen_ref,
                   xs_hbm, wgu_hbm, bgu_ref, wd_hbm, bd_ref,
                   ys_hbm,
                   xbuf, ybuf, zbuf, wgu_st, wd_st, wgu_bf, wd_bf, gsem, ssem, zsem, wsem):
    r = xbuf.shape[1]
    n_tiles = tailrow_ref.shape[0]
    b = pl.program_id(0)
    nb = nb_ref[0]
    slot = b & 1

    rows = lambda v: pl.multiple_of(v, ROW_ALIGN)

    def weight_copies(e):
        return (pltpu.make_async_copy(wgu_hbm.at[e], wgu_st, wsem.at[0]),
                pltpu.make_async_copy(wd_hbm.at[e], wd_st, wsem.at[1]))

    def tail_copy(i):
        n = rows(taillen_ref[i])
        return pltpu.make_async_copy(zbuf.at[pl.ds(0, n), :], ys_hbm.at[pl.ds(rows(tailrow_ref[i]), n), :],
                                     zsem.at[0])

    def for_each_run(blk, fn):
        e = be_ref[blk]
        j0 = jrow_ref[blk]

        def tile_body(i, c):
            idx = i * N_EXPERTS + e
            cu = cum_ref[idx]
            lo = jnp.maximum(cu, j0)
            hi = jnp.minimum(cu + cnt_ref[idx], j0 + r)

            @pl.when(hi > lo)
            def _():
                fn(rows(base_ref[idx] + (lo - cu)), rows(lo - j0), rows(hi - lo))
            return c
        lax.fori_loop(ilo_ref[blk], ihi_ref[blk] + 1, tile_body, 0)

    def start_gather(dst_slot):
        def fn(hbm_row, buf_row, n):
            pltpu.make_async_copy(xs_hbm.at[pl.ds(hbm_row, n), :],
                                  xbuf.at[dst_slot, pl.ds(buf_row, n), :], gsem.at[dst_slot]).start()
        return fn

    def start_scatter(src_slot):
        def fn(hbm_row, buf_row, n):
            pltpu.make_async_copy(ybuf.at[src_slot, pl.ds(buf_row, n), :],
                                  ys_hbm.at[pl.ds(hbm_row, n), :], ssem.at[src_slot]).start()
        return fn

    def wait_gather(s, n):
        pltpu.make_async_copy(xs_hbm.at[pl.ds(0, rows(n)), :], xbuf.at[s, pl.ds(0, rows(n)), :], gsem.at[s]).wait()

    def wait_scatter(s, n):
        pltpu.make_async_copy(ybuf.at[s, pl.ds(0, rows(n)), :], ys_hbm.at[pl.ds(0, rows(n)), :], ssem.at[s]).wait()

    @pl.when(b == 0)
    def _():
        for c in weight_copies(be_ref[0]):
            c.start()
        xbuf[...] = jnp.zeros_like(xbuf)
        for_each_run(0, start_gather(0))
        zbuf[...] = jnp.zeros_like(zbuf)

        def tail_body(i, c):
            tail_copy(i).start()
            return c
        lax.fori_loop(0, n_tiles, tail_body, 0)

    @pl.when(b < nb)
    def _():
        wait_gather(slot, nval_ref[b])

        @pl.when(b + 1 < nb)
        def _():
            for_each_run(b + 1, start_gather(1 - slot))

        @pl.when(first_ref[b] == 1)
        def _():
            for c in weight_copies(be_ref[b]):
                c.wait()
            wgu_bf[...] = wgu_st[...].astype(BF16)
            wd_bf[...] = wd_st[...].astype(BF16)

            @pl.when(nexte_ref[b] >= 0)
            def _():
                for c in weight_copies(nexte_ref[b]):
                    c.start()

        @pl.when(b >= 2)
        def _():
            wait_scatter(slot, nval_ref[jnp.maximum(b - 2, 0)])

        def expert_rows(m):
            xw = xbuf[slot, 0:m]
            xb = xw[:, 0:D_MODEL].astype(BF16)
            lane = lax.broadcasted_iota(I32, (1, PAYLOAD), 1)
            gate = jnp.sum(jnp.where((lane & (N_EXPERTS - 1)) == be_ref[b], xw[:, D_MODEL:], 0.0),
                           axis=1, keepdims=True)
            gu = jnp.dot(xb, wgu_bf[...], preferred_element_type=F32) + bgu_ref[...]
            g = jnp.minimum(gu[:, 0:D_MODEL], SWIGLU_LIMIT)
            lin = jnp.clip(gu[:, D_MODEL:], -SWIGLU_LIMIT, SWIGLU_LIMIT)
            act = g * _sigmoid(SWIGLU_ALPHA * g) * (lin + 1.0)
            y = jnp.dot(act.astype(BF16), wd_bf[...], preferred_element_type=F32) + bd_ref[...]
            ybuf[slot, 0:m] = y * gate

        @pl.when(nval_ref[b] > r // 2)
        def _():
            expert_rows(r)

        @pl.when(nval_ref[b] <= r // 2)
        def _():
            expert_rows(r // 2)

        for_each_run(b, start_scatter(slot))

        @pl.when(b == nb - 1)
        def _():
            wait_scatter(slot, nval_ref[b])

            @pl.when(nb >= 2)
            def _():
                wait_scatter(1 - slot, nval_ref[jnp.maximum(b - 1, 0)])

            def tail_body(i, c):
                tail_copy(i).wait()
                return c
            lax.fori_loop(0, n_tiles, tail_body, 0)


def _expert_call(tables, xs, w_gu, b_gu, w_down, b_down):
    nblk = tables[0].shape[0]
    r = ROW_BLOCK
    d = D_MODEL
    bmap = lambda b, be, *_: (be[b], 0, 0)
    grid_spec = pltpu.PrefetchScalarGridSpec(
        num_scalar_prefetch=len(tables),
        grid=(nblk,),
        in_specs=[
            pl.BlockSpec(memory_space=pl.ANY),
            pl.BlockSpec(memory_space=pl.ANY),
            pl.BlockSpec((None, 1, 2 * d), bmap),
            pl.BlockSpec(memory_space=pl.ANY),
            pl.BlockSpec((None, 1, d), bmap),
        ],
        out_specs=pl.BlockSpec(memory_space=pl.ANY),
        scratch_shapes=[
            pltpu.VMEM((2, r, ROW_W), F32),
            pltpu.VMEM((2, r, d), F32),
            pltpu.VMEM((ROW_ALIGN * N_EXPERTS, d), F32),
            pltpu.VMEM((d, 2 * d), F32),
            pltpu.VMEM((d, d), F32),
            pltpu.VMEM((d, 2 * d), BF16),
            pltpu.VMEM((d, d), BF16),
            pltpu.SemaphoreType.DMA((2,)),
            pltpu.SemaphoreType.DMA((2,)),
            pltpu.SemaphoreType.DMA((1,)),
            pltpu.SemaphoreType.DMA((2,)),
        ],
    )
    return pl.pallas_call(
        _expert_kernel,
        grid_spec=grid_spec,
        out_shape=jax.ShapeDtypeStruct((xs.shape[0], d), F32),
        compiler_params=pltpu.CompilerParams(
            dimension_semantics=("arbitrary",), vmem_limit_bytes=VMEM_LIMIT_BYTES),
        name="experts",
    )(*tables, xs, w_gu, b_gu, w_down, b_down)


def _final_kernel(h1_ref, mod_ref, ys_ref, pos_ref, fg_ref, o_ref):
    t = h1_ref.shape[0]
    g2 = mod_ref[5:6, :]
    pos_rows = [pos_ref[k:k + 1, :] for k in range(TOP_K)]
    perm = _perm_matrix(pos_rows, TILE_ROWS, t)
    moe = lax.dot_general(perm, ys_ref[...].astype(BF16), (((0,), (0,)), ((), ())), preferred_element_type=F32)
    o_ref[...] = _rms(h1_ref[...] + g2 * moe, fg_ref[...])


def _final_call(h1, mod3, ys, pos, final_g, bsz, seq):
    n_tok, d = h1.shape
    t = SEQ_TILE
    per_batch = seq // t
    out = pl.pallas_call(
        _final_kernel,
        grid=(n_tok // t,),
        in_specs=[
            pl.BlockSpec((t, d), lambda i: (i, 0)),
            pl.BlockSpec((None, 6, d), lambda i: (i // per_batch, 0, 0)),
            pl.BlockSpec((TILE_ROWS, d), lambda i: (i, 0)),
            pl.BlockSpec((TOP_K, t), lambda i: (0, i)),
            pl.BlockSpec((1, d), lambda i: (0, 0)),
        ],
        out_specs=pl.BlockSpec((t, d), lambda i: (i, 0)),
        out_shape=jax.ShapeDtypeStruct((n_tok, d), F32),
        compiler_params=pltpu.CompilerParams(
            dimension_semantics=("arbitrary",), vmem_limit_bytes=VMEM_LIMIT_BYTES),
        name="combine_final",
    )(h1, mod3, ys, pos, final_g.reshape(1, d))
    return out.reshape(bsz, seq, d)


def _block_diag(w):
    h, i, o = w.shape
    eye = jnp.eye(h, dtype=w.dtype)
    return (w[:, :, None, :] * eye[:, None, :, None]).reshape(h * i, h * o)


def _route_tables(cnt):
    r = ROW_BLOCK
    n_tiles = cnt.shape[0]
    nblk = n_tiles * TILE_ROWS // r + N_EXPERTS
    cnt = (cnt + (ROW_ALIGN - 1)) // ROW_ALIGN * ROW_ALIGN
    total = jnp.sum(cnt, axis=0)
    cum = jnp.cumsum(cnt, axis=0) - cnt
    seg_off = jnp.cumsum(cnt, axis=1) - cnt
    base = seg_off + jnp.arange(n_tiles, dtype=I32)[:, None] * TILE_ROWS
    nblk_e = (total + r - 1) // r
    blk_end = jnp.cumsum(nblk_e)
    blk_start = blk_end - nblk_e
    nb_used = blk_end[-1]
    blk = jnp.arange(nblk, dtype=I32)
    be = jnp.minimum(jnp.sum((blk[:, None] >= blk_end[None, :]).astype(I32), axis=1), N_EXPERTS - 1)
    used = blk < nb_used
    be = jnp.where(used, be, be[jnp.maximum(nb_used - 1, 0)])
    jrow = jnp.where(used, (blk - blk_start[be]) * r, 0)
    nval = jnp.where(used, jnp.clip(total[be] - jrow, 0, r), 0)
    first = jnp.concatenate([jnp.ones((1,), I32), (be[1:] != be[:-1]).astype(I32)])
    nxt = blk_end[be]
    next_e = jnp.where(nxt < nb_used, be[jnp.minimum(nxt, nblk - 1)], -1)
    run_start = cum[:, be].T
    run_end = run_start + cnt[:, be].T
    ilo = jnp.sum((run_end <= jrow[:, None]).astype(I32), axis=1)
    ihi = n_tiles - 1 - jnp.sum((run_start >= (jrow + r)[:, None]).astype(I32), axis=1)
    used_rows = jnp.sum(cnt, axis=1)
    tail_row = jnp.arange(n_tiles, dtype=I32) * TILE_ROWS + used_rows
    tail_len = TILE_ROWS - used_rows
    i32 = lambda v: v.astype(I32)
    return (i32(be), i32(first), i32(nval), i32(jrow), i32(jnp.minimum(ilo, n_tiles - 1)), i32(ihi), i32(next_e),
            i32(nb_used).reshape(1), i32(cnt).reshape(-1), i32(cum).reshape(-1), i32(base).reshape(-1),
            i32(tail_row), i32(tail_len))


def kernel(x, c, ada_w, ada_b, norm1_g, w_in, conv_w, conv_b, lru_wr, lru_br, lru_wi, lru_bi, lru_lambda, sgu_ln_g, sgu_ln_b, sgu_w, sgu_b, gnorm_lru_g, gnorm_sgu_g, w_out, norm2_g, router_w, router_b, exp_w_gu, exp_b_gu, exp_w_down, exp_b_down, final_g):
    bsz, seq, d = x.shape
    depth = ada_w.shape[0]
    assert depth == 1 and d == D_MODEL and seq % SEQ_TILE == 0 and ROW_BLOCK >= ROW_ALIGN * N_EXPERTS
    l = 0
    mod = _adaln_call(c, ada_w[l], ada_b[l])
    mod3 = mod.reshape(bsz, 6, d)

    row = lambda v: v.reshape(1, -1)
    wgate = jnp.concatenate([_block_diag(lru_wr[l]), _block_diag(lru_wi[l])], axis=1).astype(BF16)
    bgate = jnp.concatenate([lru_br[l], lru_bi[l]]).reshape(1, -1)
    sgub_full = jnp.repeat(sgu_b[l].T, SGU_HEAD_DIM, axis=1)
    h1, xs, pos, cnt = _mixer_call(
        x, mod3, row(norm1_g[l]), w_in[l].astype(BF16), conv_w[l], row(conv_b[l]), wgate, bgate,
        row(lru_lambda[l]), row(sgu_ln_g[l]), row(sgu_ln_b[l]), sgu_w[l], sgub_full,
        row(gnorm_lru_g[l]), row(gnorm_sgu_g[l]), w_out[l].astype(BF16), row(norm2_g[l]),
        router_w[l].T, router_b[l].reshape(-1, 1))

    tables = _route_tables(cnt[:, :, 0])
    ys = _expert_call(tables, xs, exp_w_gu[l], exp_b_gu[l].reshape(N_EXPERTS, 1, -1),
                      exp_w_down[l], exp_b_down[l].reshape(N_EXPERTS, 1, -1))
    return _final_call(h1, mod3, ys, pos, final_g, bsz, seq)
```

```python
import jax
import jax.numpy as jnp
from jax import lax
from jax.experimental import pallas as pl
from jax.experimental.pallas import tpu as pltpu

F32 = jnp.float32
BF16 = jnp.bfloat16
I32 = jnp.int32

D_MODEL = 1024
D_LRU = 512
D_SGU = 512
LRU_HEADS = 8
CONV_WIDTH = 4
LRU_C = 8.0
SGU_HEADS = 8
SGU_HEAD_DIM = D_SGU // SGU_HEADS
CHUNK = 128
N_EXPERTS = 32
TOP_K = 4
SWIGLU_ALPHA = 1.702
SWIGLU_LIMIT = 7.0
EPS = 1e-6

LANES = 128
SEQ_TILE = 512
ROW_ALIGN = 8
TILE_ROWS = TOP_K * SEQ_TILE + ROW_ALIGN * N_EXPERTS
PAYLOAD = LANES
ROW_W = D_MODEL + PAYLOAD
ROW_BLOCK = 512
VMEM_LIMIT_BYTES = 56 * 1024 * 1024


def _sigmoid(x):
    return 1.0 / (1.0 + jnp.exp(-x))


def _gelu_tanh(x):
    return 0.5 * x * (1.0 + jnp.tanh(0.7978845608028654 * (x + 0.044715 * (x * x * x))))


def _rms(x, g):
    ms = jnp.mean(x * x, axis=-1, keepdims=True)
    return x * lax.rsqrt(ms + EPS) * g


def _perm_matrix(pos_rows, n_rows, n_cols):
    blk = 256
    rid = lax.broadcasted_iota(I32, (blk, n_cols), 0).astype(F32).astype(BF16)
    one = jnp.ones((), BF16)
    blocks = []
    for q in range(n_rows // blk):
        acc = jnp.zeros((blk, n_cols), BF16)
        for pk in pos_rows:
            in_blk = jnp.where((pk >> 8) == q, pk & (blk - 1), -1).astype(F32).astype(BF16)
            acc = jnp.where(rid == in_blk, one, acc)
        blocks.append(acc)
    return jnp.concatenate(blocks, axis=0)


def _adaln_kernel(c_ref, w_ref, b_ref, o_ref):
    c = c_ref[...]
    ca = c * _sigmoid(c)
    o_ref[...] = jnp.dot(ca.astype(BF16), w_ref[...].astype(BF16), preferred_element_type=F32) + b_ref[...]


def _adaln_call(c, w, b):
    bsz, d = c.shape
    n_out = w.shape[1]
    return pl.pallas_call(
        _adaln_kernel,
        grid=(n_out // d,),
        in_specs=[
            pl.BlockSpec((bsz, d), lambda j: (0, 0)),
            pl.BlockSpec((d, d), lambda j: (0, j)),
            pl.BlockSpec((1, d), lambda j: (0, j)),
        ],
        out_specs=pl.BlockSpec((bsz, d), lambda j: (0, j)),
        out_shape=jax.ShapeDtypeStruct((bsz, n_out), F32),
        name="adaln",
    )(c, w, b.reshape(1, n_out))


SCAN_SEGMENTS = 8
SCAN_PITCH_PAD = 8


def _scan_pitch(t):
    return t // SCAN_SEGMENTS + SCAN_PITCH_PAD


def _linear_scan(a, b, h0, a_buf, b_buf):
    t, c = a.shape
    nseg = SCAN_SEGMENTS
    seg = t // nseg
    pitch = _scan_pitch(t)
    nlb = c // LANES
    for j in range(nlb):
        for s in range(nseg):
            a_buf[j, s * pitch:s * pitch + seg, :] = a[s * seg:(s + 1) * seg, j * LANES:(j + 1) * LANES]
            b_buf[j, s * pitch:s * pitch + seg, :] = b[s * seg:(s + 1) * seg, j * LANES:(j + 1) * LANES]

    def step(g, carry):
        hs, ps = carry
        new_h, new_p = [], []
        for j in range(nlb):
            view = (j, pl.ds(g, nseg, stride=pitch), slice(None))
            ag = a_buf[view]
            hj = ag * hs[j] + b_buf[view]
            pj = ag * ps[j]
            b_buf[view] = hj
            a_buf[view] = pj
            new_h.append(hj)
            new_p.append(pj)
        return tuple(new_h), tuple(new_p)

    init = (tuple(jnp.zeros((nseg, LANES), F32) for _ in range(nlb)),
            tuple(jnp.ones((nseg, LANES), F32) for _ in range(nlb)))
    h_end, p_end = lax.fori_loop(0, seg, step, init, unroll=True)
    h_end = jnp.concatenate(h_end, axis=1)
    p_end = jnp.concatenate(p_end, axis=1)
    state = h0
    out = []
    for s in range(nseg):
        h_loc = jnp.concatenate([b_buf[j, s * pitch:s * pitch + seg, :] for j in range(nlb)], axis=1)
        p_loc = jnp.concatenate([a_buf[j, s * pitch:s * pitch + seg, :] for j in range(nlb)], axis=1)
        out.append(h_loc + p_loc * state)
        state = h_end[s:s + 1] + p_end[s:s + 1] * state
    return jnp.concatenate(out, axis=0), state


def _mixer_kernel(x_ref, mod_ref, n1g_ref, win_ref, convw_ref, convb_ref, wgate_ref, bgate_ref, lam_ref,
                  lng_ref, lnb_ref, sguw_ref, sgub_ref, gl_ref, gs_ref, wout_ref, n2g_ref, rwt_ref, rb_ref,
                  h1_ref, xs_ref, pos_ref, cnt_ref,
                  xa_tail, h_carry, scan_a, scan_b):
    t = x_ref.shape[0]
    ji = pl.program_id(1)

    @pl.when(ji == 0)
    def _():
        xa_tail[...] = jnp.zeros_like(xa_tail)
        h_carry[...] = jnp.zeros_like(h_carry)

    mod = mod_ref[...]
    sh1, sc1, g1 = mod[0:1], mod[1:2], mod[2:3]
    sh2, sc2 = mod[3:4], mod[4:5]

    x = x_ref[...]
    z = _rms(x, n1g_ref[...] * (1.0 + sc1)) + sh1
    proj = jnp.dot(z.astype(BF16), win_ref[...], preferred_element_type=F32)
    xa = proj[:, 0:D_LRU]
    ya = proj[:, D_LRU:2 * D_LRU]
    u = proj[:, 2 * D_LRU:2 * D_LRU + D_SGU]
    v = proj[:, 2 * D_LRU + D_SGU:]

    tail = xa_tail[...]
    row8 = lax.broadcasted_iota(I32, (8, 1), 0)
    xc = xa * convw_ref[CONV_WIDTH - 1:CONV_WIDTH, :] + convb_ref[...]
    for sft in range(1, CONV_WIDTH):
        rolled = pltpu.roll(xa, sft, 0)
        head = jnp.where(row8 < sft, pltpu.roll(tail, sft, 0), rolled[0:8])
        shifted = jnp.concatenate([head, rolled[8:]], axis=0)
        xc = xc + shifted * convw_ref[CONV_WIDTH - 1 - sft:CONV_WIDTH - sft, :]
    xa_tail[...] = xa[t - 8:t]

    gates = jnp.dot(xc.astype(BF16), wgate_ref[...], preferred_element_type=F32) + bgate_ref[...]
    r_gate = _sigmoid(gates[:, 0:D_LRU])
    i_gate = _sigmoid(gates[:, D_LRU:])
    nlam = -lam_ref[...]
    softplus = jnp.maximum(nlam, 0.0) + jnp.log1p(jnp.exp(-jnp.abs(nlam)))
    log_a = (-LRU_C) * r_gate * softplus
    a = jnp.exp(log_a)
    om = -jnp.tanh(log_a) * (a * a + 1.0)
    mult = jnp.where(om > 0.0, om * lax.rsqrt(om), 0.0)
    bterm = mult * i_gate * xc
    h, h_last = _linear_scan(a, bterm, h_carry[...], scan_a, scan_b)
    h_carry[...] = h_last
    o_lru = _rms(h * _gelu_tanh(ya), gl_ref[...])

    ug = _gelu_tanh(u)
    vg = _gelu_tanh(v)
    mu = jnp.mean(vg, axis=-1, keepdims=True)
    vcen = vg - mu
    var = jnp.mean(vcen * vcen, axis=-1, keepdims=True)
    vn = (vcen * lax.rsqrt(var + EPS) * lng_ref[...] + lnb_ref[...]).astype(BF16)
    ri = lax.broadcasted_iota(I32, (CHUNK, CHUNK), 0)
    ci = lax.broadcasted_iota(I32, (CHUNK, CHUNK), 1)
    causal = ri >= ci
    lane = lax.broadcasted_iota(I32, (1, 2 * SGU_HEAD_DIM), 1)
    first_half = lane < SGU_HEAD_DIM
    pair_w = []
    for p in range(SGU_HEADS // 2):
        w0 = jnp.where(causal, sguw_ref[2 * p], 0.0).astype(BF16)
        w1 = jnp.where(causal, sguw_ref[2 * p + 1], 0.0).astype(BF16)
        pair_w.append(jnp.concatenate([w0, w1], axis=1))
    chunks = []
    zero = jnp.zeros((), BF16)
    for n in range(t // CHUNK):
        cols = []
        for p in range(SGU_HEADS // 2):
            blk = vn[n * CHUNK:(n + 1) * CHUNK, p * LANES:(p + 1) * LANES]
            rhs = jnp.concatenate([jnp.where(first_half, blk, zero), jnp.where(first_half, zero, blk)], axis=0)
            cols.append(jnp.dot(pair_w[p], rhs, preferred_element_type=F32))
        chunks.append(jnp.concatenate(cols, axis=1) + sgub_ref[...])
    mixed = jnp.concatenate(chunks, axis=0)
    o_sgu = _rms(ug * mixed, gs_ref[...])

    heads = jnp.concatenate([o_lru, o_sgu], axis=1).astype(BF16)
    h1 = x + g1 * jnp.dot(heads, wout_ref[...], preferred_element_type=F32)
    h1_ref[...] = h1

    z2 = _rms(h1, n2g_ref[...] * (1.0 + sc2)) + sh2
    logits = lax.dot_general(rwt_ref[...], z2, (((1,), (1,)), ((), ())),
                             precision=lax.Precision.HIGHEST, preferred_element_type=F32) + rb_ref[...]
    eidx = lax.broadcasted_iota(I32, (N_EXPERTS, t), 0)
    work = logits
    sel = []
    tops = []
    for k in range(TOP_K):
        m = jnp.max(work, axis=0, keepdims=True)
        idx = jnp.min(jnp.where(work == m, eidx, N_EXPERTS), axis=0, keepdims=True)
        onehot = eidx == idx
        sel.append(onehot)
        tops.append(m)
        work = jnp.where(onehot, -jnp.inf, work)
    exps = [jnp.exp(tk - tops[0]) for tk in tops]
    denom = exps[0] + exps[1] + exps[2] + exps[3]
    chosen = jnp.zeros((N_EXPERTS, t), F32)
    gsel = jnp.zeros((N_EXPERTS, t), F32)
    for k in range(TOP_K):
        chosen = jnp.where(sel[k], 1.0, chosen)
        gsel = jnp.where(sel[k], exps[k] / denom, gsel)

    chosen_b = chosen.astype(BF16)
    si = lax.broadcasted_iota(I32, (t, t), 0)
    ti = lax.broadcasted_iota(I32, (t, t), 1)
    before = jnp.where(si < ti, 1.0, 0.0).astype(BF16)
    excl = jnp.dot(chosen_b, before, preferred_element_type=F32)
    cnt_col = jnp.sum(chosen, axis=1, keepdims=True).astype(I32)
    cnt_ref[...] = jnp.broadcast_to(cnt_col, cnt_ref.shape)
    run_len = ((cnt_col + (ROW_ALIGN - 1)) // ROW_ALIGN * ROW_ALIGN).astype(F32)
    er = lax.broadcasted_iota(I32, (N_EXPERTS, N_EXPERTS), 0)
    ec = lax.broadcasted_iota(I32, (N_EXPERTS, N_EXPERTS), 1)
    lower = jnp.where(ec < er, 1.0, 0.0).astype(BF16)
    run_start = jnp.dot(lower, jnp.broadcast_to(run_len, (N_EXPERTS, LANES)).astype(BF16),
                        preferred_element_type=F32)[:, 0:1]
    posmat = excl + run_start
    pos_rows = []
    for k in range(TOP_K):
        pk = jnp.sum(jnp.where(sel[k], posmat, 0.0), axis=0, keepdims=True).astype(I32)
        pos_ref[k:k + 1, :] = pk
        pos_rows.append(pk)

    perm = _perm_matrix(pos_rows, TILE_ROWS, t)
    xs_ref[:, 0:D_MODEL] = jnp.dot(perm, z2.astype(BF16), preferred_element_type=F32)
    g_hi = gsel.astype(BF16)
    rem = gsel - g_hi.astype(F32)
    g_mid = rem.astype(BF16)
    g_lo = (rem - g_mid.astype(F32)).astype(BF16)
    gp = jnp.concatenate([g_hi, g_mid, g_lo, jnp.zeros((PAYLOAD - 3 * N_EXPERTS, t), BF16)], axis=0)
    xs_ref[:, D_MODEL:] = lax.dot_general(perm, gp, (((1,), (1,)), ((), ())), preferred_element_type=F32)


def _mixer_call(x, mod3, n1g, win, convw, convb, wgate, bgate, lam, lng, lnb, sguw, sgub_full, gl, gs, wout,
                n2g, rwt, rb):
    bsz, seq, d = x.shape
    t = SEQ_TILE
    n_tok = bsz * seq
    tiles = seq // t

    def const(shape):
        return pl.BlockSpec(shape, lambda b, j: (0,) * len(shape))

    return pl.pallas_call(
        _mixer_kernel,
        grid=(bsz, tiles),
        in_specs=[
            pl.BlockSpec((None, t, d), lambda b, j: (b, j, 0)),
            pl.BlockSpec((None, 6, d), lambda b, j: (b, 0, 0)),
            const((1, d)),
            const((d, 2 * d)),
            const((CONV_WIDTH, D_LRU)),
            const((1, D_LRU)),
            const((D_LRU, 2 * D_LRU)),
            const((1, 2 * D_LRU)),
            const((1, D_LRU)),
            const((1, D_SGU)),
            const((1, D_SGU)),
            const((SGU_HEADS, CHUNK, CHUNK)),
            const((CHUNK, D_SGU)),
            const((1, D_LRU)),
            const((1, D_SGU)),
            const((d, d)),
            const((1, d)),
            const((N_EXPERTS, d)),
            const((N_EXPERTS, 1)),
        ],
        out_specs=[
            pl.BlockSpec((t, d), lambda b, j: (b * tiles + j, 0)),
            pl.BlockSpec((TILE_ROWS, ROW_W), lambda b, j: (b * tiles + j, 0)),
            pl.BlockSpec((TOP_K, t), lambda b, j: (0, b * tiles + j)),
            pl.BlockSpec((None, N_EXPERTS, LANES), lambda b, j: (b * tiles + j, 0, 0)),
        ],
        out_shape=[
            jax.ShapeDtypeStruct((n_tok, d), F32),
            jax.ShapeDtypeStruct((n_tok // t * TILE_ROWS, ROW_W), F32),
            jax.ShapeDtypeStruct((TOP_K, n_tok), I32),
            jax.ShapeDtypeStruct((n_tok // t, N_EXPERTS, LANES), I32),
        ],
        scratch_shapes=[pltpu.VMEM((8, D_LRU), F32), pltpu.VMEM((1, D_LRU), F32),
                        pltpu.VMEM((D_LRU // LANES, SCAN_SEGMENTS * _scan_pitch(t), LANES), F32),
                        pltpu.VMEM((D_LRU // LANES, SCAN_SEGMENTS * _scan_pitch(t), LANES), F32)],
        compiler_params=pltpu.CompilerParams(
            dimension_semantics=("arbitrary", "arbitrary"), vmem_limit_bytes=VMEM_LIMIT_BYTES),
        name="mixer_router",
    )(x, mod3, n1g, win, convw, convb, wgate, bgate, lam, lng, lnb, sguw, sgub_full, gl, gs, wout, n2g, rwt, rb)


def _expert_kernel(be_ref, first_ref, nval_ref, jrow_ref, ilo_ref, ihi_ref, nexte_ref, nb_ref,
                   cnt_ref, cum_ref, base_ref, tailrow_ref, taillen_ref,
                   xs_hbm, wgu_hbm, bgu_ref, wd_hbm, bd_ref,
                   ys_hbm,
                   xbuf, ybuf, zbuf, wgu_st, wd_st, wgu_bf, wd_bf, gsem, ssem, zsem, wsem):
    r = xbuf.shape[1]
    n_tiles = tailrow_ref.shape[0]
    b = pl.program_id(0)
    nb = nb_ref[0]
    slot = b & 1

    rows = lambda v: pl.multiple_of(v, ROW_ALIGN)

    def weight_copies(e):
        return (pltpu.make_async_copy(wgu_hbm.at[e], wgu_st, wsem.at[0]),
                pltpu.make_async_copy(wd_hbm.at[e], wd_st, wsem.at[1]))

    def tail_copy(i):
        n = rows(taillen_ref[i])
        return pltpu.make_async_copy(zbuf.at[pl.ds(0, n), :], ys_hbm.at[pl.ds(rows(tailrow_ref[i]), n), :],
                                     zsem.at[0])

    def for_each_run(blk, fn):
        e = be_ref[blk]
        j0 = jrow_ref[blk]

        def tile_body(i, c):
            idx = i * N_EXPERTS + e
            cu = cum_ref[idx]
            lo = jnp.maximum(cu, j0)
            hi = jnp.minimum(cu + cnt_ref[idx], j0 + r)

            @pl.when(hi > lo)
            def _():
                fn(rows(base_ref[idx] + (lo - cu)), rows(lo - j0), rows(hi - lo))
            return c
        lax.fori_loop(ilo_ref[blk], ihi_ref[blk] + 1, tile_body, 0)

    def start_gather(dst_slot):
        def fn(hbm_row, buf_row, n):
            pltpu.make_async_copy(xs_hbm.at[pl.ds(hbm_row, n), :],
                                  xbuf.at[dst_slot, pl.ds(buf_row, n), :], gsem.at[dst_slot]).start()
        return fn

    def start_scatter(src_slot):
        def fn(hbm_row, buf_row, n):
            pltpu.make_async_copy(ybuf.at[src_slot, pl.ds(buf_row, n), :],
                                  ys_hbm.at[pl.ds(hbm_row, n), :], ssem.at[src_slot]).start()
        return fn

    def wait_gather(s, n):
        pltpu.make_async_copy(xs_hbm.at[pl.ds(0, rows(n)), :], xbuf.at[s, pl.ds(0, rows(n)), :], gsem.at[s]).wait()

    def wait_scatter(s, n):
        pltpu.make_async_copy(ybuf.at[s, pl.ds(0, rows(n)), :], ys_hbm.at[pl.ds(0, rows(n)), :], ssem.at[s]).wait()

    @pl.when(b == 0)
    def _():
        for c in weight_copies(be_ref[0]):
            c.start()
        xbuf[...] = jnp.zeros_like(xbuf)
        for_each_run(0, start_gather(0))
        zbuf[...] = jnp.zeros_like(zbuf)

        def tail_body(i, c):
            tail_copy(i).start()
            return c
        lax.fori_loop(0, n_tiles, tail_body, 0)

    @pl.when(b < nb)
    def _():
        wait_gather(slot, nval_ref[b])

        @pl.when(b + 1 < nb)
        def _():
            for_each_run(b + 1, start_gather(1 - slot))

        @pl.when(first_ref[b] == 1)
        def _():
            for c in weight_copies(be_ref[b]):
                c.wait()
            wgu_bf[...] = wgu_st[...].astype(BF16)
            wd_bf[...] = wd_st[...].astype(BF16)

            @pl.when(nexte_ref[b] >= 0)
            def _():
                for c in weight_copies(nexte_ref[b]):
                    c.start()

        @pl.when(b >= 2)
        def _():
            wait_scatter(slot, nval_ref[jnp.maximum(b - 2, 0)])

        def expert_rows(m):
            xw = xbuf[slot, 0:m]
            xb = xw[:, 0:D_MODEL].astype(BF16)
            lane = lax.broadcasted_iota(I32, (1, PAYLOAD), 1)
            gate = jnp.sum(jnp.where((lane & (N_EXPERTS - 1)) == be_ref[b], xw[:, D_MODEL:], 0.0),
                           axis=1, keepdims=True)
            gu = jnp.dot(xb, wgu_bf[...], preferred_element_type=F32) + bgu_ref[...]
            g = jnp.minimum(gu[:, 0:D_MODEL], SWIGLU_LIMIT)
            lin = jnp.clip(gu[:, D_MODEL:], -SWIGLU_LIMIT, SWIGLU_LIMIT)
            act = g * _sigmoid(SWIGLU_ALPHA * g) * (lin + 1.0)
            y = jnp.dot(act.astype(BF16), wd_bf[...], preferred_element_type=F32) + bd_ref[...]
            ybuf[slot, 0:m] = y * gate

        @pl.when(nval_ref[b] > r // 2)
        def _():
            expert_rows(r)

        @pl.when(nval_ref[b] <= r // 2)
        def _():
            expert_rows(r // 2)

        for_each_run(b, start_scatter(slot))

        @pl.when(b == nb - 1)
        def _():
            wait_scatter(slot, nval_ref[b])

            @pl.when(nb >= 2)
            def _():
                wait_scatter(1 - slot, nval_ref[jnp.maximum(b - 1, 0)])

            def tail_body(i, c):
                tail_copy(i).wait()
                return c
            lax.fori_loop(0, n_tiles, tail_body, 0)


def _expert_call(tables, xs, w_gu, b_gu, w_down, b_down):
    nblk = tables[0].shape[0]
    r = ROW_BLOCK
    d = D_MODEL
    bmap = lambda b, be, *_: (be[b], 0, 0)
    grid_spec = pltpu.PrefetchScalarGridSpec(
        num_scalar_prefetch=len(tables),
        grid=(nblk,),
        in_specs=[
            pl.BlockSpec(memory_space=pl.ANY),
            pl.BlockSpec(memory_space=pl.ANY),
            pl.BlockSpec((None, 1, 2 * d), bmap),
            pl.BlockSpec(memory_space=pl.ANY),
            pl.BlockSpec((None, 1, d), bmap),
        ],
        out_specs=pl.BlockSpec(memory_space=pl.ANY),
        scratch_shapes=[
            pltpu.VMEM((2, r, ROW_W), F32),
            pltpu.VMEM((2, r, d), F32),
            pltpu.VMEM((ROW_ALIGN * N_EXPERTS, d), F32),
            pltpu.VMEM((d, 2 * d), F32),
            pltpu.VMEM((d, d), F32),
            pltpu.VMEM((d, 2 * d), BF16),
            pltpu.VMEM((d, d), BF16),
            pltpu.SemaphoreType.DMA((2,)),
            pltpu.SemaphoreType.DMA((2,)),
            pltpu.SemaphoreType.DMA((1,)),
            pltpu.SemaphoreType.DMA((2,)),
        ],
    )
    return pl.pallas_call(
        _expert_kernel,
        grid_spec=grid_spec,
        out_shape=jax.ShapeDtypeStruct((xs.shape[0], d), F32),
        compiler_params=pltpu.CompilerParams(
            dimension_semantics=("arbitrary",), vmem_limit_bytes=VMEM_LIMIT_BYTES),
        name="experts",
    )(*tables, xs, w_gu, b_gu, w_down, b_down)


def _final_kernel(h1_ref, mod_ref, ys_ref, pos_ref, fg_ref, o_ref):
    t = h1_ref.shape[0]
    g2 = mod_ref[5:6, :]
    pos_rows = [pos_ref[k:k + 1, :] for k in range(TOP_K)]
    perm = _perm_matrix(pos_rows, TILE_ROWS, t)
    moe = lax.dot_general(perm, ys_ref[...].astype(BF16), (((0,), (0,)), ((), ())), preferred_element_type=F32)
    o_ref[...] = _rms(h1_ref[...] + g2 * moe, fg_ref[...])


def _final_call(h1, mod3, ys, pos, final_g, bsz, seq):
    n_tok, d = h1.shape
    t = SEQ_TILE
    per_batch = seq // t
    out = pl.pallas_call(
        _final_kernel,
        grid=(n_tok // t,),
        in_specs=[
            pl.BlockSpec((t, d), lambda i: (i, 0)),
            pl.BlockSpec((None, 6, d), lambda i: (i // per_batch, 0, 0)),
            pl.BlockSpec((TILE_ROWS, d), lambda i: (i, 0)),
            pl.BlockSpec((TOP_K, t), lambda i: (0, i)),
            pl.BlockSpec((1, d), lambda i: (0, 0)),
        ],
        out_specs=pl.BlockSpec((t, d), lambda i: (i, 0)),
        out_shape=jax.ShapeDtypeStruct((n_tok, d), F32),
        compiler_params=pltpu.CompilerParams(
            dimension_semantics=("arbitrary",), vmem_limit_bytes=VMEM_LIMIT_BYTES),
        name="combine_final",
    )(h1, mod3, ys, pos, final_g.reshape(1, d))
    return out.reshape(bsz, seq, d)


def _block_diag(w):
    h, i, o = w.shape
    eye = jnp.eye(h, dtype=w.dtype)
    return (w[:, :, None, :] * eye[:, None, :, None]).reshape(h * i, h * o)


def _route_tables(cnt):
    r = ROW_BLOCK
    n_tiles = cnt.shape[0]
    nblk = n_tiles * TILE_ROWS // r + N_EXPERTS
    cnt = (cnt + (ROW_ALIGN - 1)) // ROW_ALIGN * ROW_ALIGN
    total = jnp.sum(cnt, axis=0)
    cum = jnp.cumsum(cnt, axis=0) - cnt
    seg_off = jnp.cumsum(cnt, axis=1) - cnt
    base = seg_off + jnp.arange(n_tiles, dtype=I32)[:, None] * TILE_ROWS
    nblk_e = (total + r - 1) // r
    blk_end = jnp.cumsum(nblk_e)
    blk_start = blk_end - nblk_e
    nb_used = blk_end[-1]
    blk = jnp.arange(nblk, dtype=I32)
    be = jnp.minimum(jnp.sum((blk[:, None] >= blk_end[None, :]).astype(I32), axis=1), N_EXPERTS - 1)
    used = blk < nb_used
    be = jnp.where(used, be, be[jnp.maximum(nb_used - 1, 0)])
    jrow = jnp.where(used, (blk - blk_start[be]) * r, 0)
    nval = jnp.where(used, jnp.clip(total[be] - jrow, 0, r), 0)
    first = jnp.concatenate([jnp.ones((1,), I32), (be[1:] != be[:-1]).astype(I32)])
    nxt = blk_end[be]
    next_e = jnp.where(nxt < nb_used, be[jnp.minimum(nxt, nblk - 1)], -1)
    be_onehot = (be[:, None] == jnp.arange(N_EXPERTS, dtype=I32)[None, :]).astype(I32)
    run_start = jnp.sum(be_onehot[:, None, :] * cum[None, :, :], axis=-1)
    run_end = run_start + jnp.sum(be_onehot[:, None, :] * cnt[None, :, :], axis=-1)
    ilo = jnp.sum((run_end <= jrow[:, None]).astype(I32), axis=1)
    ihi = n_tiles - 1 - jnp.sum((run_start >= (jrow + r)[:, None]).astype(I32), axis=1)
    used_rows = jnp.sum(cnt, axis=1)
    tail_row = jnp.arange(n_tiles, dtype=I32) * TILE_ROWS + used_rows
    tail_len = TILE_ROWS - used_rows
    i32 = lambda v: v.astype(I32)
    return (i32(be), i32(first), i32(nval), i32(jrow), i32(jnp.minimum(ilo, n_tiles - 1)), i32(ihi), i32(next_e),
            i32(nb_used).reshape(1), i32(cnt).reshape(-1), i32(cum).reshape(-1), i32(base).reshape(-1),
            i32(tail_row), i32(tail_len))


def kernel(x, c, ada_w, ada_b, norm1_g, w_in, conv_w, conv_b, lru_wr, lru_br, lru_wi, lru_bi, lru_lambda, sgu_ln_g, sgu_ln_b, sgu_w, sgu_b, gnorm_lru_g, gnorm_sgu_g, w_out, norm2_g, router_w, router_b, exp_w_gu, exp_b_gu, exp_w_down, exp_b_down, final_g):
    bsz, seq, d = x.shape
    depth = ada_w.shape[0]
    assert depth == 1 and d == D_MODEL and seq % SEQ_TILE == 0 and ROW_BLOCK >= ROW_ALIGN * N_EXPERTS
    l = 0
    mod = _adaln_call(c, ada_w[l], ada_b[l])
    mod3 = mod.reshape(bsz, 6, d)

    row = lambda v: v.reshape(1, -1)
    wgate = jnp.concatenate([_block_diag(lru_wr[l]), _block_diag(lru_wi[l])], axis=1).astype(BF16)
    bgate = jnp.concatenate([lru_br[l], lru_bi[l]]).reshape(1, -1)
    sgub_full = jnp.repeat(sgu_b[l].T, SGU_HEAD_DIM, axis=1)
    h1, xs, pos, cnt = _mixer_call(
        x, mod3, row(norm1_g[l]), w_in[l].astype(BF16), conv_w[l], row(conv_b[l]), wgate, bgate,
        row(lru_lambda[l]), row(sgu_ln_g[l]), row(sgu_ln_b[l]), sgu_w[l], sgub_full,
        row(gnorm_lru_g[l]), row(gnorm_sgu_g[l]), w_out[l].astype(BF16), row(norm2_g[l]),
        router_w[l].T, router_b[l].reshape(-1, 1))

    tables = _route_tables(cnt[:, :, 0])
    ys = _expert_call(tables, xs, exp_w_gu[l], exp_b_gu[l].reshape(N_EXPERTS, 1, -1),
                      exp_w_down[l], exp_b_down[l].reshape(N_EXPERTS, 1, -1))
    return _final_call(h1, mod3, ys, pos, final_g, bsz, seq)
```

```python
import functools

import jax
import jax.numpy as jnp
from jax import lax
from jax.experimental import pallas as pl
from jax.experimental.pallas import tpu as pltpu

F32 = jnp.float32
BF16 = jnp.bfloat16
I32 = jnp.int32

D_MODEL = 1024
D_LRU = 512
D_SGU = 512
LRU_HEADS = 8
CONV_WIDTH = 4
LRU_C = 8.0
SGU_HEADS = 8
SGU_HEAD_DIM = D_SGU // SGU_HEADS
CHUNK = 128
N_EXPERTS = 32
TOP_K = 4
SWIGLU_ALPHA = 1.702
SWIGLU_LIMIT = 7.0
EPS = 1e-6

LANES = 128
SEQ_TILE = 512
ROW_ALIGN = 8
TILE_ROWS = TOP_K * SEQ_TILE + ROW_ALIGN * N_EXPERTS
PAYLOAD = LANES
ROW_W = D_MODEL + PAYLOAD
ROW_BLOCK = 512
VMEM_LIMIT_BYTES = 56 * 1024 * 1024


def _sigmoid(x):
    return 1.0 / (1.0 + jnp.exp(-x))


def _gelu_tanh(x):
    return 0.5 * x * (1.0 + jnp.tanh(0.7978845608028654 * (x + 0.044715 * (x * x * x))))


def _rms(x, g):
    ms = jnp.mean(x * x, axis=-1, keepdims=True)
    return x * lax.rsqrt(ms + EPS) * g


PERM_BLOCK = 256


def _perm_blocks(pos_rows, n_rows, n_cols):
    blk = PERM_BLOCK
    rid = lax.broadcasted_iota(I32, (blk, n_cols), 0).astype(F32).astype(BF16)
    one = jnp.ones((), BF16)
    blocks = []
    for q in range(n_rows // blk):
        acc = jnp.zeros((blk, n_cols), BF16)
        for pk in pos_rows:
            in_blk = jnp.where(pk // blk == q, pk % blk, -1).astype(F32).astype(BF16)
            acc = jnp.where(rid == in_blk, one, acc)
        blocks.append(acc)
    return blocks


def _adaln_kernel(c_ref, w_ref, b_ref, o_ref):
    c = c_ref[...]
    ca = c * _sigmoid(c)
    o_ref[...] = jnp.dot(ca.astype(BF16), w_ref[...].astype(BF16), preferred_element_type=F32) + b_ref[...]


def _adaln_call(c, w, b):
    bsz, d = c.shape
    n_out = w.shape[1]
    return pl.pallas_call(
        _adaln_kernel,
        grid=(n_out // d,),
        in_specs=[
            pl.BlockSpec((bsz, d), lambda j: (0, 0)),
            pl.BlockSpec((d, d), lambda j: (0, j)),
            pl.BlockSpec((1, d), lambda j: (0, j)),
        ],
        out_specs=pl.BlockSpec((bsz, d), lambda j: (0, j)),
        out_shape=jax.ShapeDtypeStruct((bsz, n_out), F32),
        name="adaln",
    )(c, w, b.reshape(1, n_out))


SCAN_SEGMENTS = 8
SCAN_PITCH_PAD = 8


def _scan_pitch(t):
    return t // SCAN_SEGMENTS + SCAN_PITCH_PAD


def _linear_scan(a, b, h0, a_buf, b_buf):
    t, c = a.shape
    nseg = SCAN_SEGMENTS
    seg = t // nseg
    pitch = _scan_pitch(t)
    nlb = c // LANES
    for j in range(nlb):
        for s in range(nseg):
            a_buf[j, s * pitch:s * pitch + seg, :] = a[s * seg:(s + 1) * seg, j * LANES:(j + 1) * LANES]
            b_buf[j, s * pitch:s * pitch + seg, :] = b[s * seg:(s + 1) * seg, j * LANES:(j + 1) * LANES]

    def step(g, carry):
        hs, ps = carry
        new_h, new_p = [], []
        for j in range(nlb):
            view = (j, pl.ds(g, nseg, stride=pitch), slice(None))
            ag = a_buf[view]
            hj = ag * hs[j] + b_buf[view]
            pj = ag * ps[j]
            b_buf[view] = hj
            a_buf[view] = pj
            new_h.append(hj)
            new_p.append(pj)
        return tuple(new_h), tuple(new_p)

    init = (tuple(jnp.zeros((nseg, LANES), F32) for _ in range(nlb)),
            tuple(jnp.ones((nseg, LANES), F32) for _ in range(nlb)))
    h_end, p_end = lax.fori_loop(0, seg, step, init, unroll=True)
    h_end = jnp.concatenate(h_end, axis=1)
    p_end = jnp.concatenate(p_end, axis=1)
    state = h0
    out = []
    for s in range(nseg):
        h_loc = jnp.concatenate([b_buf[j, s * pitch:s * pitch + seg, :] for j in range(nlb)], axis=1)
        p_loc = jnp.concatenate([a_buf[j, s * pitch:s * pitch + seg, :] for j in range(nlb)], axis=1)
        out.append(h_loc + p_loc * state)
        state = h_end[s:s + 1] + p_end[s:s + 1] * state
    return jnp.concatenate(out, axis=0), state


def _mixer_kernel(x_ref, mod_ref, modp_ref, n1g_ref, win_ref, convw_ref, convb_ref, wgate_ref, bgate_ref, lam_ref,
                  lng_ref, lnb_ref, sguw_ref, sgub_ref, gl_ref, gs_ref, wout_ref, n2g_ref, rwt_ref, rb_ref,
                  h1_ref, xs_ref, pos_ref, cnt_ref,
                  xa_tail, h_carry, scan_a, scan_b, h1_prev, *, tiles_per_seq):
    t = x_ref.shape[0]
    i = pl.program_id(0)

    @pl.when(i == 0)
    def _():
        h1_prev[...] = jnp.zeros_like(h1_prev)

    @pl.when(i % tiles_per_seq == 0)
    def _():
        xa_tail[...] = jnp.zeros_like(xa_tail)
        h_carry[...] = jnp.zeros_like(h_carry)

    z2b, logits = _router_logits(h1_prev[...], modp_ref[...], n2g_ref, rwt_ref, rb_ref)
    pos_rows, gate_pieces = _route(logits, pos_ref, cnt_ref)

    mod = mod_ref[...]
    sh1, sc1, g1 = mod[0:1], mod[1:2], mod[2:3]

    x = x_ref[...]
    z = _rms(x, n1g_ref[...] * (1.0 + sc1)) + sh1
    proj = jnp.dot(z.astype(BF16), win_ref[...], preferred_element_type=F32)
    perm_blocks = _perm_blocks(pos_rows, TILE_ROWS, t)
    n_pb = len(perm_blocks)
    sort_rows = functools.partial(_sort_rows, perm_blocks, z2b, gate_pieces, xs_ref)
    sort_rows(0, n_pb // 3)
    xa = proj[:, 0:D_LRU]
    ya = proj[:, D_LRU:2 * D_LRU]
    u = proj[:, 2 * D_LRU:2 * D_LRU + D_SGU]
    v = proj[:, 2 * D_LRU + D_SGU:]

    tail = xa_tail[...]
    row8 = lax.broadcasted_iota(I32, (8, 1), 0)
    xc = xa * convw_ref[CONV_WIDTH - 1:CONV_WIDTH, :] + convb_ref[...]
    for sft in range(1, CONV_WIDTH):
        rolled = pltpu.roll(xa, sft, 0)
        head = jnp.where(row8 < sft, pltpu.roll(tail, sft, 0), rolled[0:8])
        shifted = jnp.concatenate([head, rolled[8:]], axis=0)
        xc = xc + shifted * convw_ref[CONV_WIDTH - 1 - sft:CONV_WIDTH - sft, :]
    xa_tail[...] = xa[t - 8:t]

    gates = jnp.dot(xc.astype(BF16), wgate_ref[...], preferred_element_type=F32) + bgate_ref[...]
    sort_rows(n_pb // 3, 2 * n_pb // 3)
    r_gate = _sigmoid(gates[:, 0:D_LRU])
    i_gate = _sigmoid(gates[:, D_LRU:])
    nlam = -lam_ref[...]
    softplus = jnp.maximum(nlam, 0.0) + jnp.log1p(jnp.exp(-jnp.abs(nlam)))
    log_a = (-LRU_C) * r_gate * softplus
    a = jnp.exp(log_a)
    om = -jnp.tanh(log_a) * (a * a + 1.0)
    mult = jnp.where(om > 0.0, om * lax.rsqrt(om), 0.0)
    bterm = mult * i_gate * xc
    sort_rows(2 * n_pb // 3, n_pb)
    h, h_last = _linear_scan(a, bterm, h_carry[...], scan_a, scan_b)
    h_carry[...] = h_last
    o_lru = _rms(h * _gelu_tanh(ya), gl_ref[...])

    ug = _gelu_tanh(u)
    vg = _gelu_tanh(v)
    mu = jnp.mean(vg, axis=-1, keepdims=True)
    vcen = vg - mu
    var = jnp.mean(vcen * vcen, axis=-1, keepdims=True)
    vn = (vcen * lax.rsqrt(var + EPS) * lng_ref[...] + lnb_ref[...]).astype(BF16)
    ri = lax.broadcasted_iota(I32, (CHUNK, CHUNK), 0)
    ci = lax.broadcasted_iota(I32, (CHUNK, CHUNK), 1)
    causal = ri >= ci
    lane = lax.broadcasted_iota(I32, (1, 2 * SGU_HEAD_DIM), 1)
    first_half = lane < SGU_HEAD_DIM
    pair_w = []
    for p in range(SGU_HEADS // 2):
        w0 = jnp.where(causal, sguw_ref[2 * p], 0.0).astype(BF16)
        w1 = jnp.where(causal, sguw_ref[2 * p + 1], 0.0).astype(BF16)
        pair_w.append(jnp.concatenate([w0, w1], axis=1))
    chunks = []
    zero = jnp.zeros((), BF16)
    for n in range(t // CHUNK):
        cols = []
        for p in range(SGU_HEADS // 2):
            blk = vn[n * CHUNK:(n + 1) * CHUNK, p * LANES:(p + 1) * LANES]
            rhs = jnp.concatenate([jnp.where(first_half, blk, zero), jnp.where(first_half, zero, blk)], axis=0)
            cols.append(jnp.dot(pair_w[p], rhs, preferred_element_type=F32))
        chunks.append(jnp.concatenate(cols, axis=1) + sgub_ref[...])
    mixed = jnp.concatenate(chunks, axis=0)
    o_sgu = _rms(ug * mixed, gs_ref[...])

    heads = jnp.concatenate([o_lru, o_sgu], axis=1).astype(BF16)
    h1 = x + g1 * jnp.dot(heads, wout_ref[...], preferred_element_type=F32)
    h1_ref[...] = h1
    h1_prev[...] = h1


def _router_logits(h1, mod, n2g_ref, rwt_ref, rb_ref):
    sh2, sc2 = mod[3:4], mod[4:5]
    z2 = _rms(h1, n2g_ref[...] * (1.0 + sc2)) + sh2
    logits = lax.dot_general(rwt_ref[...], z2, (((1,), (1,)), ((), ())),
                             precision=lax.Precision.HIGHEST, preferred_element_type=F32) + rb_ref[...]
    return z2.astype(BF16), logits


def _route(logits, pos_ref, cnt_ref):
    t = logits.shape[1]
    eidx = lax.broadcasted_iota(I32, (N_EXPERTS, t), 0)
    work = logits
    sel = []
    tops = []
    for k in range(TOP_K):
        m = jnp.max(work, axis=0, keepdims=True)
        idx = jnp.min(jnp.where(work == m, eidx, N_EXPERTS), axis=0, keepdims=True)
        onehot = eidx == idx
        sel.append(onehot)
        tops.append(m)
        work = jnp.where(onehot, -jnp.inf, work)
    exps = [jnp.exp(tk - tops[0]) for tk in tops]
    denom = exps[0] + exps[1] + exps[2] + exps[3]
    chosen = jnp.zeros((N_EXPERTS, t), F32)
    gsel = jnp.zeros((N_EXPERTS, t), F32)
    for k in range(TOP_K):
        chosen = jnp.where(sel[k], 1.0, chosen)
        gsel = jnp.where(sel[k], exps[k] / denom, gsel)

    chosen_b = chosen.astype(BF16)
    si = lax.broadcasted_iota(I32, (t, t), 0)
    ti = lax.broadcasted_iota(I32, (t, t), 1)
    before = jnp.where(si < ti, 1.0, 0.0).astype(BF16)
    excl = jnp.dot(chosen_b, before, preferred_element_type=F32)
    cnt_col = jnp.sum(chosen, axis=1, keepdims=True).astype(I32)
    cnt_ref[...] = jnp.broadcast_to(cnt_col, cnt_ref.shape)
    run_len = ((cnt_col + (ROW_ALIGN - 1)) // ROW_ALIGN * ROW_ALIGN).astype(F32)
    er = lax.broadcasted_iota(I32, (N_EXPERTS, N_EXPERTS), 0)
    ec = lax.broadcasted_iota(I32, (N_EXPERTS, N_EXPERTS), 1)
    lower = jnp.where(ec < er, 1.0, 0.0).astype(BF16)
    run_start = jnp.dot(lower, jnp.broadcast_to(run_len, (N_EXPERTS, LANES)).astype(BF16),
                        preferred_element_type=F32)[:, 0:1]
    posmat = excl + run_start
    pos_rows = []
    for k in range(TOP_K):
        pk = jnp.sum(jnp.where(sel[k], posmat, 0.0), axis=0, keepdims=True).astype(I32)
        pos_ref[k:k + 1, :] = pk
        pos_rows.append(pk)

    g_hi = gsel.astype(BF16)
    rem = gsel - g_hi.astype(F32)
    g_mid = rem.astype(BF16)
    g_lo = (rem - g_mid.astype(F32)).astype(BF16)
    gp = jnp.concatenate([g_hi, g_mid, g_lo, jnp.zeros((PAYLOAD - 3 * N_EXPERTS, t), BF16)], axis=0)
    return pos_rows, gp


def _sort_rows(perm_blocks, z2b, gate_pieces, xs_ref, q0, q1):
    for q in range(q0, q1):
        rows = slice(q * PERM_BLOCK, (q + 1) * PERM_BLOCK)
        xs_ref[rows, 0:D_MODEL] = jnp.dot(perm_blocks[q], z2b, preferred_element_type=F32)
        xs_ref[rows, D_MODEL:] = lax.dot_general(perm_blocks[q], gate_pieces, (((1,), (1,)), ((), ())),
                                                 preferred_element_type=F32)


def _mixer_call(x, mod3, n1g, win, convw, convb, wgate, bgate, lam, lng, lnb, sguw, sgub_full, gl, gs, wout,
                n2g, rwt, rb):
    bsz, seq, d = x.shape
    t = SEQ_TILE
    tiles = seq // t
    n_tiles = bsz * tiles
    x2 = x.reshape(bsz * seq, d)

    def const(shape):
        return pl.BlockSpec(shape, lambda i: (0,) * len(shape))

    mixed = lambda i: jnp.minimum(i, n_tiles - 1)
    routed = lambda i: jnp.maximum(i - 1, 0)
    return pl.pallas_call(
        functools.partial(_mixer_kernel, tiles_per_seq=tiles),
        grid=(n_tiles + 1,),
        in_specs=[
            pl.BlockSpec((t, d), lambda i: (mixed(i), 0)),
            pl.BlockSpec((None, 6, d), lambda i: (mixed(i) // tiles, 0, 0)),
            pl.BlockSpec((None, 6, d), lambda i: (routed(i) // tiles, 0, 0)),
            const((1, d)),
            const((d, 2 * d)),
            const((CONV_WIDTH, D_LRU)),
            const((1, D_LRU)),
            const((D_LRU, 2 * D_LRU)),
            const((1, 2 * D_LRU)),
            const((1, D_LRU)),
            const((1, D_SGU)),
            const((1, D_SGU)),
            const((SGU_HEADS, CHUNK, CHUNK)),
            const((CHUNK, D_SGU)),
            const((1, D_LRU)),
            const((1, D_SGU)),
            const((d, d)),
            const((1, d)),
            const((N_EXPERTS, d)),
            const((N_EXPERTS, 1)),
        ],
        out_specs=[
            pl.BlockSpec((t, d), lambda i: (i, 0)),
            pl.BlockSpec((TILE_ROWS, ROW_W), lambda i: (routed(i), 0)),
            pl.BlockSpec((TOP_K, t), lambda i: (0, routed(i))),
            pl.BlockSpec((None, N_EXPERTS, LANES), lambda i: (routed(i), 0, 0)),
        ],
        out_shape=[
            jax.ShapeDtypeStruct(((n_tiles + 1) * t, d), F32),
            jax.ShapeDtypeStruct((n_tiles * TILE_ROWS, ROW_W), F32),
            jax.ShapeDtypeStruct((TOP_K, n_tiles * t), I32),
            jax.ShapeDtypeStruct((n_tiles, N_EXPERTS, LANES), I32),
        ],
        scratch_shapes=[pltpu.VMEM((8, D_LRU), F32), pltpu.VMEM((1, D_LRU), F32),
                        pltpu.VMEM((D_LRU // LANES, SCAN_SEGMENTS * _scan_pitch(t), LANES), F32),
                        pltpu.VMEM((D_LRU // LANES, SCAN_SEGMENTS * _scan_pitch(t), LANES), F32),
                        pltpu.VMEM((t, d), F32)],
        compiler_params=pltpu.CompilerParams(
            dimension_semantics=("arbitrary",), vmem_limit_bytes=VMEM_LIMIT_BYTES),
        name="mixer_router",
    )(x2, mod3, mod3, n1g, win, convw, convb, wgate, bgate, lam, lng, lnb, sguw, sgub_full, gl, gs, wout, n2g, rwt,
      rb)


def _expert_kernel(be_ref, first_ref, nval_ref, jrow_ref, ilo_ref, ihi_ref, nexte_ref, nb_ref,
                   cnt_ref, cum_ref, base_ref, tailrow_ref, taillen_ref,
                   xs_hbm, wgu_hbm, bgu_ref, wd_hbm, bd_ref,
                   ys_hbm,
                   xbuf, ybuf, zbuf, wgu_st, wd_st, wgu_bf, wd_bf, gsem, ssem, zsem, wsem):
    r = xbuf.shape[1]
    n_tiles = tailrow_ref.shape[0]
    b = pl.program_id(0)
    nb = nb_ref[0]
    slot = b & 1

    rows = lambda v: pl.multiple_of(v, ROW_ALIGN)

    def weight_copies(e):
        return (pltpu.make_async_copy(wgu_hbm.at[e], wgu_st, wsem.at[0]),
                pltpu.make_async_copy(wd_hbm.at[e], wd_st, wsem.at[1]))

    def tail_copy(i):
        n = rows(taillen_ref[i])
        return pltpu.make_async_copy(zbuf.at[pl.ds(0, n), :], ys_hbm.at[pl.ds(rows(tailrow_ref[i]), n), :],
                                     zsem.at[0])

    def for_each_run(blk, fn):
        e = be_ref[blk]
        j0 = jrow_ref[blk]

        def tile_body(i, c):
            idx = i * N_EXPERTS + e
            cu = cum_ref[idx]
            lo = jnp.maximum(cu, j0)
            hi = jnp.minimum(cu + cnt_ref[idx], j0 + r)

            @pl.when(hi > lo)
            def _():
                fn(rows(base_ref[idx] + (lo - cu)), rows(lo - j0), rows(hi - lo))
            return c
        lax.fori_loop(ilo_ref[blk], ihi_ref[blk] + 1, tile_body, 0)

    def start_gather(dst_slot):
        def fn(hbm_row, buf_row, n):
            pltpu.make_async_copy(xs_hbm.at[pl.ds(hbm_row, n), :],
                                  xbuf.at[dst_slot, pl.ds(buf_row, n), :], gsem.at[dst_slot]).start()
        return fn

    def start_scatter(src_slot):
        def fn(hbm_row, buf_row, n):
            pltpu.make_async_copy(ybuf.at[src_slot, pl.ds(buf_row, n), :],
                                  ys_hbm.at[pl.ds(hbm_row, n), :], ssem.at[src_slot]).start()
        return fn

    def wait_gather(s, n):
        pltpu.make_async_copy(xs_hbm.at[pl.ds(0, rows(n)), :], xbuf.at[s, pl.ds(0, rows(n)), :], gsem.at[s]).wait()

    def wait_scatter(s, n):
        pltpu.make_async_copy(ybuf.at[s, pl.ds(0, rows(n)), :], ys_hbm.at[pl.ds(0, rows(n)), :], ssem.at[s]).wait()

    @pl.when(b == 0)
    def _():
        for c in weight_copies(be_ref[0]):
            c.start()
        xbuf[...] = jnp.zeros_like(xbuf)
        for_each_run(0, start_gather(0))
        zbuf[...] = jnp.zeros_like(zbuf)

        def tail_body(i, c):
            tail_copy(i).start()
            return c
        lax.fori_loop(0, n_tiles, tail_body, 0)

    @pl.when(b < nb)
    def _():
        wait_gather(slot, nval_ref[b])

        @pl.when(b + 1 < nb)
        def _():
            for_each_run(b + 1, start_gather(1 - slot))

        @pl.when(first_ref[b] == 1)
        def _():
            for c in weight_copies(be_ref[b]):
                c.wait()
            wgu_bf[...] = wgu_st[...].astype(BF16)
            wd_bf[...] = wd_st[...].astype(BF16)

            @pl.when(nexte_ref[b] >= 0)
            def _():
                for c in weight_copies(nexte_ref[b]):
                    c.start()

        @pl.when(b >= 2)
        def _():
            wait_scatter(slot, nval_ref[jnp.maximum(b - 2, 0)])

        def expert_rows(m):
            xw = xbuf[slot, 0:m]
            xb = xw[:, 0:D_MODEL].astype(BF16)
            lane = lax.broadcasted_iota(I32, (1, PAYLOAD), 1)
            gate = jnp.sum(jnp.where((lane & (N_EXPERTS - 1)) == be_ref[b], xw[:, D_MODEL:], 0.0),
                           axis=1, keepdims=True)
            gu = jnp.dot(xb, wgu_bf[...], preferred_element_type=F32) + bgu_ref[...]
            g = jnp.minimum(gu[:, 0:D_MODEL], SWIGLU_LIMIT)
            lin = jnp.clip(gu[:, D_MODEL:], -SWIGLU_LIMIT, SWIGLU_LIMIT)
            act = g * _sigmoid(SWIGLU_ALPHA * g) * (lin + 1.0)
            y = jnp.dot(act.astype(BF16), wd_bf[...], preferred_element_type=F32) + bd_ref[...]
            ybuf[slot, 0:m] = y * gate

        @pl.when(nval_ref[b] > r // 2)
        def _():
            expert_rows(r)

        @pl.when(nval_ref[b] <= r // 2)
        def _():
            expert_rows(r // 2)

        for_each_run(b, start_scatter(slot))

        @pl.when(b == nb - 1)
        def _():
            wait_scatter(slot, nval_ref[b])

            @pl.when(nb >= 2)
            def _():
                wait_scatter(1 - slot, nval_ref[jnp.maximum(b - 1, 0)])

            def tail_body(i, c):
                tail_copy(i).wait()
                return c
            lax.fori_loop(0, n_tiles, tail_body, 0)


def _expert_call(tables, xs, w_gu, b_gu, w_down, b_down):
    nblk = tables[0].shape[0]
    r = ROW_BLOCK
    d = D_MODEL
    bmap = lambda b, be, *_: (be[b], 0, 0)
    grid_spec = pltpu.PrefetchScalarGridSpec(
        num_scalar_prefetch=len(tables),
        grid=(nblk,),
        in_specs=[
            pl.BlockSpec(memory_space=pl.ANY),
            pl.BlockSpec(memory_space=pl.ANY),
            pl.BlockSpec((None, 1, 2 * d), bmap),
            pl.BlockSpec(memory_space=pl.ANY),
            pl.BlockSpec((None, 1, d), bmap),
        ],
        out_specs=pl.BlockSpec(memory_space=pl.ANY),
        scratch_shapes=[
            pltpu.VMEM((2, r, ROW_W), F32),
            pltpu.VMEM((2, r, d), F32),
            pltpu.VMEM((ROW_ALIGN * N_EXPERTS, d), F32),
            pltpu.VMEM((d, 2 * d), F32),
            pltpu.VMEM((d, d), F32),
            pltpu.VMEM((d, 2 * d), BF16),
            pltpu.VMEM((d, d), BF16),
            pltpu.SemaphoreType.DMA((2,)),
            pltpu.SemaphoreType.DMA((2,)),
            pltpu.SemaphoreType.DMA((1,)),
            pltpu.SemaphoreType.DMA((2,)),
        ],
    )
    return pl.pallas_call(
        _expert_kernel,
        grid_spec=grid_spec,
        out_shape=jax.ShapeDtypeStruct((xs.shape[0], d), F32),
        compiler_params=pltpu.CompilerParams(
            dimension_semantics=("arbitrary",), vmem_limit_bytes=VMEM_LIMIT_BYTES),
        name="experts",
    )(*tables, xs, w_gu, b_gu, w_down, b_down)


def _final_kernel(h1_ref, mod_ref, ys_ref, pos_ref, fg_ref, o_ref):
    t = h1_ref.shape[0]
    g2 = mod_ref[5:6, :]
    pos_rows = [pos_ref[k:k + 1, :] for k in range(TOP_K)]
    perm = jnp.concatenate(_perm_blocks(pos_rows, TILE_ROWS, t), axis=0)
    moe = lax.dot_general(perm, ys_ref[...].astype(BF16), (((0,), (0,)), ((), ())), preferred_element_type=F32)
    o_ref[...] = _rms(h1_ref[...] + g2 * moe, fg_ref[...])


def _final_call(h1, mod3, ys, pos, final_g, bsz, seq):
    d = h1.shape[1]
    n_tok = bsz * seq
    t = SEQ_TILE
    per_batch = seq // t
    out = pl.pallas_call(
        _final_kernel,
        grid=(n_tok // t,),
        in_specs=[
            pl.BlockSpec((t, d), lambda i: (i, 0)),
            pl.BlockSpec((None, 6, d), lambda i: (i // per_batch, 0, 0)),
            pl.BlockSpec((TILE_ROWS, d), lambda i: (i, 0)),
            pl.BlockSpec((TOP_K, t), lambda i: (0, i)),
            pl.BlockSpec((1, d), lambda i: (0, 0)),
        ],
        out_specs=pl.BlockSpec((t, d), lambda i: (i, 0)),
        out_shape=jax.ShapeDtypeStruct((n_tok, d), F32),
        compiler_params=pltpu.CompilerParams(
            dimension_semantics=("arbitrary",), vmem_limit_bytes=VMEM_LIMIT_BYTES),
        name="combine_final",
    )(h1, mod3, ys, pos, final_g.reshape(1, d))
    return out.reshape(bsz, seq, d)


def _block_diag(w):
    h, i, o = w.shape
    eye = jnp.eye(h, dtype=w.dtype)
    return (w[:, :, None, :] * eye[:, None, :, None]).reshape(h * i, h * o)


def _route_tables(cnt):
    r = ROW_BLOCK
    n_tiles = cnt.shape[0]
    nblk = n_tiles * TILE_ROWS // r + N_EXPERTS
    cnt = (cnt + (ROW_ALIGN - 1)) // ROW_ALIGN * ROW_ALIGN
    ie = jnp.arange(N_EXPERTS, dtype=I32)
    it = jnp.arange(n_tiles, dtype=I32)
    e_before = (ie[:, None] < ie[None, :]).astype(I32)
    t_before = (it[:, None] < it[None, :]).astype(I32)
    total = jnp.sum(cnt, axis=0)
    cum = jnp.sum(t_before[:, :, None] * cnt[:, None, :], axis=0)
    seg_off = jnp.sum(cnt[:, :, None] * e_before[None, :, :], axis=1)
    base = seg_off + it[:, None] * TILE_ROWS
    nblk_e = (total + r - 1) // r
    blk_start = jnp.sum(nblk_e[:, None] * e_before, axis=0)
    blk_end = blk_start + nblk_e
    nb_used = jnp.sum(nblk_e)
    blk = jnp.arange(nblk, dtype=I32)
    used = blk < nb_used
    last_e = jnp.max(jnp.where(nblk_e > 0, ie, 0))
    be = jnp.minimum(jnp.sum((blk[:, None] >= blk_end[None, :]).astype(I32), axis=1), N_EXPERTS - 1)
    be = jnp.where(used, be, last_e)
    be_onehot = (be[:, None] == ie[None, :]).astype(I32)
    pick = lambda v: jnp.sum(be_onehot * v[None, :], axis=1)
    jrow = jnp.where(used, (blk - pick(blk_start)) * r, 0)
    nval = jnp.where(used, jnp.clip(pick(total) - jrow, 0, r), 0)
    first = jnp.concatenate([jnp.ones((1,), I32), (be[1:] != be[:-1]).astype(I32)])
    nxt = pick(blk_end)
    next_e = jnp.where(nxt < nb_used, jnp.sum((nxt[:, None] >= blk_end[None, :]).astype(I32), axis=1), -1)
    run_start = jnp.sum(be_onehot[:, None, :] * cum[None, :, :], axis=-1)
    run_end = run_start + jnp.sum(be_onehot[:, None, :] * cnt[None, :, :], axis=-1)
    ilo = jnp.sum((run_end <= jrow[:, None]).astype(I32), axis=1)
    ihi = n_tiles - 1 - jnp.sum((run_start >= (jrow + r)[:, None]).astype(I32), axis=1)
    used_rows = jnp.sum(cnt, axis=1)
    tail_row = jnp.arange(n_tiles, dtype=I32) * TILE_ROWS + used_rows
    tail_len = TILE_ROWS - used_rows
    i32 = lambda v: v.astype(I32)
    return (i32(be), i32(first), i32(nval), i32(jrow), i32(jnp.minimum(ilo, n_tiles - 1)), i32(ihi), i32(next_e),
            i32(nb_used).reshape(1), i32(cnt).reshape(-1), i32(cum).reshape(-1), i32(base).reshape(-1),
            i32(tail_row), i32(tail_len))


def kernel(x, c, ada_w, ada_b, norm1_g, w_in, conv_w, conv_b, lru_wr, lru_br, lru_wi, lru_bi, lru_lambda, sgu_ln_g, sgu_ln_b, sgu_w, sgu_b, gnorm_lru_g, gnorm_sgu_g, w_out, norm2_g, router_w, router_b, exp_w_gu, exp_b_gu, exp_w_down, exp_b_down, final_g):
    bsz, seq, d = x.shape
    depth = ada_w.shape[0]
    assert depth == 1 and d == D_MODEL and seq % SEQ_TILE == 0 and ROW_BLOCK >= ROW_ALIGN * N_EXPERTS
    l = 0
    mod = _adaln_call(c, ada_w[l], ada_b[l])
    mod3 = mod.reshape(bsz, 6, d)

    row = lambda v: v.reshape(1, -1)
    wgate = jnp.concatenate([_block_diag(lru_wr[l]), _block_diag(lru_wi[l])], axis=1).astype(BF16)
    bgate = jnp.concatenate([lru_br[l], lru_bi[l]]).reshape(1, -1)
    sgub_full = jnp.repeat(sgu_b[l].T, SGU_HEAD_DIM, axis=1)
    h1, xs, pos, cnt = _mixer_call(
        x, mod3, row(norm1_g[l]), w_in[l].astype(BF16), conv_w[l], row(conv_b[l]), wgate, bgate,
        row(lru_lambda[l]), row(sgu_ln_g[l]), row(sgu_ln_b[l]), sgu_w[l], sgub_full,
        row(gnorm_lru_g[l]), row(gnorm_sgu_g[l]), w_out[l].astype(BF16), row(norm2_g[l]),
        router_w[l].T, router_b[l].reshape(-1, 1))

    tables = _route_tables(cnt[:, :, 0])
    ys = _expert_call(tables, xs, exp_w_gu[l], exp_b_gu[l].reshape(N_EXPERTS, 1, -1),
                      exp_w_down[l], exp_b_down[l].reshape(N_EXPERTS, 1, -1))
    return _final_call(h1, mod3, ys, pos, final_g, bsz, seq)
```

```python
import functools

import jax
import jax.numpy as jnp
from jax import lax
from jax.experimental import pallas as pl
from jax.experimental.pallas import tpu as pltpu

F32 = jnp.float32
BF16 = jnp.bfloat16
I32 = jnp.int32

D_MODEL = 1024
D_LRU = 512
D_SGU = 512
LRU_HEADS = 8
CONV_WIDTH = 4
LRU_C = 8.0
SGU_HEADS = 8
SGU_HEAD_DIM = D_SGU // SGU_HEADS
CHUNK = 128
N_EXPERTS = 32
TOP_K = 4
SWIGLU_ALPHA = 1.702
SWIGLU_LIMIT = 7.0
EPS = 1e-6

LANES = 128
SEQ_TILE = 512
ROW_ALIGN = 8
TILE_ROWS = TOP_K * SEQ_TILE + ROW_ALIGN * N_EXPERTS
PAYLOAD = LANES
ROW_W = D_MODEL + PAYLOAD
ROW_BLOCK = 512
VMEM_LIMIT_BYTES = 56 * 1024 * 1024


def _sigmoid(x):
    return 1.0 / (1.0 + jnp.exp(-x))


def _gelu_tanh(x):
    return 0.5 * x * (1.0 + jnp.tanh(0.7978845608028654 * (x + 0.044715 * (x * x * x))))


def _rms(x, g):
    ms = jnp.mean(x * x, axis=-1, keepdims=True)
    return x * lax.rsqrt(ms + EPS) * g


PERM_BLOCK = 256


def _perm_blocks(pos_rows, n_rows, n_cols):
    blk = PERM_BLOCK
    rid = lax.broadcasted_iota(I32, (blk, n_cols), 0).astype(F32).astype(BF16)
    one = jnp.ones((), BF16)
    blocks = []
    for q in range(n_rows // blk):
        acc = jnp.zeros((blk, n_cols), BF16)
        for pk in pos_rows:
            in_blk = jnp.where(pk // blk == q, pk % blk, -1).astype(F32).astype(BF16)
            acc = jnp.where(rid == in_blk, one, acc)
        blocks.append(acc)
    return blocks


def _adaln_kernel(c_ref, w_ref, b_ref, o_ref):
    c = c_ref[...]
    ca = c * _sigmoid(c)
    o_ref[...] = jnp.dot(ca.astype(BF16), w_ref[...].astype(BF16), preferred_element_type=F32) + b_ref[...]


def _adaln_call(c, w, b):
    bsz, d = c.shape
    n_out = w.shape[1]
    return pl.pallas_call(
        _adaln_kernel,
        grid=(n_out // d,),
        in_specs=[
            pl.BlockSpec((bsz, d), lambda j: (0, 0)),
            pl.BlockSpec((d, d), lambda j: (0, j)),
            pl.BlockSpec((1, d), lambda j: (0, j)),
        ],
        out_specs=pl.BlockSpec((bsz, d), lambda j: (0, j)),
        out_shape=jax.ShapeDtypeStruct((bsz, n_out), F32),
        name="adaln",
    )(c, w, b.reshape(1, n_out))


SCAN_SEGMENTS = 8
SCAN_PITCH_PAD = 8


def _scan_pitch(t):
    return t // SCAN_SEGMENTS + SCAN_PITCH_PAD


def _linear_scan(a, b, h0, a_buf, b_buf):
    t, c = a.shape
    nseg = SCAN_SEGMENTS
    seg = t // nseg
    pitch = _scan_pitch(t)
    nlb = c // LANES
    for j in range(nlb):
        for s in range(nseg):
            a_buf[j, s * pitch:s * pitch + seg, :] = a[s * seg:(s + 1) * seg, j * LANES:(j + 1) * LANES]
            b_buf[j, s * pitch:s * pitch + seg, :] = b[s * seg:(s + 1) * seg, j * LANES:(j + 1) * LANES]

    def step(g, carry):
        hs, ps = carry
        new_h, new_p = [], []
        for j in range(nlb):
            view = (j, pl.ds(g, nseg, stride=pitch), slice(None))
            ag = a_buf[view]
            hj = ag * hs[j] + b_buf[view]
            pj = ag * ps[j]
            b_buf[view] = hj
            a_buf[view] = pj
            new_h.append(hj)
            new_p.append(pj)
        return tuple(new_h), tuple(new_p)

    init = (tuple(jnp.zeros((nseg, LANES), F32) for _ in range(nlb)),
            tuple(jnp.ones((nseg, LANES), F32) for _ in range(nlb)))
    h_end, p_end = lax.fori_loop(0, seg, step, init, unroll=True)
    h_end = jnp.concatenate(h_end, axis=1)
    p_end = jnp.concatenate(p_end, axis=1)
    state = h0
    out = []
    for s in range(nseg):
        h_loc = jnp.concatenate([b_buf[j, s * pitch:s * pitch + seg, :] for j in range(nlb)], axis=1)
        p_loc = jnp.concatenate([a_buf[j, s * pitch:s * pitch + seg, :] for j in range(nlb)], axis=1)
        out.append(h_loc + p_loc * state)
        state = h_end[s:s + 1] + p_end[s:s + 1] * state
    return jnp.concatenate(out, axis=0), state


def _mixer_kernel(x_ref, mod_ref, modp_ref, n1g_ref, win_ref, convw_ref, convb_ref, wgate_ref, bgate_ref, lam_ref,
                  lng_ref, lnb_ref, sguw_ref, sgub_ref, gl_ref, gs_ref, wout_ref, n2g_ref, rwt_ref, rb_ref,
                  h1_ref, xs_ref, pos_ref, cnt_ref,
                  xa_tail, h_carry, scan_a, scan_b, h1_prev, *, tiles_per_seq):
    t = x_ref.shape[0]
    i = pl.program_id(0)

    @pl.when(i == 0)
    def _():
        h1_prev[...] = jnp.zeros_like(h1_prev)

    @pl.when(i % tiles_per_seq == 0)
    def _():
        xa_tail[...] = jnp.zeros_like(xa_tail)
        h_carry[...] = jnp.zeros_like(h_carry)

    z2b, logits = _router_logits(h1_prev[...], modp_ref[...], n2g_ref, rwt_ref, rb_ref)
    pos_rows, gate_pieces = _route(logits, pos_ref, cnt_ref)

    mod = mod_ref[...]
    sh1, sc1, g1 = mod[0:1], mod[1:2], mod[2:3]

    x = x_ref[...]
    z = _rms(x, n1g_ref[...] * (1.0 + sc1)) + sh1
    proj = jnp.dot(z.astype(BF16), win_ref[...], preferred_element_type=F32)
    perm_blocks = _perm_blocks(pos_rows, TILE_ROWS, t)
    n_pb = len(perm_blocks)
    sort_rows = functools.partial(_sort_rows, perm_blocks, z2b, gate_pieces, xs_ref)
    sort_rows(0, n_pb // 3)
    xa = proj[:, 0:D_LRU]
    ya = proj[:, D_LRU:2 * D_LRU]
    u = proj[:, 2 * D_LRU:2 * D_LRU + D_SGU]
    v = proj[:, 2 * D_LRU + D_SGU:]

    tail = xa_tail[...]
    row8 = lax.broadcasted_iota(I32, (8, 1), 0)
    xc = xa * convw_ref[CONV_WIDTH - 1:CONV_WIDTH, :] + convb_ref[...]
    for sft in range(1, CONV_WIDTH):
        rolled = pltpu.roll(xa, sft, 0)
        head = jnp.where(row8 < sft, pltpu.roll(tail, sft, 0), rolled[0:8])
        shifted = jnp.concatenate([head, rolled[8:]], axis=0)
        xc = xc + shifted * convw_ref[CONV_WIDTH - 1 - sft:CONV_WIDTH - sft, :]
    xa_tail[...] = xa[t - 8:t]

    gates = jnp.dot(xc.astype(BF16), wgate_ref[...], preferred_element_type=F32) + bgate_ref[...]
    sort_rows(n_pb // 3, 2 * n_pb // 3)
    r_gate = _sigmoid(gates[:, 0:D_LRU])
    i_gate = _sigmoid(gates[:, D_LRU:])
    nlam = -lam_ref[...]
    softplus = jnp.maximum(nlam, 0.0) + jnp.log1p(jnp.exp(-jnp.abs(nlam)))
    log_a = (-LRU_C) * r_gate * softplus
    a = jnp.exp(log_a)
    om = -jnp.tanh(log_a) * (a * a + 1.0)
    mult = jnp.where(om > 0.0, om * lax.rsqrt(om), 0.0)
    bterm = mult * i_gate * xc
    sort_rows(2 * n_pb // 3, n_pb)
    h, h_last = _linear_scan(a, bterm, h_carry[...], scan_a, scan_b)
    h_carry[...] = h_last
    o_lru = _rms(h * _gelu_tanh(ya), gl_ref[...])

    ug = _gelu_tanh(u)
    vg = _gelu_tanh(v)
    mu = jnp.mean(vg, axis=-1, keepdims=True)
    vcen = vg - mu
    var = jnp.mean(vcen * vcen, axis=-1, keepdims=True)
    vn = (vcen * lax.rsqrt(var + EPS) * lng_ref[...] + lnb_ref[...]).astype(BF16)
    ri = lax.broadcasted_iota(I32, (CHUNK, CHUNK), 0)
    ci = lax.broadcasted_iota(I32, (CHUNK, CHUNK), 1)
    causal = ri >= ci
    lane = lax.broadcasted_iota(I32, (1, 2 * SGU_HEAD_DIM), 1)
    first_half = lane < SGU_HEAD_DIM
    pair_w = []
    for p in range(SGU_HEADS // 2):
        w0 = jnp.where(causal, sguw_ref[2 * p], 0.0).astype(BF16)
        w1 = jnp.where(causal, sguw_ref[2 * p + 1], 0.0).astype(BF16)
        pair_w.append(jnp.concatenate([w0, w1], axis=1))
    chunks = []
    zero = jnp.zeros((), BF16)
    for n in range(t // CHUNK):
        cols = []
        for p in range(SGU_HEADS // 2):
            blk = vn[n * CHUNK:(n + 1) * CHUNK, p * LANES:(p + 1) * LANES]
            rhs = jnp.concatenate([jnp.where(first_half, blk, zero), jnp.where(first_half, zero, blk)], axis=0)
            cols.append(jnp.dot(pair_w[p], rhs, preferred_element_type=F32))
        chunks.append(jnp.concatenate(cols, axis=1) + sgub_ref[...])
    mixed = jnp.concatenate(chunks, axis=0)
    o_sgu = _rms(ug * mixed, gs_ref[...])

    heads = jnp.concatenate([o_lru, o_sgu], axis=1).astype(BF16)
    h1 = x + g1 * jnp.dot(heads, wout_ref[...], preferred_element_type=F32)
    h1_ref[...] = h1
    h1_prev[...] = h1


def _router_logits(h1, mod, n2g_ref, rwt_ref, rb_ref):
    sh2, sc2 = mod[3:4], mod[4:5]
    z2b = (_rms(h1, n2g_ref[...] * (1.0 + sc2)) + sh2).astype(BF16)
    logits = lax.dot_general(rwt_ref[...].astype(BF16), z2b, (((1,), (1,)), ((), ())),
                             preferred_element_type=F32) + rb_ref[...]
    return z2b, logits


def _route(logits, pos_ref, cnt_ref):
    t = logits.shape[1]
    eidx = lax.broadcasted_iota(I32, (N_EXPERTS, t), 0)
    work = logits
    sel = []
    tops = []
    for k in range(TOP_K):
        m = jnp.max(work, axis=0, keepdims=True)
        idx = jnp.min(jnp.where(work == m, eidx, N_EXPERTS), axis=0, keepdims=True)
        onehot = eidx == idx
        sel.append(onehot)
        tops.append(m)
        work = jnp.where(onehot, -jnp.inf, work)
    exps = [jnp.exp(tk - tops[0]) for tk in tops]
    denom = exps[0] + exps[1] + exps[2] + exps[3]
    chosen = jnp.zeros((N_EXPERTS, t), F32)
    gsel = jnp.zeros((N_EXPERTS, t), F32)
    for k in range(TOP_K):
        chosen = jnp.where(sel[k], 1.0, chosen)
        gsel = jnp.where(sel[k], exps[k] / denom, gsel)

    chosen_b = chosen.astype(BF16)
    si = lax.broadcasted_iota(I32, (t, t), 0)
    ti = lax.broadcasted_iota(I32, (t, t), 1)
    before = jnp.where(si < ti, 1.0, 0.0).astype(BF16)
    excl = jnp.dot(chosen_b, before, preferred_element_type=F32)
    cnt_col = jnp.sum(chosen, axis=1, keepdims=True).astype(I32)
    cnt_ref[...] = jnp.broadcast_to(cnt_col, cnt_ref.shape)
    run_len = ((cnt_col + (ROW_ALIGN - 1)) // ROW_ALIGN * ROW_ALIGN).astype(F32)
    er = lax.broadcasted_iota(I32, (N_EXPERTS, N_EXPERTS), 0)
    ec = lax.broadcasted_iota(I32, (N_EXPERTS, N_EXPERTS), 1)
    lower = jnp.where(ec < er, 1.0, 0.0).astype(BF16)
    run_start = jnp.dot(lower, jnp.broadcast_to(run_len, (N_EXPERTS, LANES)).astype(BF16),
                        preferred_element_type=F32)[:, 0:1]
    posmat = excl + run_start
    pos_rows = []
    for k in range(TOP_K):
        pk = jnp.sum(jnp.where(sel[k], posmat, 0.0), axis=0, keepdims=True).astype(I32)
        pos_ref[k:k + 1, :] = pk
        pos_rows.append(pk)

    g_hi = gsel.astype(BF16)
    rem = gsel - g_hi.astype(F32)
    g_mid = rem.astype(BF16)
    g_lo = (rem - g_mid.astype(F32)).astype(BF16)
    gp = jnp.concatenate([g_hi, g_mid, g_lo, jnp.zeros((PAYLOAD - 3 * N_EXPERTS, t), BF16)], axis=0)
    return pos_rows, gp


def _sort_rows(perm_blocks, z2b, gate_pieces, xs_ref, q0, q1):
    for q in range(q0, q1):
        rows = slice(q * PERM_BLOCK, (q + 1) * PERM_BLOCK)
        xs_ref[rows, 0:D_MODEL] = jnp.dot(perm_blocks[q], z2b, preferred_element_type=F32)
        xs_ref[rows, D_MODEL:] = lax.dot_general(perm_blocks[q], gate_pieces, (((1,), (1,)), ((), ())),
                                                 preferred_element_type=F32)


def _mixer_call(x, mod3, n1g, win, convw, convb, wgate, bgate, lam, lng, lnb, sguw, sgub_full, gl, gs, wout,
                n2g, rwt, rb):
    bsz, seq, d = x.shape
    t = SEQ_TILE
    tiles = seq // t
    n_tiles = bsz * tiles
    x2 = x.reshape(bsz * seq, d)

    def const(shape):
        return pl.BlockSpec(shape, lambda i: (0,) * len(shape))

    mixed = lambda i: jnp.minimum(i, n_tiles - 1)
    routed = lambda i: jnp.maximum(i - 1, 0)
    return pl.pallas_call(
        functools.partial(_mixer_kernel, tiles_per_seq=tiles),
        grid=(n_tiles + 1,),
        in_specs=[
            pl.BlockSpec((t, d), lambda i: (mixed(i), 0)),
            pl.BlockSpec((None, 6, d), lambda i: (mixed(i) // tiles, 0, 0)),
            pl.BlockSpec((None, 6, d), lambda i: (routed(i) // tiles, 0, 0)),
            const((1, d)),
            const((d, 2 * d)),
            const((CONV_WIDTH, D_LRU)),
            const((1, D_LRU)),
            const((D_LRU, 2 * D_LRU)),
            const((1, 2 * D_LRU)),
            const((1, D_LRU)),
            const((1, D_SGU)),
            const((1, D_SGU)),
            const((SGU_HEADS, CHUNK, CHUNK)),
            const((CHUNK, D_SGU)),
            const((1, D_LRU)),
            const((1, D_SGU)),
            const((d, d)),
            const((1, d)),
            const((N_EXPERTS, d)),
            const((N_EXPERTS, 1)),
        ],
        out_specs=[
            pl.BlockSpec((t, d), lambda i: (i, 0)),
            pl.BlockSpec((TILE_ROWS, ROW_W), lambda i: (routed(i), 0)),
            pl.BlockSpec((TOP_K, t), lambda i: (0, routed(i))),
            pl.BlockSpec((None, N_EXPERTS, LANES), lambda i: (routed(i), 0, 0)),
        ],
        out_shape=[
            jax.ShapeDtypeStruct(((n_tiles + 1) * t, d), F32),
            jax.ShapeDtypeStruct((n_tiles * TILE_ROWS, ROW_W), F32),
            jax.ShapeDtypeStruct((TOP_K, n_tiles * t), I32),
            jax.ShapeDtypeStruct((n_tiles, N_EXPERTS, LANES), I32),
        ],
        scratch_shapes=[pltpu.VMEM((8, D_LRU), F32), pltpu.VMEM((1, D_LRU), F32),
                        pltpu.VMEM((D_LRU // LANES, SCAN_SEGMENTS * _scan_pitch(t), LANES), F32),
                        pltpu.VMEM((D_LRU // LANES, SCAN_SEGMENTS * _scan_pitch(t), LANES), F32),
                        pltpu.VMEM((t, d), F32)],
        compiler_params=pltpu.CompilerParams(
            dimension_semantics=("arbitrary",), vmem_limit_bytes=VMEM_LIMIT_BYTES),
        name="mixer_router",
    )(x2, mod3, mod3, n1g, win, convw, convb, wgate, bgate, lam, lng, lnb, sguw, sgub_full, gl, gs, wout, n2g, rwt,
      rb)


def _expert_kernel(be_ref, first_ref, nval_ref, jrow_ref, ilo_ref, ihi_ref, nexte_ref, nb_ref,
                   cnt_ref, cum_ref, base_ref, tailrow_ref, taillen_ref,
                   xs_hbm, wgu_hbm, bgu_ref, wd_hbm, bd_ref,
                   ys_hbm,
                   xbuf, ybuf, zbuf, wgu_st, wd_st, wgu_bf, wd_bf, gsem, ssem, zsem, wsem):
    r = xbuf.shape[1]
    n_tiles = tailrow_ref.shape[0]
    b = pl.program_id(0)
    nb = nb_ref[0]
    slot = b & 1

    rows = lambda v: pl.multiple_of(v, ROW_ALIGN)

    def weight_copies(e):
        return (pltpu.make_async_copy(wgu_hbm.at[e], wgu_st, wsem.at[0]),
                pltpu.make_async_copy(wd_hbm.at[e], wd_st, wsem.at[1]))

    def tail_copy(i):
        n = rows(taillen_ref[i])
        return pltpu.make_async_copy(zbuf.at[pl.ds(0, n), :], ys_hbm.at[pl.ds(rows(tailrow_ref[i]), n), :],
                                     zsem.at[0])

    def for_each_run(blk, fn):
        e = be_ref[blk]
        j0 = jrow_ref[blk]

        def tile_body(i, c):
            idx = i * N_EXPERTS + e
            cu = cum_ref[idx]
            lo = jnp.maximum(cu, j0)
            hi = jnp.minimum(cu + cnt_ref[idx], j0 + r)

            @pl.when(hi > lo)
            def _():
                fn(rows(base_ref[idx] + (lo - cu)), rows(lo - j0), rows(hi - lo))
            return c
        lax.fori_loop(ilo_ref[blk], ihi_ref[blk] + 1, tile_body, 0)

    def start_gather(dst_slot):
        def fn(hbm_row, buf_row, n):
            pltpu.make_async_copy(xs_hbm.at[pl.ds(hbm_row, n), :],
                                  xbuf.at[dst_slot, pl.ds(buf_row, n), :], gsem.at[dst_slot]).start()
        return fn

    def start_scatter(src_slot):
        def fn(hbm_row, buf_row, n):
            pltpu.make_async_copy(ybuf.at[src_slot, pl.ds(buf_row, n), :],
                                  ys_hbm.at[pl.ds(hbm_row, n), :], ssem.at[src_slot]).start()
        return fn

    def wait_gather(s, n):
        pltpu.make_async_copy(xs_hbm.at[pl.ds(0, rows(n)), :], xbuf.at[s, pl.ds(0, rows(n)), :], gsem.at[s]).wait()

    def wait_scatter(s, n):
        pltpu.make_async_copy(ybuf.at[s, pl.ds(0, rows(n)), :], ys_hbm.at[pl.ds(0, rows(n)), :], ssem.at[s]).wait()

    @pl.when(b == 0)
    def _():
        for c in weight_copies(be_ref[0]):
            c.start()
        xbuf[...] = jnp.zeros_like(xbuf)
        for_each_run(0, start_gather(0))
        zbuf[...] = jnp.zeros_like(zbuf)

        def tail_body(i, c):
            tail_copy(i).start()
            return c
        lax.fori_loop(0, n_tiles, tail_body, 0)

    @pl.when(b < nb)
    def _():
        wait_gather(slot, nval_ref[b])

        @pl.when(b + 1 < nb)
        def _():
            for_each_run(b + 1, start_gather(1 - slot))

        @pl.when(first_ref[b] == 1)
        def _():
            for c in weight_copies(be_ref[b]):
                c.wait()
            wgu_bf[...] = wgu_st[...].astype(BF16)
            wd_bf[...] = wd_st[...].astype(BF16)

            @pl.when(nexte_ref[b] >= 0)
            def _():
                for c in weight_copies(nexte_ref[b]):
                    c.start()

        @pl.when(b >= 2)
        def _():
            wait_scatter(slot, nval_ref[jnp.maximum(b - 2, 0)])

        def expert_rows(m):
            xw = xbuf[slot, 0:m]
            xb = xw[:, 0:D_MODEL].astype(BF16)
            lane = lax.broadcasted_iota(I32, (1, PAYLOAD), 1)
            gate = jnp.sum(jnp.where((lane & (N_EXPERTS - 1)) == be_ref[b], xw[:, D_MODEL:], 0.0),
                           axis=1, keepdims=True)
            gu = jnp.dot(xb, wgu_bf[...], preferred_element_type=F32) + bgu_ref[...]
            g = jnp.minimum(gu[:, 0:D_MODEL], SWIGLU_LIMIT)
            lin = jnp.clip(gu[:, D_MODEL:], -SWIGLU_LIMIT, SWIGLU_LIMIT)
            act = g * _sigmoid(SWIGLU_ALPHA * g) * (lin + 1.0)
            y = jnp.dot(act.astype(BF16), wd_bf[...], preferred_element_type=F32) + bd_ref[...]
            ybuf[slot, 0:m] = y * gate

        @pl.when(nval_ref[b] > r // 2)
        def _():
            expert_rows(r)

        @pl.when(nval_ref[b] <= r // 2)
        def _():
            expert_rows(r // 2)

        for_each_run(b, start_scatter(slot))

        @pl.when(b == nb - 1)
        def _():
            wait_scatter(slot, nval_ref[b])

            @pl.when(nb >= 2)
            def _():
                wait_scatter(1 - slot, nval_ref[jnp.maximum(b - 1, 0)])

            def tail_body(i, c):
                tail_copy(i).wait()
                return c
            lax.fori_loop(0, n_tiles, tail_body, 0)


def _expert_call(tables, xs, w_gu, b_gu, w_down, b_down):
    nblk = tables[0].shape[0]
    r = ROW_BLOCK
    d = D_MODEL
    bmap = lambda b, be, *_: (be[b], 0, 0)
    grid_spec = pltpu.PrefetchScalarGridSpec(
        num_scalar_prefetch=len(tables),
        grid=(nblk,),
        in_specs=[
            pl.BlockSpec(memory_space=pl.ANY),
            pl.BlockSpec(memory_space=pl.ANY),
            pl.BlockSpec((None, 1, 2 * d), bmap),
            pl.BlockSpec(memory_space=pl.ANY),
            pl.BlockSpec((None, 1, d), bmap),
        ],
        out_specs=pl.BlockSpec(memory_space=pl.ANY),
        scratch_shapes=[
            pltpu.VMEM((2, r, ROW_W), F32),
            pltpu.VMEM((2, r, d), F32),
            pltpu.VMEM((ROW_ALIGN * N_EXPERTS, d), F32),
            pltpu.VMEM((d, 2 * d), F32),
            pltpu.VMEM((d, d), F32),
            pltpu.VMEM((d, 2 * d), BF16),
            pltpu.VMEM((d, d), BF16),
            pltpu.SemaphoreType.DMA((2,)),
            pltpu.SemaphoreType.DMA((2,)),
            pltpu.SemaphoreType.DMA((1,)),
            pltpu.SemaphoreType.DMA((2,)),
        ],
    )
    return pl.pallas_call(
        _expert_kernel,
        grid_spec=grid_spec,
        out_shape=jax.ShapeDtypeStruct((xs.shape[0], d), F32),
        compiler_params=pltpu.CompilerParams(
            dimension_semantics=("arbitrary",), vmem_limit_bytes=VMEM_LIMIT_BYTES),
        name="experts",
    )(*tables, xs, w_gu, b_gu, w_down, b_down)


def _final_kernel(h1_ref, mod_ref, ys_ref, pos_ref, fg_ref, o_ref):
    t = h1_ref.shape[0]
    g2 = mod_ref[5:6, :]
    pos_rows = [pos_ref[k:k + 1, :] for k in range(TOP_K)]
    perm = jnp.concatenate(_perm_blocks(pos_rows, TILE_ROWS, t), axis=0)
    moe = lax.dot_general(perm, ys_ref[...].astype(BF16), (((0,), (0,)), ((), ())), preferred_element_type=F32)
    o_ref[...] = _rms(h1_ref[...] + g2 * moe, fg_ref[...])


def _final_call(h1, mod3, ys, pos, final_g, bsz, seq):
    d = h1.shape[1]
    n_tok = bsz * seq
    t = SEQ_TILE
    per_batch = seq // t
    out = pl.pallas_call(
        _final_kernel,
        grid=(n_tok // t,),
        in_specs=[
            pl.BlockSpec((t, d), lambda i: (i, 0)),
            pl.BlockSpec((None, 6, d), lambda i: (i // per_batch, 0, 0)),
            pl.BlockSpec((TILE_ROWS, d), lambda i: (i, 0)),
            pl.BlockSpec((TOP_K, t), lambda i: (0, i)),
            pl.BlockSpec((1, d), lambda i: (0, 0)),
        ],
        out_specs=pl.BlockSpec((t, d), lambda i: (i, 0)),
        out_shape=jax.ShapeDtypeStruct((n_tok, d), F32),
        compiler_params=pltpu.CompilerParams(
            dimension_semantics=("arbitrary",), vmem_limit_bytes=VMEM_LIMIT_BYTES),
        name="combine_final",
    )(h1, mod3, ys, pos, final_g.reshape(1, d))
    return out.reshape(bsz, seq, d)


def _block_diag(w):
    h, i, o = w.shape
    eye = jnp.eye(h, dtype=w.dtype)
    return (w[:, :, None, :] * eye[:, None, :, None]).reshape(h * i, h * o)


def _route_tables(cnt):
    r = ROW_BLOCK
    n_tiles = cnt.shape[0]
    nblk = n_tiles * TILE_ROWS // r + N_EXPERTS
    cnt = (cnt + (ROW_ALIGN - 1)) // ROW_ALIGN * ROW_ALIGN
    ie = jnp.arange(N_EXPERTS, dtype=I32)
    it = jnp.arange(n_tiles, dtype=I32)
    e_before = (ie[:, None] < ie[None, :]).astype(I32)
    t_before = (it[:, None] < it[None, :]).astype(I32)
    total = jnp.sum(cnt, axis=0)
    cum = jnp.sum(t_before[:, :, None] * cnt[:, None, :], axis=0)
    seg_off = jnp.sum(cnt[:, :, None] * e_before[None, :, :], axis=1)
    base = seg_off + it[:, None] * TILE_ROWS
    nblk_e = (total + r - 1) // r
    blk_start = jnp.sum(nblk_e[:, None] * e_before, axis=0)
    blk_end = blk_start + nblk_e
    nb_used = jnp.sum(nblk_e)
    blk = jnp.arange(nblk, dtype=I32)
    used = blk < nb_used
    last_e = jnp.max(jnp.where(nblk_e > 0, ie, 0))
    be = jnp.minimum(jnp.sum((blk[:, None] >= blk_end[None, :]).astype(I32), axis=1), N_EXPERTS - 1)
    be = jnp.where(used, be, last_e)
    be_onehot = (be[:, None] == ie[None, :]).astype(I32)
    pick = lambda v: jnp.sum(be_onehot * v[None, :], axis=1)
    jrow = jnp.where(used, (blk - pick(blk_start)) * r, 0)
    nval = jnp.where(used, jnp.clip(pick(total) - jrow, 0, r), 0)
    first = jnp.concatenate([jnp.ones((1,), I32), (be[1:] != be[:-1]).astype(I32)])
    nxt = pick(blk_end)
    next_e = jnp.where(nxt < nb_used, jnp.sum((nxt[:, None] >= blk_end[None, :]).astype(I32), axis=1), -1)
    run_start = jnp.sum(be_onehot[:, None, :] * cum[None, :, :], axis=-1)
    run_end = run_start + jnp.sum(be_onehot[:, None, :] * cnt[None, :, :], axis=-1)
    ilo = jnp.sum((run_end <= jrow[:, None]).astype(I32), axis=1)
    ihi = n_tiles - 1 - jnp.sum((run_start >= (jrow + r)[:, None]).astype(I32), axis=1)
    used_rows = jnp.sum(cnt, axis=1)
    tail_row = jnp.arange(n_tiles, dtype=I32) * TILE_ROWS + used_rows
    tail_len = TILE_ROWS - used_rows
    i32 = lambda v: v.astype(I32)
    return (i32(be), i32(first), i32(nval), i32(jrow), i32(jnp.minimum(ilo, n_tiles - 1)), i32(ihi), i32(next_e),
            i32(nb_used).reshape(1), i32(cnt).reshape(-1), i32(cum).reshape(-1), i32(base).reshape(-1),
            i32(tail_row), i32(tail_len))


def kernel(x, c, ada_w, ada_b, norm1_g, w_in, conv_w, conv_b, lru_wr, lru_br, lru_wi, lru_bi, lru_lambda, sgu_ln_g, sgu_ln_b, sgu_w, sgu_b, gnorm_lru_g, gnorm_sgu_g, w_out, norm2_g, router_w, router_b, exp_w_gu, exp_b_gu, exp_w_down, exp_b_down, final_g):
    bsz, seq, d = x.shape
    depth = ada_w.shape[0]
    assert depth == 1 and d == D_MODEL and seq % SEQ_TILE == 0 and ROW_BLOCK >= ROW_ALIGN * N_EXPERTS
    l = 0
    mod = _adaln_call(c, ada_w[l], ada_b[l])
    mod3 = mod.reshape(bsz, 6, d)

    row = lambda v: v.reshape(1, -1)
    wgate = jnp.concatenate([_block_diag(lru_wr[l]), _block_diag(lru_wi[l])], axis=1).astype(BF16)
    bgate = jnp.concatenate([lru_br[l], lru_bi[l]]).reshape(1, -1)
    sgub_full = jnp.repeat(sgu_b[l].T, SGU_HEAD_DIM, axis=1)
    h1, xs, pos, cnt = _mixer_call(
        x, mod3, row(norm1_g[l]), w_in[l].astype(BF16), conv_w[l], row(conv_b[l]), wgate, bgate,
        row(lru_lambda[l]), row(sgu_ln_g[l]), row(sgu_ln_b[l]), sgu_w[l], sgub_full,
        row(gnorm_lru_g[l]), row(gnorm_sgu_g[l]), w_out[l].astype(BF16), row(norm2_g[l]),
        router_w[l].T, router_b[l].reshape(-1, 1))

    tables = _route_tables(cnt[:, :, 0])
    ys = _expert_call(tables, xs, exp_w_gu[l], exp_b_gu[l].reshape(N_EXPERTS, 1, -1),
                      exp_w_down[l], exp_b_down[l].reshape(N_EXPERTS, 1, -1))
    return _final_call(h1, mod3, ys, pos, final_g, bsz, seq)
```

```python
import functools

import jax
import jax.numpy as jnp
from jax import lax
from jax.experimental import pallas as pl
from jax.experimental.pallas import tpu as pltpu

F32 = jnp.float32
BF16 = jnp.bfloat16
I32 = jnp.int32
U32 = jnp.uint32

D_MODEL = 1024
D_LRU = 512
D_SGU = 512
LRU_HEADS = 8
CONV_WIDTH = 4
LRU_C = 8.0
SGU_HEADS = 8
SGU_HEAD_DIM = D_SGU // SGU_HEADS
CHUNK = 128
N_EXPERTS = 32
TOP_K = 4
SWIGLU_ALPHA = 1.702
SWIGLU_LIMIT = 7.0
EPS = 1e-6

LANES = 128
SEQ_TILE = 512
ROW_ALIGN = 8
TILE_ROWS = TOP_K * SEQ_TILE + ROW_ALIGN * N_EXPERTS
PAYLOAD = LANES
ROW_W = D_MODEL + PAYLOAD
ROW_BLOCK = 512
VMEM_LIMIT_BYTES = 56 * 1024 * 1024


def _sigmoid(x):
    return 1.0 / (1.0 + jnp.exp(-x))


def _gelu_tanh(x):
    return 0.5 * x * (1.0 + jnp.tanh(0.7978845608028654 * (x + 0.044715 * (x * x * x))))


def _rms(x, g):
    ms = jnp.mean(x * x, axis=-1, keepdims=True)
    return x * lax.rsqrt(ms + EPS) * g


def _pack_bf16_pairs(v):
    n = v.shape[1] // 2
    r = v.astype(BF16).astype(F32)
    bits = lax.bitcast_convert_type(r, U32)
    return (bits[:, 0:n] >> 16) | (bits[:, n:] & jnp.uint32(0xFFFF0000))


def _unpack_bf16_pairs(w):
    lo = lax.bitcast_convert_type(w << 16, F32).astype(BF16)
    hi = lax.bitcast_convert_type(w & jnp.uint32(0xFFFF0000), F32).astype(BF16)
    return lo, hi


PERM_BLOCK = 256


def _perm_blocks(pos_rows, n_rows, n_cols):
    blk = PERM_BLOCK
    rid = lax.broadcasted_iota(I32, (blk, n_cols), 0).astype(F32).astype(BF16)
    one = jnp.ones((), BF16)
    blocks = []
    for q in range(n_rows // blk):
        acc = jnp.zeros((blk, n_cols), BF16)
        for pk in pos_rows:
            in_blk = jnp.where(pk // blk == q, pk % blk, -1).astype(F32).astype(BF16)
            acc = jnp.where(rid == in_blk, one, acc)
        blocks.append(acc)
    return blocks


def _adaln_kernel(c_ref, w_ref, b_ref, o_ref):
    c = c_ref[...]
    ca = c * _sigmoid(c)
    o_ref[...] = jnp.dot(ca.astype(BF16), w_ref[...].astype(BF16), preferred_element_type=F32) + b_ref[...]


def _adaln_call(c, w, b):
    bsz, d = c.shape
    n_out = w.shape[1]
    return pl.pallas_call(
        _adaln_kernel,
        grid=(n_out // d,),
        in_specs=[
            pl.BlockSpec((bsz, d), lambda j: (0, 0)),
            pl.BlockSpec((d, d), lambda j: (0, j)),
            pl.BlockSpec((1, d), lambda j: (0, j)),
        ],
        out_specs=pl.BlockSpec((bsz, d), lambda j: (0, j)),
        out_shape=jax.ShapeDtypeStruct((bsz, n_out), F32),
        name="adaln",
    )(c, w, b.reshape(1, n_out))


SCAN_SEGMENTS = 8
SCAN_PITCH_PAD = 8


def _scan_pitch(t):
    return t // SCAN_SEGMENTS + SCAN_PITCH_PAD


def _linear_scan(a, b, h0, a_buf, b_buf):
    t, c = a.shape
    nseg = SCAN_SEGMENTS
    seg = t // nseg
    pitch = _scan_pitch(t)
    nlb = c // LANES
    for j in range(nlb):
        for s in range(nseg):
            a_buf[j, s * pitch:s * pitch + seg, :] = a[s * seg:(s + 1) * seg, j * LANES:(j + 1) * LANES]
            b_buf[j, s * pitch:s * pitch + seg, :] = b[s * seg:(s + 1) * seg, j * LANES:(j + 1) * LANES]

    def step(g, carry):
        hs, ps = carry
        new_h, new_p = [], []
        for j in range(nlb):
            view = (j, pl.ds(g, nseg, stride=pitch), slice(None))
            ag = a_buf[view]
            hj = ag * hs[j] + b_buf[view]
            pj = ag * ps[j]
            b_buf[view] = hj
            a_buf[view] = pj
            new_h.append(hj)
            new_p.append(pj)
        return tuple(new_h), tuple(new_p)

    init = (tuple(jnp.zeros((nseg, LANES), F32) for _ in range(nlb)),
            tuple(jnp.ones((nseg, LANES), F32) for _ in range(nlb)))
    h_end, p_end = lax.fori_loop(0, seg, step, init, unroll=True)
    h_end = jnp.concatenate(h_end, axis=1)
    p_end = jnp.concatenate(p_end, axis=1)
    state = h0
    out = []
    for s in range(nseg):
        h_loc = jnp.concatenate([b_buf[j, s * pitch:s * pitch + seg, :] for j in range(nlb)], axis=1)
        p_loc = jnp.concatenate([a_buf[j, s * pitch:s * pitch + seg, :] for j in range(nlb)], axis=1)
        out.append(h_loc + p_loc * state)
        state = h_end[s:s + 1] + p_end[s:s + 1] * state
    return jnp.concatenate(out, axis=0), state


def _mixer_kernel(x_ref, mod_ref, modp_ref, n1g_ref, win_ref, convw_ref, convb_ref, wgate_ref, bgate_ref, lam_ref,
                  lng_ref, lnb_ref, sguw_ref, sgub_ref, gl_ref, gs_ref, wout_ref, n2g_ref, rwt_ref, rb_ref,
                  h1_ref, xs_ref, pos_ref, cnt_ref,
                  xa_tail, h_carry, scan_a, scan_b, h1_prev, *, tiles_per_seq):
    t = x_ref.shape[0]
    i = pl.program_id(0)

    @pl.when(i == 0)
    def _():
        h1_prev[...] = jnp.zeros_like(h1_prev)

    @pl.when(i % tiles_per_seq == 0)
    def _():
        xa_tail[...] = jnp.zeros_like(xa_tail)
        h_carry[...] = jnp.zeros_like(h_carry)

    z2b, logits = _router_logits(h1_prev[...], modp_ref[...], n2g_ref, rwt_ref, rb_ref)
    pos_rows, gate_pieces = _route(logits, pos_ref, cnt_ref)

    mod = mod_ref[...]
    sh1, sc1, g1 = mod[0:1], mod[1:2], mod[2:3]

    x = x_ref[...]
    z = _rms(x, n1g_ref[...] * (1.0 + sc1)) + sh1
    proj = jnp.dot(z.astype(BF16), win_ref[...], preferred_element_type=F32)
    perm_blocks = _perm_blocks(pos_rows, TILE_ROWS, t)
    n_pb = len(perm_blocks)
    sort_rows = functools.partial(_sort_rows, perm_blocks, z2b, gate_pieces, xs_ref)
    sort_rows(0, n_pb // 3)
    xa = proj[:, 0:D_LRU]
    ya = proj[:, D_LRU:2 * D_LRU]
    u = proj[:, 2 * D_LRU:2 * D_LRU + D_SGU]
    v = proj[:, 2 * D_LRU + D_SGU:]

    tail = xa_tail[...]
    row8 = lax.broadcasted_iota(I32, (8, 1), 0)
    xc = xa * convw_ref[CONV_WIDTH - 1:CONV_WIDTH, :] + convb_ref[...]
    for sft in range(1, CONV_WIDTH):
        rolled = pltpu.roll(xa, sft, 0)
        head = jnp.where(row8 < sft, pltpu.roll(tail, sft, 0), rolled[0:8])
        shifted = jnp.concatenate([head, rolled[8:]], axis=0)
        xc = xc + shifted * convw_ref[CONV_WIDTH - 1 - sft:CONV_WIDTH - sft, :]
    xa_tail[...] = xa[t - 8:t]

    gates = jnp.dot(xc.astype(BF16), wgate_ref[...], preferred_element_type=F32) + bgate_ref[...]
    sort_rows(n_pb // 3, 2 * n_pb // 3)
    r_gate = _sigmoid(gates[:, 0:D_LRU])
    i_gate = _sigmoid(gates[:, D_LRU:])
    nlam = -lam_ref[...]
    softplus = jnp.maximum(nlam, 0.0) + jnp.log1p(jnp.exp(-jnp.abs(nlam)))
    log_a = (-LRU_C) * r_gate * softplus
    a = jnp.exp(log_a)
    om = -jnp.tanh(log_a) * (a * a + 1.0)
    mult = jnp.where(om > 0.0, om * lax.rsqrt(om), 0.0)
    bterm = mult * i_gate * xc
    sort_rows(2 * n_pb // 3, n_pb)
    h, h_last = _linear_scan(a, bterm, h_carry[...], scan_a, scan_b)
    h_carry[...] = h_last
    o_lru = _rms(h * _gelu_tanh(ya), gl_ref[...])

    ug = _gelu_tanh(u)
    vg = _gelu_tanh(v)
    mu = jnp.mean(vg, axis=-1, keepdims=True)
    vcen = vg - mu
    var = jnp.mean(vcen * vcen, axis=-1, keepdims=True)
    vn = (vcen * lax.rsqrt(var + EPS) * lng_ref[...] + lnb_ref[...]).astype(BF16)
    ri = lax.broadcasted_iota(I32, (CHUNK, CHUNK), 0)
    ci = lax.broadcasted_iota(I32, (CHUNK, CHUNK), 1)
    causal = ri >= ci
    lane = lax.broadcasted_iota(I32, (1, 2 * SGU_HEAD_DIM), 1)
    first_half = lane < SGU_HEAD_DIM
    pair_w = []
    for p in range(SGU_HEADS // 2):
        w0 = jnp.where(causal, sguw_ref[2 * p], 0.0).astype(BF16)
        w1 = jnp.where(causal, sguw_ref[2 * p + 1], 0.0).astype(BF16)
        pair_w.append(jnp.concatenate([w0, w1], axis=1))
    chunks = []
    zero = jnp.zeros((), BF16)
    for n in range(t // CHUNK):
        cols = []
        for p in range(SGU_HEADS // 2):
            blk = vn[n * CHUNK:(n + 1) * CHUNK, p * LANES:(p + 1) * LANES]
            rhs = jnp.concatenate([jnp.where(first_half, blk, zero), jnp.where(first_half, zero, blk)], axis=0)
            cols.append(jnp.dot(pair_w[p], rhs, preferred_element_type=F32))
        chunks.append(jnp.concatenate(cols, axis=1) + sgub_ref[...])
    mixed = jnp.concatenate(chunks, axis=0)
    o_sgu = _rms(ug * mixed, gs_ref[...])

    heads = jnp.concatenate([o_lru, o_sgu], axis=1).astype(BF16)
    h1 = x + g1 * jnp.dot(heads, wout_ref[...], preferred_element_type=F32)
    h1_ref[...] = h1
    h1_prev[...] = h1


def _router_logits(h1, mod, n2g_ref, rwt_ref, rb_ref):
    sh2, sc2 = mod[3:4], mod[4:5]
    z2b = (_rms(h1, n2g_ref[...] * (1.0 + sc2)) + sh2).astype(BF16)
    logits = lax.dot_general(rwt_ref[...].astype(BF16), z2b, (((1,), (1,)), ((), ())),
                             preferred_element_type=F32) + rb_ref[...]
    return z2b, logits


def _route(logits, pos_ref, cnt_ref):
    t = logits.shape[1]
    eidx = lax.broadcasted_iota(I32, (N_EXPERTS, t), 0)
    work = logits
    sel = []
    tops = []
    for k in range(TOP_K):
        m = jnp.max(work, axis=0, keepdims=True)
        idx = jnp.min(jnp.where(work == m, eidx, N_EXPERTS), axis=0, keepdims=True)
        onehot = eidx == idx
        sel.append(onehot)
        tops.append(m)
        work = jnp.where(onehot, -jnp.inf, work)
    exps = [jnp.exp(tk - tops[0]) for tk in tops]
    denom = exps[0] + exps[1] + exps[2] + exps[3]
    chosen = jnp.zeros((N_EXPERTS, t), F32)
    gsel = jnp.zeros((N_EXPERTS, t), F32)
    for k in range(TOP_K):
        chosen = jnp.where(sel[k], 1.0, chosen)
        gsel = jnp.where(sel[k], exps[k] / denom, gsel)

    chosen_b = chosen.astype(BF16)
    si = lax.broadcasted_iota(I32, (t, t), 0)
    ti = lax.broadcasted_iota(I32, (t, t), 1)
    before = jnp.where(si < ti, 1.0, 0.0).astype(BF16)
    excl = jnp.dot(chosen_b, before, preferred_element_type=F32)
    cnt_col = jnp.sum(chosen, axis=1, keepdims=True).astype(I32)
    cnt_ref[...] = jnp.broadcast_to(cnt_col, cnt_ref.shape)
    run_len = ((cnt_col + (ROW_ALIGN - 1)) // ROW_ALIGN * ROW_ALIGN).astype(F32)
    er = lax.broadcasted_iota(I32, (N_EXPERTS, N_EXPERTS), 0)
    ec = lax.broadcasted_iota(I32, (N_EXPERTS, N_EXPERTS), 1)
    lower = jnp.where(ec < er, 1.0, 0.0).astype(BF16)
    run_start = jnp.dot(lower, jnp.broadcast_to(run_len, (N_EXPERTS, LANES)).astype(BF16),
                        preferred_element_type=F32)[:, 0:1]
    posmat = excl + run_start
    pos_rows = []
    for k in range(TOP_K):
        pk = jnp.sum(jnp.where(sel[k], posmat, 0.0), axis=0, keepdims=True).astype(I32)
        pos_ref[k:k + 1, :] = pk
        pos_rows.append(pk)

    g_hi = gsel.astype(BF16)
    rem = gsel - g_hi.astype(F32)
    g_mid = rem.astype(BF16)
    g_lo = (rem - g_mid.astype(F32)).astype(BF16)
    gp = jnp.concatenate([g_hi, g_mid, g_lo, jnp.zeros((PAYLOAD - 3 * N_EXPERTS, t), BF16)], axis=0)
    return pos_rows, gp


def _sort_rows(perm_blocks, z2b, gate_pieces, xs_ref, q0, q1):
    for q in range(q0, q1):
        rows = slice(q * PERM_BLOCK, (q + 1) * PERM_BLOCK)
        xs_ref[rows, 0:D_MODEL] = jnp.dot(perm_blocks[q], z2b, preferred_element_type=F32)
        xs_ref[rows, D_MODEL:] = lax.dot_general(perm_blocks[q], gate_pieces, (((1,), (1,)), ((), ())),
                                                 preferred_element_type=F32)


def _mixer_call(x, mod3, n1g, win, convw, convb, wgate, bgate, lam, lng, lnb, sguw, sgub_full, gl, gs, wout,
                n2g, rwt, rb):
    bsz, seq, d = x.shape
    t = SEQ_TILE
    tiles = seq // t
    n_tiles = bsz * tiles
    x2 = x.reshape(bsz * seq, d)

    def const(shape):
        return pl.BlockSpec(shape, lambda i: (0,) * len(shape))

    mixed = lambda i: jnp.minimum(i, n_tiles - 1)
    routed = lambda i: jnp.maximum(i - 1, 0)
    return pl.pallas_call(
        functools.partial(_mixer_kernel, tiles_per_seq=tiles),
        grid=(n_tiles + 1,),
        in_specs=[
            pl.BlockSpec((t, d), lambda i: (mixed(i), 0)),
            pl.BlockSpec((None, 6, d), lambda i: (mixed(i) // tiles, 0, 0)),
            pl.BlockSpec((None, 6, d), lambda i: (routed(i) // tiles, 0, 0)),
            const((1, d)),
            const((d, 2 * d)),
            const((CONV_WIDTH, D_LRU)),
            const((1, D_LRU)),
            const((D_LRU, 2 * D_LRU)),
            const((1, 2 * D_LRU)),
            const((1, D_LRU)),
            const((1, D_SGU)),
            const((1, D_SGU)),
            const((SGU_HEADS, CHUNK, CHUNK)),
            const((CHUNK, D_SGU)),
            const((1, D_LRU)),
            const((1, D_SGU)),
            const((d, d)),
            const((1, d)),
            const((N_EXPERTS, d)),
            const((N_EXPERTS, 1)),
        ],
        out_specs=[
            pl.BlockSpec((t, d), lambda i: (i, 0)),
            pl.BlockSpec((TILE_ROWS, ROW_W), lambda i: (routed(i), 0)),
            pl.BlockSpec((TOP_K, t), lambda i: (0, routed(i))),
            pl.BlockSpec((None, N_EXPERTS, LANES), lambda i: (routed(i), 0, 0)),
        ],
        out_shape=[
            jax.ShapeDtypeStruct(((n_tiles + 1) * t, d), F32),
            jax.ShapeDtypeStruct((n_tiles * TILE_ROWS, ROW_W), F32),
            jax.ShapeDtypeStruct((TOP_K, n_tiles * t), I32),
            jax.ShapeDtypeStruct((n_tiles, N_EXPERTS, LANES), I32),
        ],
        scratch_shapes=[pltpu.VMEM((8, D_LRU), F32), pltpu.VMEM((1, D_LRU), F32),
                        pltpu.VMEM((D_LRU // LANES, SCAN_SEGMENTS * _scan_pitch(t), LANES), F32),
                        pltpu.VMEM((D_LRU // LANES, SCAN_SEGMENTS * _scan_pitch(t), LANES), F32),
                        pltpu.VMEM((t, d), F32)],
        compiler_params=pltpu.CompilerParams(
            dimension_semantics=("arbitrary",), vmem_limit_bytes=VMEM_LIMIT_BYTES),
        name="mixer_router",
    )(x2, mod3, mod3, n1g, win, convw, convb, wgate, bgate, lam, lng, lnb, sguw, sgub_full, gl, gs, wout, n2g, rwt,
      rb)


def _expert_kernel(be_ref, first_ref, nval_ref, jrow_ref, ilo_ref, ihi_ref, nexte_ref, nb_ref,
                   cnt_ref, cum_ref, base_ref, tailrow_ref, taillen_ref,
                   xs_hbm, wgu_hbm, bgu_ref, wd_hbm, bd_ref,
                   ys_hbm,
                   xbuf, ybuf, zbuf, wgu_st, wd_st, wgu_bf, wd_bf, gsem, ssem, zsem, wsem):
    r = xbuf.shape[1]
    n_tiles = tailrow_ref.shape[0]
    b = pl.program_id(0)
    nb = nb_ref[0]
    slot = b & 1

    rows = lambda v: pl.multiple_of(v, ROW_ALIGN)

    def weight_copies(e):
        return (pltpu.make_async_copy(wgu_hbm.at[e], wgu_st, wsem.at[0]),
                pltpu.make_async_copy(wd_hbm.at[e], wd_st, wsem.at[1]))

    def tail_copy(i):
        n = rows(taillen_ref[i])
        return pltpu.make_async_copy(zbuf.at[pl.ds(0, n), :], ys_hbm.at[pl.ds(rows(tailrow_ref[i]), n), :],
                                     zsem.at[0])

    def for_each_run(blk, fn):
        e = be_ref[blk]
        j0 = jrow_ref[blk]

        def tile_body(i, c):
            idx = i * N_EXPERTS + e
            cu = cum_ref[idx]
            lo = jnp.maximum(cu, j0)
            hi = jnp.minimum(cu + cnt_ref[idx], j0 + r)

            @pl.when(hi > lo)
            def _():
                fn(rows(base_ref[idx] + (lo - cu)), rows(lo - j0), rows(hi - lo))
            return c
        lax.fori_loop(ilo_ref[blk], ihi_ref[blk] + 1, tile_body, 0)

    def start_gather(dst_slot):
        def fn(hbm_row, buf_row, n):
            pltpu.make_async_copy(xs_hbm.at[pl.ds(hbm_row, n), :],
                                  xbuf.at[dst_slot, pl.ds(buf_row, n), :], gsem.at[dst_slot]).start()
        return fn

    def start_scatter(src_slot):
        def fn(hbm_row, buf_row, n):
            pltpu.make_async_copy(ybuf.at[src_slot, pl.ds(buf_row, n), :],
                                  ys_hbm.at[pl.ds(hbm_row, n), :], ssem.at[src_slot]).start()
        return fn

    def wait_gather(s, n):
        pltpu.make_async_copy(xs_hbm.at[pl.ds(0, rows(n)), :], xbuf.at[s, pl.ds(0, rows(n)), :], gsem.at[s]).wait()

    def wait_scatter(s, n):
        pltpu.make_async_copy(ybuf.at[s, pl.ds(0, rows(n)), :], ys_hbm.at[pl.ds(0, rows(n)), :], ssem.at[s]).wait()

    @pl.when(b == 0)
    def _():
        for c in weight_copies(be_ref[0]):
            c.start()
        xbuf[...] = jnp.zeros_like(xbuf)
        for_each_run(0, start_gather(0))
        zbuf[...] = jnp.zeros_like(zbuf)

        def tail_body(i, c):
            tail_copy(i).start()
            return c
        lax.fori_loop(0, n_tiles, tail_body, 0)

    @pl.when(b < nb)
    def _():
        wait_gather(slot, nval_ref[b])

        @pl.when(b + 1 < nb)
        def _():
            for_each_run(b + 1, start_gather(1 - slot))

        @pl.when(first_ref[b] == 1)
        def _():
            for c in weight_copies(be_ref[b]):
                c.wait()
            wgu_bf[...] = wgu_st[...].astype(BF16)
            wd_bf[...] = wd_st[...].astype(BF16)

            @pl.when(nexte_ref[b] >= 0)
            def _():
                for c in weight_copies(nexte_ref[b]):
                    c.start()

        @pl.when(b >= 2)
        def _():
            wait_scatter(slot, nval_ref[jnp.maximum(b - 2, 0)])

        def expert_rows(m):
            xw = xbuf[slot, 0:m]
            xb = xw[:, 0:D_MODEL].astype(BF16)
            lane = lax.broadcasted_iota(I32, (1, PAYLOAD), 1)
            gate = jnp.sum(jnp.where((lane & (N_EXPERTS - 1)) == be_ref[b], xw[:, D_MODEL:], 0.0),
                           axis=1, keepdims=True)
            gu = jnp.dot(xb, wgu_bf[...], preferred_element_type=F32) + bgu_ref[...]
            g = jnp.minimum(gu[:, 0:D_MODEL], SWIGLU_LIMIT)
            lin = jnp.clip(gu[:, D_MODEL:], -SWIGLU_LIMIT, SWIGLU_LIMIT)
            act = g * _sigmoid(SWIGLU_ALPHA * g) * (lin + 1.0)
            y = jnp.dot(act.astype(BF16), wd_bf[...], preferred_element_type=F32) + bd_ref[...]
            ybuf[slot, 0:m] = _pack_bf16_pairs(y * gate)

        @pl.when(nval_ref[b] > r // 2)
        def _():
            expert_rows(r)

        @pl.when(nval_ref[b] <= r // 2)
        def _():
            expert_rows(r // 2)

        for_each_run(b, start_scatter(slot))

        @pl.when(b == nb - 1)
        def _():
            wait_scatter(slot, nval_ref[b])

            @pl.when(nb >= 2)
            def _():
                wait_scatter(1 - slot, nval_ref[jnp.maximum(b - 1, 0)])

            def tail_body(i, c):
                tail_copy(i).wait()
                return c
            lax.fori_loop(0, n_tiles, tail_body, 0)


def _expert_call(tables, xs, w_gu, b_gu, w_down, b_down):
    nblk = tables[0].shape[0]
    r = ROW_BLOCK
    d = D_MODEL
    bmap = lambda b, be, *_: (be[b], 0, 0)
    grid_spec = pltpu.PrefetchScalarGridSpec(
        num_scalar_prefetch=len(tables),
        grid=(nblk,),
        in_specs=[
            pl.BlockSpec(memory_space=pl.ANY),
            pl.BlockSpec(memory_space=pl.ANY),
            pl.BlockSpec((None, 1, 2 * d), bmap),
            pl.BlockSpec(memory_space=pl.ANY),
            pl.BlockSpec((None, 1, d), bmap),
        ],
        out_specs=pl.BlockSpec(memory_space=pl.ANY),
        scratch_shapes=[
            pltpu.VMEM((2, r, ROW_W), F32),
            pltpu.VMEM((2, r, d // 2), U32),
            pltpu.VMEM((ROW_ALIGN * N_EXPERTS, d // 2), U32),
            pltpu.VMEM((d, 2 * d), F32),
            pltpu.VMEM((d, d), F32),
            pltpu.VMEM((d, 2 * d), BF16),
            pltpu.VMEM((d, d), BF16),
            pltpu.SemaphoreType.DMA((2,)),
            pltpu.SemaphoreType.DMA((2,)),
            pltpu.SemaphoreType.DMA((1,)),
            pltpu.SemaphoreType.DMA((2,)),
        ],
    )
    return pl.pallas_call(
        _expert_kernel,
        grid_spec=grid_spec,
        out_shape=jax.ShapeDtypeStruct((xs.shape[0], d // 2), U32),
        compiler_params=pltpu.CompilerParams(
            dimension_semantics=("arbitrary",), vmem_limit_bytes=VMEM_LIMIT_BYTES),
        name="experts",
    )(*tables, xs, w_gu, b_gu, w_down, b_down)


def _final_kernel(h1_ref, mod_ref, ys_ref, pos_ref, fg_ref, o_ref):
    t = h1_ref.shape[0]
    g2 = mod_ref[5:6, :]
    pos_rows = [pos_ref[k:k + 1, :] for k in range(TOP_K)]
    perm = jnp.concatenate(_perm_blocks(pos_rows, TILE_ROWS, t), axis=0)
    tn = (((0,), (0,)), ((), ()))
    y_lo, y_hi = _unpack_bf16_pairs(ys_ref[...])
    moe = jnp.concatenate([lax.dot_general(perm, y_lo, tn, preferred_element_type=F32),
                           lax.dot_general(perm, y_hi, tn, preferred_element_type=F32)], axis=1)
    o_ref[...] = _rms(h1_ref[...] + g2 * moe, fg_ref[...])


def _final_call(h1, mod3, ys, pos, final_g, bsz, seq):
    d = h1.shape[1]
    n_tok = bsz * seq
    t = SEQ_TILE
    per_batch = seq // t
    out = pl.pallas_call(
        _final_kernel,
        grid=(n_tok // t,),
        in_specs=[
            pl.BlockSpec((t, d), lambda i: (i, 0)),
            pl.BlockSpec((None, 6, d), lambda i: (i // per_batch, 0, 0)),
            pl.BlockSpec((TILE_ROWS, d // 2), lambda i: (i, 0)),
            pl.BlockSpec((TOP_K, t), lambda i: (0, i)),
            pl.BlockSpec((1, d), lambda i: (0, 0)),
        ],
        out_specs=pl.BlockSpec((t, d), lambda i: (i, 0)),
        out_shape=jax.ShapeDtypeStruct((n_tok, d), F32),
        compiler_params=pltpu.CompilerParams(
            dimension_semantics=("arbitrary",), vmem_limit_bytes=VMEM_LIMIT_BYTES),
        name="combine_final",
    )(h1, mod3, ys, pos, final_g.reshape(1, d))
    return out.reshape(bsz, seq, d)


def _block_diag(w):
    h, i, o = w.shape
    eye = jnp.eye(h, dtype=w.dtype)
    return (w[:, :, None, :] * eye[:, None, :, None]).reshape(h * i, h * o)


def _route_tables(cnt):
    r = ROW_BLOCK
    n_tiles = cnt.shape[0]
    nblk = n_tiles * TILE_ROWS // r + N_EXPERTS
    cnt = (cnt + (ROW_ALIGN - 1)) // ROW_ALIGN * ROW_ALIGN
    ie = jnp.arange(N_EXPERTS, dtype=I32)
    it = jnp.arange(n_tiles, dtype=I32)
    e_before = (ie[:, None] < ie[None, :]).astype(I32)
    t_before = (it[:, None] < it[None, :]).astype(I32)
    total = jnp.sum(cnt, axis=0)
    cum = jnp.sum(t_before[:, :, None] * cnt[:, None, :], axis=0)
    seg_off = jnp.sum(cnt[:, :, None] * e_before[None, :, :], axis=1)
    base = seg_off + it[:, None] * TILE_ROWS
    nblk_e = (total + r - 1) // r
    blk_start = jnp.sum(nblk_e[:, None] * e_before, axis=0)
    blk_end = blk_start + nblk_e
    nb_used = jnp.sum(nblk_e)
    blk = jnp.arange(nblk, dtype=I32)
    used = blk < nb_used
    last_e = jnp.max(jnp.where(nblk_e > 0, ie, 0))
    be = jnp.minimum(jnp.sum((blk[:, None] >= blk_end[None, :]).astype(I32), axis=1), N_EXPERTS - 1)
    be = jnp.where(used, be, last_e)
    be_onehot = (be[:, None] == ie[None, :]).astype(I32)
    pick = lambda v: jnp.sum(be_onehot * v[None, :], axis=1)
    jrow = jnp.where(used, (blk - pick(blk_start)) * r, 0)
    nval = jnp.where(used, jnp.clip(pick(total) - jrow, 0, r), 0)
    first = jnp.concatenate([jnp.ones((1,), I32), (be[1:] != be[:-1]).astype(I32)])
    nxt = pick(blk_end)
    next_e = jnp.where(nxt < nb_used, jnp.sum((nxt[:, None] >= blk_end[None, :]).astype(I32), axis=1), -1)
    run_start = jnp.sum(be_onehot[:, None, :] * cum[None, :, :], axis=-1)
    run_end = run_start + jnp.sum(be_onehot[:, None, :] * cnt[None, :, :], axis=-1)
    ilo = jnp.sum((run_end <= jrow[:, None]).astype(I32), axis=1)
    ihi = n_tiles - 1 - jnp.sum((run_start >= (jrow + r)[:, None]).astype(I32), axis=1)
    used_rows = jnp.sum(cnt, axis=1)
    tail_row = jnp.arange(n_tiles, dtype=I32) * TILE_ROWS + used_rows
    tail_len = TILE_ROWS - used_rows
    i32 = lambda v: v.astype(I32)
    return (i32(be), i32(first), i32(nval), i32(jrow), i32(jnp.minimum(ilo, n_tiles - 1)), i32(ihi), i32(next_e),
            i32(nb_used).reshape(1), i32(cnt).reshape(-1), i32(cum).reshape(-1), i32(base).reshape(-1),
            i32(tail_row), i32(tail_len))


def kernel(x, c, ada_w, ada_b, norm1_g, w_in, conv_w, conv_b, lru_wr, lru_br, lru_wi, lru_bi, lru_lambda, sgu_ln_g, sgu_ln_b, sgu_w, sgu_b, gnorm_lru_g, gnorm_sgu_g, w_out, norm2_g, router_w, router_b, exp_w_gu, exp_b_gu, exp_w_down, exp_b_down, final_g):
    bsz, seq, d = x.shape
    depth = ada_w.shape[0]
    assert depth == 1 and d == D_MODEL and seq % SEQ_TILE == 0 and ROW_BLOCK >= ROW_ALIGN * N_EXPERTS
    l = 0
    mod = _adaln_call(c, ada_w[l], ada_b[l])
    mod3 = mod.reshape(bsz, 6, d)

    row = lambda v: v.reshape(1, -1)
    wgate = jnp.concatenate([_block_diag(lru_wr[l]), _block_diag(lru_wi[l])], axis=1).astype(BF16)
    bgate = jnp.concatenate([lru_br[l], lru_bi[l]]).reshape(1, -1)
    sgub_full = jnp.repeat(sgu_b[l].T, SGU_HEAD_DIM, axis=1)
    h1, xs, pos, cnt = _mixer_call(
        x, mod3, row(norm1_g[l]), w_in[l].astype(BF16), conv_w[l], row(conv_b[l]), wgate, bgate,
        row(lru_lambda[l]), row(sgu_ln_g[l]), row(sgu_ln_b[l]), sgu_w[l], sgub_full,
        row(gnorm_lru_g[l]), row(gnorm_sgu_g[l]), w_out[l].astype(BF16), row(norm2_g[l]),
        router_w[l].T, router_b[l].reshape(-1, 1))

    tables = _route_tables(cnt[:, :, 0])
    ys = _expert_call(tables, xs, exp_w_gu[l], exp_b_gu[l].reshape(N_EXPERTS, 1, -1),
                      exp_w_down[l], exp_b_down[l].reshape(N_EXPERTS, 1, -1))
    return _final_call(h1, mod3, ys, pos, final_g, bsz, seq)
```

```python
import functools

import jax
import jax.numpy as jnp
from jax import lax
from jax.experimental import pallas as pl
from jax.experimental.pallas import tpu as pltpu

F32 = jnp.float32
BF16 = jnp.bfloat16
I32 = jnp.int32
U32 = jnp.uint32

D_MODEL = 1024
D_LRU = 512
D_SGU = 512
LRU_HEADS = 8
CONV_WIDTH = 4
LRU_C = 8.0
SGU_HEADS = 8
SGU_HEAD_DIM = D_SGU // SGU_HEADS
CHUNK = 128
N_EXPERTS = 32
TOP_K = 4
SWIGLU_ALPHA = 1.702
SWIGLU_LIMIT = 7.0
EPS = 1e-6

LANES = 128
SEQ_TILE = 512
ROW_ALIGN = 8
TILE_ROWS = TOP_K * SEQ_TILE + ROW_ALIGN * N_EXPERTS
PAYLOAD = LANES
ROW_W = D_MODEL + PAYLOAD
ROW_BLOCK = 512
VMEM_LIMIT_BYTES = 56 * 1024 * 1024


def _sigmoid(x):
    return 1.0 / (1.0 + jnp.exp(-x))


def _gelu_tanh(x):
    return 0.5 * x * (1.0 + jnp.tanh(0.7978845608028654 * (x + 0.044715 * (x * x * x))))


def _rms(x, g):
    ms = jnp.mean(x * x, axis=-1, keepdims=True)
    return x * lax.rsqrt(ms + EPS) * g


def _pack_bf16_pairs(v):
    n = v.shape[1] // 2
    r = v.astype(BF16).astype(F32)
    bits = lax.bitcast_convert_type(r, U32)
    return (bits[:, 0:n] >> 16) | (bits[:, n:] & jnp.uint32(0xFFFF0000))


def _unpack_bf16_pairs(w):
    lo = lax.bitcast_convert_type(w << 16, F32).astype(BF16)
    hi = lax.bitcast_convert_type(w & jnp.uint32(0xFFFF0000), F32).astype(BF16)
    return lo, hi


PERM_BLOCK = 256


def _perm_blocks(pos_rows, n_rows, n_cols):
    blk = PERM_BLOCK
    rid = lax.broadcasted_iota(I32, (blk, n_cols), 0).astype(F32).astype(BF16)
    one = jnp.ones((), BF16)
    blocks = []
    for q in range(n_rows // blk):
        acc = jnp.zeros((blk, n_cols), BF16)
        for pk in pos_rows:
            in_blk = jnp.where(pk // blk == q, pk % blk, -1).astype(F32).astype(BF16)
            acc = jnp.where(rid == in_blk, one, acc)
        blocks.append(acc)
    return blocks


def _adaln_kernel(c_ref, w_ref, b_ref, o_ref):
    c = c_ref[...]
    ca = c * _sigmoid(c)
    o_ref[...] = jnp.dot(ca.astype(BF16), w_ref[...].astype(BF16), preferred_element_type=F32) + b_ref[...]


def _adaln_call(c, w, b):
    bsz, d = c.shape
    n_out = w.shape[1]
    return pl.pallas_call(
        _adaln_kernel,
        grid=(n_out // d,),
        in_specs=[
            pl.BlockSpec((bsz, d), lambda j: (0, 0)),
            pl.BlockSpec((d, d), lambda j: (0, j)),
            pl.BlockSpec((1, d), lambda j: (0, j)),
        ],
        out_specs=pl.BlockSpec((bsz, d), lambda j: (0, j)),
        out_shape=jax.ShapeDtypeStruct((bsz, n_out), F32),
        name="adaln",
    )(c, w, b.reshape(1, n_out))


SCAN_SEGMENTS = 8
SCAN_PITCH_PAD = 8


def _scan_pitch(t):
    return t // SCAN_SEGMENTS + SCAN_PITCH_PAD


def _linear_scan(a, b, h0, a_buf, b_buf):
    t, c = a.shape
    nseg = SCAN_SEGMENTS
    seg = t // nseg
    pitch = _scan_pitch(t)
    nlb = c // LANES
    for j in range(nlb):
        for s in range(nseg):
            a_buf[j, s * pitch:s * pitch + seg, :] = a[s * seg:(s + 1) * seg, j * LANES:(j + 1) * LANES]
            b_buf[j, s * pitch:s * pitch + seg, :] = b[s * seg:(s + 1) * seg, j * LANES:(j + 1) * LANES]

    def step(g, carry):
        hs, ps = carry
        new_h, new_p = [], []
        for j in range(nlb):
            view = (j, pl.ds(g, nseg, stride=pitch), slice(None))
            ag = a_buf[view]
            hj = ag * hs[j] + b_buf[view]
            pj = ag * ps[j]
            b_buf[view] = hj
            a_buf[view] = pj
            new_h.append(hj)
            new_p.append(pj)
        return tuple(new_h), tuple(new_p)

    init = (tuple(jnp.zeros((nseg, LANES), F32) for _ in range(nlb)),
            tuple(jnp.ones((nseg, LANES), F32) for _ in range(nlb)))
    h_end, p_end = lax.fori_loop(0, seg, step, init, unroll=True)
    h_end = jnp.concatenate(h_end, axis=1)
    p_end = jnp.concatenate(p_end, axis=1)
    state = h0
    out = []
    for s in range(nseg):
        h_loc = jnp.concatenate([b_buf[j, s * pitch:s * pitch + seg, :] for j in range(nlb)], axis=1)
        p_loc = jnp.concatenate([a_buf[j, s * pitch:s * pitch + seg, :] for j in range(nlb)], axis=1)
        out.append(h_loc + p_loc * state)
        state = h_end[s:s + 1] + p_end[s:s + 1] * state
    return jnp.concatenate(out, axis=0), state


def _mixer_kernel(x_ref, mod_ref, modp_ref, n1g_ref, win_ref, convw_ref, convb_ref, wgate_ref, bgate_ref, lam_ref,
                  lng_ref, lnb_ref, sguw_ref, sgub_ref, gl_ref, gs_ref, wout_ref, n2g_ref, rwt_ref, rb_ref,
                  h1_ref, xs_ref, pos_ref, cnt_ref,
                  xa_tail, h_carry, scan_a, scan_b, h1_prev, *, tiles_per_seq):
    t = x_ref.shape[0]
    i = pl.program_id(0)

    @pl.when(i == 0)
    def _():
        h1_prev[...] = jnp.zeros_like(h1_prev)

    @pl.when(i % tiles_per_seq == 0)
    def _():
        xa_tail[...] = jnp.zeros_like(xa_tail)
        h_carry[...] = jnp.zeros_like(h_carry)

    z2b, logits = _router_logits(h1_prev[...], modp_ref[...], n2g_ref, rwt_ref, rb_ref)
    pos_rows, gate_pieces = _route(logits, pos_ref, cnt_ref)

    mod = mod_ref[...]
    sh1, sc1, g1 = mod[0:1], mod[1:2], mod[2:3]

    x = x_ref[...]
    z = _rms(x, n1g_ref[...] * (1.0 + sc1)) + sh1
    proj = jnp.dot(z.astype(BF16), win_ref[...], preferred_element_type=F32)
    perm_blocks = _perm_blocks(pos_rows, TILE_ROWS, t)
    n_pb = len(perm_blocks)
    sort_rows = functools.partial(_sort_rows, perm_blocks, z2b, gate_pieces, xs_ref)
    sort_rows(0, n_pb // 3)
    xa = proj[:, 0:D_LRU]
    ya = proj[:, D_LRU:2 * D_LRU]
    u = proj[:, 2 * D_LRU:2 * D_LRU + D_SGU]
    v = proj[:, 2 * D_LRU + D_SGU:]

    tail = xa_tail[...]
    row8 = lax.broadcasted_iota(I32, (8, 1), 0)
    xc = xa * convw_ref[CONV_WIDTH - 1:CONV_WIDTH, :] + convb_ref[...]
    for sft in range(1, CONV_WIDTH):
        rolled = pltpu.roll(xa, sft, 0)
        head = jnp.where(row8 < sft, pltpu.roll(tail, sft, 0), rolled[0:8])
        shifted = jnp.concatenate([head, rolled[8:]], axis=0)
        xc = xc + shifted * convw_ref[CONV_WIDTH - 1 - sft:CONV_WIDTH - sft, :]
    xa_tail[...] = xa[t - 8:t]

    gates = jnp.dot(xc.astype(BF16), wgate_ref[...], preferred_element_type=F32) + bgate_ref[...]
    sort_rows(n_pb // 3, 2 * n_pb // 3)
    r_gate = _sigmoid(gates[:, 0:D_LRU])
    i_gate = _sigmoid(gates[:, D_LRU:])
    nlam = -lam_ref[...]
    softplus = jnp.maximum(nlam, 0.0) + jnp.log1p(jnp.exp(-jnp.abs(nlam)))
    log_a = (-LRU_C) * r_gate * softplus
    a = jnp.exp(log_a)
    om = -jnp.tanh(log_a) * (a * a + 1.0)
    mult = jnp.where(om > 0.0, om * lax.rsqrt(om), 0.0)
    bterm = mult * i_gate * xc
    sort_rows(2 * n_pb // 3, n_pb)
    h, h_last = _linear_scan(a, bterm, h_carry[...], scan_a, scan_b)
    h_carry[...] = h_last
    o_lru = _rms(h * _gelu_tanh(ya), gl_ref[...])

    ug = _gelu_tanh(u)
    vg = _gelu_tanh(v)
    mu = jnp.mean(vg, axis=-1, keepdims=True)
    vcen = vg - mu
    var = jnp.mean(vcen * vcen, axis=-1, keepdims=True)
    vn = (vcen * lax.rsqrt(var + EPS) * lng_ref[...] + lnb_ref[...]).astype(BF16)
    ri = lax.broadcasted_iota(I32, (CHUNK, CHUNK), 0)
    ci = lax.broadcasted_iota(I32, (CHUNK, CHUNK), 1)
    causal = ri >= ci
    lane = lax.broadcasted_iota(I32, (1, 2 * SGU_HEAD_DIM), 1)
    first_half = lane < SGU_HEAD_DIM
    pair_w = []
    for p in range(SGU_HEADS // 2):
        w0 = jnp.where(causal, sguw_ref[2 * p], 0.0).astype(BF16)
        w1 = jnp.where(causal, sguw_ref[2 * p + 1], 0.0).astype(BF16)
        pair_w.append(jnp.concatenate([w0, w1], axis=1))
    chunks = []
    zero = jnp.zeros((), BF16)
    for n in range(t // CHUNK):
        cols = []
        for p in range(SGU_HEADS // 2):
            blk = vn[n * CHUNK:(n + 1) * CHUNK, p * LANES:(p + 1) * LANES]
            rhs = jnp.concatenate([jnp.where(first_half, blk, zero), jnp.where(first_half, zero, blk)], axis=0)
            cols.append(jnp.dot(pair_w[p], rhs, preferred_element_type=F32))
        chunks.append(jnp.concatenate(cols, axis=1) + sgub_ref[...])
    mixed = jnp.concatenate(chunks, axis=0)
    o_sgu = _rms(ug * mixed, gs_ref[...])

    heads = jnp.concatenate([o_lru, o_sgu], axis=1).astype(BF16)
    h1 = x + g1 * jnp.dot(heads, wout_ref[...], preferred_element_type=F32)
    h1_ref[...] = h1
    h1_prev[...] = h1


def _router_logits(h1, mod, n2g_ref, rwt_ref, rb_ref):
    sh2, sc2 = mod[3:4], mod[4:5]
    z2b = (_rms(h1, n2g_ref[...] * (1.0 + sc2)) + sh2).astype(BF16)
    logits = lax.dot_general(rwt_ref[...].astype(BF16), z2b, (((1,), (1,)), ((), ())),
                             preferred_element_type=F32) + rb_ref[...]
    return z2b, logits


def _route(logits, pos_ref, cnt_ref):
    t = logits.shape[1]
    eidx = lax.broadcasted_iota(I32, (N_EXPERTS, t), 0)
    work = logits
    sel = []
    tops = []
    for k in range(TOP_K):
        m = jnp.max(work, axis=0, keepdims=True)
        idx = jnp.min(jnp.where(work == m, eidx, N_EXPERTS), axis=0, keepdims=True)
        onehot = eidx == idx
        sel.append(onehot)
        tops.append(m)
        work = jnp.where(onehot, -jnp.inf, work)
    exps = [jnp.exp(tk - tops[0]) for tk in tops]
    denom = exps[0] + exps[1] + exps[2] + exps[3]
    chosen = jnp.zeros((N_EXPERTS, t), F32)
    gsel = jnp.zeros((N_EXPERTS, t), F32)
    for k in range(TOP_K):
        chosen = jnp.where(sel[k], 1.0, chosen)
        gsel = jnp.where(sel[k], exps[k] / denom, gsel)

    chosen_b = chosen.astype(BF16)
    si = lax.broadcasted_iota(I32, (t, t), 0)
    ti = lax.broadcasted_iota(I32, (t, t), 1)
    before = jnp.where(si < ti, 1.0, 0.0).astype(BF16)
    excl = jnp.dot(chosen_b, before, preferred_element_type=F32)
    cnt_col = jnp.sum(chosen, axis=1, keepdims=True).astype(I32)
    cnt_ref[...] = jnp.broadcast_to(cnt_col, cnt_ref.shape)
    run_len = ((cnt_col + (ROW_ALIGN - 1)) // ROW_ALIGN * ROW_ALIGN).astype(F32)
    er = lax.broadcasted_iota(I32, (N_EXPERTS, N_EXPERTS), 0)
    ec = lax.broadcasted_iota(I32, (N_EXPERTS, N_EXPERTS), 1)
    lower = jnp.where(ec < er, 1.0, 0.0).astype(BF16)
    run_start = jnp.dot(lower, jnp.broadcast_to(run_len, (N_EXPERTS, LANES)).astype(BF16),
                        preferred_element_type=F32)[:, 0:1]
    posmat = excl + run_start
    pos_rows = []
    for k in range(TOP_K):
        pk = jnp.sum(jnp.where(sel[k], posmat, 0.0), axis=0, keepdims=True).astype(I32)
        pos_ref[k:k + 1, :] = pk
        pos_rows.append(pk)

    g_hi = gsel.astype(BF16)
    rem = gsel - g_hi.astype(F32)
    g_mid = rem.astype(BF16)
    g_lo = (rem - g_mid.astype(F32)).astype(BF16)
    gp = jnp.concatenate([g_hi, g_mid, g_lo, jnp.zeros((PAYLOAD - 3 * N_EXPERTS, t), BF16)], axis=0)
    return pos_rows, gp


def _sort_rows(perm_blocks, z2b, gate_pieces, xs_ref, q0, q1):
    for q in range(q0, q1):
        rows = slice(q * PERM_BLOCK, (q + 1) * PERM_BLOCK)
        xs_ref[rows, 0:D_MODEL] = jnp.dot(perm_blocks[q], z2b, preferred_element_type=F32)
        xs_ref[rows, D_MODEL:] = lax.dot_general(perm_blocks[q], gate_pieces, (((1,), (1,)), ((), ())),
                                                 preferred_element_type=F32)


def _mixer_call(x, mod3, n1g, win, convw, convb, wgate, bgate, lam, lng, lnb, sguw, sgub_full, gl, gs, wout,
                n2g, rwt, rb):
    bsz, seq, d = x.shape
    t = SEQ_TILE
    tiles = seq // t
    n_tiles = bsz * tiles
    x2 = x.reshape(bsz * seq, d)

    def const(shape):
        return pl.BlockSpec(shape, lambda i: (0,) * len(shape))

    mixed = lambda i: jnp.minimum(i, n_tiles - 1)
    routed = lambda i: jnp.maximum(i - 1, 0)
    return pl.pallas_call(
        functools.partial(_mixer_kernel, tiles_per_seq=tiles),
        grid=(n_tiles + 1,),
        in_specs=[
            pl.BlockSpec((t, d), lambda i: (mixed(i), 0)),
            pl.BlockSpec((None, 6, d), lambda i: (mixed(i) // tiles, 0, 0)),
            pl.BlockSpec((None, 6, d), lambda i: (routed(i) // tiles, 0, 0)),
            const((1, d)),
            const((d, 2 * d)),
            const((CONV_WIDTH, D_LRU)),
            const((1, D_LRU)),
            const((D_LRU, 2 * D_LRU)),
            const((1, 2 * D_LRU)),
            const((1, D_LRU)),
            const((1, D_SGU)),
            const((1, D_SGU)),
            const((SGU_HEADS, CHUNK, CHUNK)),
            const((CHUNK, D_SGU)),
            const((1, D_LRU)),
            const((1, D_SGU)),
            const((d, d)),
            const((1, d)),
            const((N_EXPERTS, d)),
            const((N_EXPERTS, 1)),
        ],
        out_specs=[
            pl.BlockSpec((t, d), lambda i: (i, 0)),
            pl.BlockSpec((TILE_ROWS, ROW_W), lambda i: (routed(i), 0)),
            pl.BlockSpec((TOP_K, t), lambda i: (0, routed(i))),
            pl.BlockSpec((None, N_EXPERTS, LANES), lambda i: (routed(i), 0, 0)),
        ],
        out_shape=[
            jax.ShapeDtypeStruct(((n_tiles + 1) * t, d), F32),
            jax.ShapeDtypeStruct((n_tiles * TILE_ROWS, ROW_W), F32),
            jax.ShapeDtypeStruct((TOP_K, n_tiles * t), I32),
            jax.ShapeDtypeStruct((n_tiles, N_EXPERTS, LANES), I32),
        ],
        scratch_shapes=[pltpu.VMEM((8, D_LRU), F32), pltpu.VMEM((1, D_LRU), F32),
                        pltpu.VMEM((D_LRU // LANES, SCAN_SEGMENTS * _scan_pitch(t), LANES), F32),
                        pltpu.VMEM((D_LRU // LANES, SCAN_SEGMENTS * _scan_pitch(t), LANES), F32),
                        pltpu.VMEM((t, d), F32)],
        compiler_params=pltpu.CompilerParams(
            dimension_semantics=("arbitrary",), vmem_limit_bytes=VMEM_LIMIT_BYTES),
        name="mixer_router",
    )(x2, mod3, mod3, n1g, win, convw, convb, wgate, bgate, lam, lng, lnb, sguw, sgub_full, gl, gs, wout, n2g, rwt,
      rb)


def _expert_kernel(be_ref, first_ref, nval_ref, jrow_ref, ilo_ref, ihi_ref, nexte_ref, nb_ref,
                   cnt_ref, cum_ref, base_ref, tailrow_ref, taillen_ref,
                   xs_hbm, wgu_hbm, bgu_ref, wd_hbm, bd_ref,
                   ys_hbm,
                   xbuf, ybuf, zbuf, wgu_st, wd_st, wgu_bf, wd_bf, gsem, ssem, zsem, wsem):
    r = xbuf.shape[1]
    n_tiles = tailrow_ref.shape[0]
    nb = nb_ref[0]

    rows = lambda v: pl.multiple_of(v, ROW_ALIGN)

    def weight_copies(e):
        return (pltpu.make_async_copy(wgu_hbm.at[e], wgu_st, wsem.at[0]),
                pltpu.make_async_copy(wd_hbm.at[e], wd_st, wsem.at[1]))

    def tail_copy(i):
        n = rows(taillen_ref[i])
        return pltpu.make_async_copy(zbuf.at[pl.ds(0, n), :], ys_hbm.at[pl.ds(rows(tailrow_ref[i]), n), :],
                                     zsem.at[0])

    def for_each_run(blk, fn):
        e = be_ref[blk]
        j0 = jrow_ref[blk]

        def tile_body(i, c):
            idx = i * N_EXPERTS + e
            cu = cum_ref[idx]
            lo = jnp.maximum(cu, j0)
            hi = jnp.minimum(cu + cnt_ref[idx], j0 + r)

            @pl.when(hi > lo)
            def _():
                fn(rows(base_ref[idx] + (lo - cu)), rows(lo - j0), rows(hi - lo))
            return c
        lax.fori_loop(ilo_ref[blk], ihi_ref[blk] + 1, tile_body, 0)

    def start_gather(dst_slot):
        def fn(hbm_row, buf_row, n):
            pltpu.make_async_copy(xs_hbm.at[pl.ds(hbm_row, n), :],
                                  xbuf.at[dst_slot, pl.ds(buf_row, n), :], gsem.at[dst_slot]).start()
        return fn

    def start_scatter(src_slot):
        def fn(hbm_row, buf_row, n):
            pltpu.make_async_copy(ybuf.at[src_slot, pl.ds(buf_row, n), :],
                                  ys_hbm.at[pl.ds(hbm_row, n), :], ssem.at[src_slot]).start()
        return fn

    def wait_gather(s, n):
        pltpu.make_async_copy(xs_hbm.at[pl.ds(0, rows(n)), :], xbuf.at[s, pl.ds(0, rows(n)), :], gsem.at[s]).wait()

    def wait_scatter(s, n):
        pltpu.make_async_copy(ybuf.at[s, pl.ds(0, rows(n)), :], ys_hbm.at[pl.ds(0, rows(n)), :], ssem.at[s]).wait()

    for c in weight_copies(be_ref[0]):
        c.start()
    xbuf[...] = jnp.zeros_like(xbuf)
    for_each_run(0, start_gather(0))
    zbuf[...] = jnp.zeros_like(zbuf)

    def start_tail(i, c):
        tail_copy(i).start()
        return c
    lax.fori_loop(0, n_tiles, start_tail, 0)

    def block(b, carry):
        slot = b & 1
        wait_gather(slot, nval_ref[b])

        @pl.when(b + 1 < nb)
        def _():
            for_each_run(b + 1, start_gather(1 - slot))

        @pl.when(first_ref[b] == 1)
        def _():
            for c in weight_copies(be_ref[b]):
                c.wait()
            wgu_bf[...] = wgu_st[...].astype(BF16)
            wd_bf[...] = wd_st[...].astype(BF16)

            @pl.when(nexte_ref[b] >= 0)
            def _():
                for c in weight_copies(nexte_ref[b]):
                    c.start()

        @pl.when(b >= 2)
        def _():
            wait_scatter(slot, nval_ref[jnp.maximum(b - 2, 0)])

        def expert_rows(m):
            xw = xbuf[slot, 0:m]
            xb = xw[:, 0:D_MODEL].astype(BF16)
            lane = lax.broadcasted_iota(I32, (1, PAYLOAD), 1)
            gate = jnp.sum(jnp.where((lane & (N_EXPERTS - 1)) == be_ref[b], xw[:, D_MODEL:], 0.0),
                           axis=1, keepdims=True)
            gu = jnp.dot(xb, wgu_bf[...], preferred_element_type=F32) + bgu_ref[be_ref[b]]
            g = jnp.minimum(gu[:, 0:D_MODEL], SWIGLU_LIMIT)
            lin = jnp.clip(gu[:, D_MODEL:], -SWIGLU_LIMIT, SWIGLU_LIMIT)
            act = g * _sigmoid(SWIGLU_ALPHA * g) * (lin + 1.0)
            y = jnp.dot(act.astype(BF16), wd_bf[...], preferred_element_type=F32) + bd_ref[be_ref[b]]
            ybuf[slot, 0:m] = _pack_bf16_pairs(y * gate)

        @pl.when(nval_ref[b] > r // 2)
        def _():
            expert_rows(r)

        @pl.when(nval_ref[b] <= r // 2)
        def _():
            expert_rows(r // 2)

        for_each_run(b, start_scatter(slot))
        return carry

    lax.fori_loop(0, nb, block, 0)

    last = nb - 1
    wait_scatter(last & 1, nval_ref[last])

    @pl.when(nb >= 2)
    def _():
        wait_scatter(1 - (last & 1), nval_ref[jnp.maximum(last - 1, 0)])

    def wait_tail(i, c):
        tail_copy(i).wait()
        return c
    lax.fori_loop(0, n_tiles, wait_tail, 0)


def _expert_call(tables, xs, w_gu, b_gu, w_down, b_down):
    nblk = tables[0].shape[0]
    r = ROW_BLOCK
    d = D_MODEL
    whole = lambda i, *_: (0, 0, 0)
    grid_spec = pltpu.PrefetchScalarGridSpec(
        num_scalar_prefetch=len(tables),
        grid=(1,),
        in_specs=[
            pl.BlockSpec(memory_space=pl.ANY),
            pl.BlockSpec(memory_space=pl.ANY),
            pl.BlockSpec((N_EXPERTS, 1, 2 * d), whole),
            pl.BlockSpec(memory_space=pl.ANY),
            pl.BlockSpec((N_EXPERTS, 1, d), whole),
        ],
        out_specs=pl.BlockSpec(memory_space=pl.ANY),
        scratch_shapes=[
            pltpu.VMEM((2, r, ROW_W), F32),
            pltpu.VMEM((2, r, d // 2), U32),
            pltpu.VMEM((ROW_ALIGN * N_EXPERTS, d // 2), U32),
            pltpu.VMEM((d, 2 * d), F32),
            pltpu.VMEM((d, d), F32),
            pltpu.VMEM((d, 2 * d), BF16),
            pltpu.VMEM((d, d), BF16),
            pltpu.SemaphoreType.DMA((2,)),
            pltpu.SemaphoreType.DMA((2,)),
            pltpu.SemaphoreType.DMA((1,)),
            pltpu.SemaphoreType.DMA((2,)),
        ],
    )
    return pl.pallas_call(
        _expert_kernel,
        grid_spec=grid_spec,
        out_shape=jax.ShapeDtypeStruct((xs.shape[0], d // 2), U32),
        compiler_params=pltpu.CompilerParams(
            dimension_semantics=("arbitrary",), vmem_limit_bytes=VMEM_LIMIT_BYTES),
        name="experts",
    )(*tables, xs, w_gu, b_gu, w_down, b_down)


def _final_kernel(h1_ref, mod_ref, ys_ref, pos_ref, fg_ref, o_ref):
    t = h1_ref.shape[0]
    g2 = mod_ref[5:6, :]
    pos_rows = [pos_ref[k:k + 1, :] for k in range(TOP_K)]
    perm = jnp.concatenate(_perm_blocks(pos_rows, TILE_ROWS, t), axis=0)
    tn = (((0,), (0,)), ((), ()))
    y_lo, y_hi = _unpack_bf16_pairs(ys_ref[...])
    moe = jnp.concatenate([lax.dot_general(perm, y_lo, tn, preferred_element_type=F32),
                           lax.dot_general(perm, y_hi, tn, preferred_element_type=F32)], axis=1)
    o_ref[...] = _rms(h1_ref[...] + g2 * moe, fg_ref[...])


def _final_call(h1, mod3, ys, pos, final_g, bsz, seq):
    d = h1.shape[1]
    n_tok = bsz * seq
    t = SEQ_TILE
    per_batch = seq // t
    out = pl.pallas_call(
        _final_kernel,
        grid=(n_tok // t,),
        in_specs=[
            pl.BlockSpec((t, d), lambda i: (i, 0)),
            pl.BlockSpec((None, 6, d), lambda i: (i // per_batch, 0, 0)),
            pl.BlockSpec((TILE_ROWS, d // 2), lambda i: (i, 0)),
            pl.BlockSpec((TOP_K, t), lambda i: (0, i)),
            pl.BlockSpec((1, d), lambda i: (0, 0)),
        ],
        out_specs=pl.BlockSpec((t, d), lambda i: (i, 0)),
        out_shape=jax.ShapeDtypeStruct((n_tok, d), F32),
        compiler_params=pltpu.CompilerParams(
            dimension_semantics=("arbitrary",), vmem_limit_bytes=VMEM_LIMIT_BYTES),
        name="combine_final",
    )(h1, mod3, ys, pos, final_g.reshape(1, d))
    return out.reshape(bsz, seq, d)


def _block_diag(w):
    h, i, o = w.shape
    eye = jnp.eye(h, dtype=w.dtype)
    return (w[:, :, None, :] * eye[:, None, :, None]).reshape(h * i, h * o)


def _route_tables(cnt):
    r = ROW_BLOCK
    n_tiles = cnt.shape[0]
    nblk = n_tiles * TILE_ROWS // r + N_EXPERTS
    cnt = (cnt + (ROW_ALIGN - 1)) // ROW_ALIGN * ROW_ALIGN
    ie = jnp.arange(N_EXPERTS, dtype=I32)
    it = jnp.arange(n_tiles, dtype=I32)
    e_before = (ie[:, None] < ie[None, :]).astype(I32)
    t_before = (it[:, None] < it[None, :]).astype(I32)
    total = jnp.sum(cnt, axis=0)
    cum = jnp.sum(t_before[:, :, None] * cnt[:, None, :], axis=0)
    seg_off = jnp.sum(cnt[:, :, None] * e_before[None, :, :], axis=1)
    base = seg_off + it[:, None] * TILE_ROWS
    nblk_e = (total + r - 1) // r
    blk_start = jnp.sum(nblk_e[:, None] * e_before, axis=0)
    blk_end = blk_start + nblk_e
    nb_used = jnp.sum(nblk_e)
    blk = jnp.arange(nblk, dtype=I32)
    used = blk < nb_used
    last_e = jnp.max(jnp.where(nblk_e > 0, ie, 0))
    be = jnp.minimum(jnp.sum((blk[:, None] >= blk_end[None, :]).astype(I32), axis=1), N_EXPERTS - 1)
    be = jnp.where(used, be, last_e)
    be_onehot = (be[:, None] == ie[None, :]).astype(I32)
    pick = lambda v: jnp.sum(be_onehot * v[None, :], axis=1)
    jrow = jnp.where(used, (blk - pick(blk_start)) * r, 0)
    nval = jnp.where(used, jnp.clip(pick(total) - jrow, 0, r), 0)
    first = jnp.concatenate([jnp.ones((1,), I32), (be[1:] != be[:-1]).astype(I32)])
    nxt = pick(blk_end)
    next_e = jnp.where(nxt < nb_used, jnp.sum((nxt[:, None] >= blk_end[None, :]).astype(I32), axis=1), -1)
    run_start = jnp.sum(be_onehot[:, None, :] * cum[None, :, :], axis=-1)
    run_end = run_start + jnp.sum(be_onehot[:, None, :] * cnt[None, :, :], axis=-1)
    ilo = jnp.sum((run_end <= jrow[:, None]).astype(I32), axis=1)
    ihi = n_tiles - 1 - jnp.sum((run_start >= (jrow + r)[:, None]).astype(I32), axis=1)
    used_rows = jnp.sum(cnt, axis=1)
    tail_row = jnp.arange(n_tiles, dtype=I32) * TILE_ROWS + used_rows
    tail_len = TILE_ROWS - used_rows
    i32 = lambda v: v.astype(I32)
    return (i32(be), i32(first), i32(nval), i32(jrow), i32(jnp.minimum(ilo, n_tiles - 1)), i32(ihi), i32(next_e),
            i32(nb_used).reshape(1), i32(cnt).reshape(-1), i32(cum).reshape(-1), i32(base).reshape(-1),
            i32(tail_row), i32(tail_len))


def kernel(x, c, ada_w, ada_b, norm1_g, w_in, conv_w, conv_b, lru_wr, lru_br, lru_wi, lru_bi, lru_lambda, sgu_ln_g, sgu_ln_b, sgu_w, sgu_b, gnorm_lru_g, gnorm_sgu_g, w_out, norm2_g, router_w, router_b, exp_w_gu, exp_b_gu, exp_w_down, exp_b_down, final_g):
    bsz, seq, d = x.shape
    depth = ada_w.shape[0]
    assert depth == 1 and d == D_MODEL and seq % SEQ_TILE == 0 and ROW_BLOCK >= ROW_ALIGN * N_EXPERTS
    l = 0
    mod = _adaln_call(c, ada_w[l], ada_b[l])
    mod3 = mod.reshape(bsz, 6, d)

    row = lambda v: v.reshape(1, -1)
    wgate = jnp.concatenate([_block_diag(lru_wr[l]), _block_diag(lru_wi[l])], axis=1).astype(BF16)
    bgate = jnp.concatenate([lru_br[l], lru_bi[l]]).reshape(1, -1)
    sgub_full = jnp.repeat(sgu_b[l].T, SGU_HEAD_DIM, axis=1)
    h1, xs, pos, cnt = _mixer_call(
        x, mod3, row(norm1_g[l]), w_in[l].astype(BF16), conv_w[l], row(conv_b[l]), wgate, bgate,
        row(lru_lambda[l]), row(sgu_ln_g[l]), row(sgu_ln_b[l]), sgu_w[l], sgub_full,
        row(gnorm_lru_g[l]), row(gnorm_sgu_g[l]), w_out[l].astype(BF16), row(norm2_g[l]),
        router_w[l].T, router_b[l].reshape(-1, 1))

    tables = _route_tables(cnt[:, :, 0])
    ys = _expert_call(tables, xs, exp_w_gu[l], exp_b_gu[l].reshape(N_EXPERTS, 1, -1),
                      exp_w_down[l], exp_b_down[l].reshape(N_EXPERTS, 1, -1))
    return _final_call(h1, mod3, ys, pos, final_g, bsz, seq)
```

```python
import functools

import jax
import jax.numpy as jnp
from jax import lax
from jax.experimental import pallas as pl
from jax.experimental.pallas import tpu as pltpu

F32 = jnp.float32
BF16 = jnp.bfloat16
I32 = jnp.int32
U32 = jnp.uint32

D_MODEL = 1024
D_LRU = 512
D_SGU = 512
LRU_HEADS = 8
CONV_WIDTH = 4
LRU_C = 8.0
SGU_HEADS = 8
SGU_HEAD_DIM = D_SGU // SGU_HEADS
CHUNK = 128
N_EXPERTS = 32
TOP_K = 4
SWIGLU_ALPHA = 1.702
SWIGLU_LIMIT = 7.0
EPS = 1e-6

LANES = 128
SEQ_TILE = 512
ROW_ALIGN = 8
TILE_ROWS = TOP_K * SEQ_TILE + ROW_ALIGN * N_EXPERTS
PAYLOAD = LANES
ROW_W = D_MODEL + PAYLOAD
ROW_BLOCK = 512
VMEM_LIMIT_BYTES = 56 * 1024 * 1024


def _sigmoid(x):
    return 1.0 / (1.0 + jnp.exp(-x))


def _gelu_tanh(x):
    return 0.5 * x * (1.0 + jnp.tanh(0.7978845608028654 * (x + 0.044715 * (x * x * x))))


def _rms(x, g):
    ms = jnp.mean(x * x, axis=-1, keepdims=True)
    return x * lax.rsqrt(ms + EPS) * g


def _pack_bf16_pairs(v):
    n = v.shape[1] // 2
    r = v.astype(BF16).astype(F32)
    bits = lax.bitcast_convert_type(r, U32)
    return (bits[:, 0:n] >> 16) | (bits[:, n:] & jnp.uint32(0xFFFF0000))


def _unpack_bf16_pairs(w):
    lo = lax.bitcast_convert_type(w << 16, F32).astype(BF16)
    hi = lax.bitcast_convert_type(w & jnp.uint32(0xFFFF0000), F32).astype(BF16)
    return lo, hi


PERM_BLOCK = 256


def _perm_blocks(pos_rows, n_rows, n_cols):
    blk = PERM_BLOCK
    rid = lax.broadcasted_iota(I32, (blk, n_cols), 0).astype(F32).astype(BF16)
    one = jnp.ones((), BF16)
    blocks = []
    for q in range(n_rows // blk):
        acc = jnp.zeros((blk, n_cols), BF16)
        for pk in pos_rows:
            in_blk = jnp.where(pk // blk == q, pk % blk, -1).astype(F32).astype(BF16)
            acc = jnp.where(rid == in_blk, one, acc)
        blocks.append(acc)
    return blocks


def _adaln_kernel(c_ref, w_ref, b_ref, o_ref):
    c = c_ref[...]
    ca = c * _sigmoid(c)
    o_ref[...] = jnp.dot(ca.astype(BF16), w_ref[...].astype(BF16), preferred_element_type=F32) + b_ref[...]


def _adaln_call(c, w, b):
    bsz, d = c.shape
    n_out = w.shape[1]
    return pl.pallas_call(
        _adaln_kernel,
        grid=(n_out // d,),
        in_specs=[
            pl.BlockSpec((bsz, d), lambda j: (0, 0)),
            pl.BlockSpec((d, d), lambda j: (0, j)),
            pl.BlockSpec((1, d), lambda j: (0, j)),
        ],
        out_specs=pl.BlockSpec((bsz, d), lambda j: (0, j)),
        out_shape=jax.ShapeDtypeStruct((bsz, n_out), F32),
        name="adaln",
    )(c, w, b.reshape(1, n_out))


SCAN_SEGMENTS = 8
SCAN_PITCH_PAD = 8


def _scan_pitch(t):
    return t // SCAN_SEGMENTS + SCAN_PITCH_PAD


def _linear_scan(a, b, h0, a_buf, b_buf):
    t, c = a.shape
    nseg = SCAN_SEGMENTS
    seg = t // nseg
    pitch = _scan_pitch(t)
    nlb = c // LANES
    for j in range(nlb):
        for s in range(nseg):
            a_buf[j, s * pitch:s * pitch + seg, :] = a[s * seg:(s + 1) * seg, j * LANES:(j + 1) * LANES]
            b_buf[j, s * pitch:s * pitch + seg, :] = b[s * seg:(s + 1) * seg, j * LANES:(j + 1) * LANES]

    def step(g, carry):
        hs, ps = carry
        new_h, new_p = [], []
        for j in range(nlb):
            view = (j, pl.ds(g, nseg, stride=pitch), slice(None))
            ag = a_buf[view]
            hj = ag * hs[j] + b_buf[view]
            pj = ag * ps[j]
            b_buf[view] = hj
            a_buf[view] = pj
            new_h.append(hj)
            new_p.append(pj)
        return tuple(new_h), tuple(new_p)

    init = (tuple(jnp.zeros((nseg, LANES), F32) for _ in range(nlb)),
            tuple(jnp.ones((nseg, LANES), F32) for _ in range(nlb)))
    h_end, p_end = lax.fori_loop(0, seg, step, init, unroll=True)
    h_end = jnp.concatenate(h_end, axis=1)
    p_end = jnp.concatenate(p_end, axis=1)
    state = h0
    out = []
    for s in range(nseg):
        h_loc = jnp.concatenate([b_buf[j, s * pitch:s * pitch + seg, :] for j in range(nlb)], axis=1)
        p_loc = jnp.concatenate([a_buf[j, s * pitch:s * pitch + seg, :] for j in range(nlb)], axis=1)
        out.append(h_loc + p_loc * state)
        state = h_end[s:s + 1] + p_end[s:s + 1] * state
    return jnp.concatenate(out, axis=0), state


def _mixer_kernel(x_ref, mod_ref, modp_ref, n1g_ref, win_ref, convw_ref, convb_ref, wgate_ref, bgate_ref, lam_ref,
                  lng_ref, lnb_ref, sguw_ref, sgub_ref, gl_ref, gs_ref, wout_ref, n2g_ref, rwt_ref, rb_ref,
                  h1_ref, xs_ref, pos_ref, cnt_ref,
                  xa_tail, h_carry, scan_a, scan_b, h1_prev, *, tiles_per_seq):
    t = x_ref.shape[0]
    i = pl.program_id(0)

    @pl.when(i == 0)
    def _():
        h1_prev[...] = jnp.zeros_like(h1_prev)

    @pl.when(i % tiles_per_seq == 0)
    def _():
        xa_tail[...] = jnp.zeros_like(xa_tail)
        h_carry[...] = jnp.zeros_like(h_carry)

    z2b, logits = _router_logits(h1_prev[...], modp_ref[...], n2g_ref, rwt_ref, rb_ref)
    pos_rows, gate_pieces = _route(logits, pos_ref, cnt_ref)

    mod = mod_ref[...]
    sh1, sc1, g1 = mod[0:1], mod[1:2], mod[2:3]

    x = x_ref[...]
    z = _rms(x, n1g_ref[...] * (1.0 + sc1)) + sh1
    proj = jnp.dot(z.astype(BF16), win_ref[...], preferred_element_type=F32)
    perm_blocks = _perm_blocks(pos_rows, TILE_ROWS, t)
    n_pb = len(perm_blocks)
    sorted_blocks = iter(range(n_pb))

    def sort_next():
        q = next(sorted_blocks)
        _sort_rows(perm_blocks, z2b, gate_pieces, xs_ref, q, q + 1)

    sort_next()
    xa = proj[:, 0:D_LRU]
    ya = proj[:, D_LRU:2 * D_LRU]
    u = proj[:, 2 * D_LRU:2 * D_LRU + D_SGU]
    v = proj[:, 2 * D_LRU + D_SGU:]

    tail = xa_tail[...]
    row8 = lax.broadcasted_iota(I32, (8, 1), 0)
    xc = xa * convw_ref[CONV_WIDTH - 1:CONV_WIDTH, :] + convb_ref[...]
    for sft in range(1, CONV_WIDTH):
        rolled = pltpu.roll(xa, sft, 0)
        head = jnp.where(row8 < sft, pltpu.roll(tail, sft, 0), rolled[0:8])
        shifted = jnp.concatenate([head, rolled[8:]], axis=0)
        xc = xc + shifted * convw_ref[CONV_WIDTH - 1 - sft:CONV_WIDTH - sft, :]
    xa_tail[...] = xa[t - 8:t]
    sort_next()

    xcb = xc.astype(BF16)
    hw = D_LRU // 2
    g_lo = jnp.dot(xcb[:, 0:hw], wgate_ref[0], preferred_element_type=F32)
    g_hi = jnp.dot(xcb[:, hw:], wgate_ref[1], preferred_element_type=F32)
    sort_next()
    r_gate = _sigmoid(jnp.concatenate([g_lo[:, 0:hw], g_hi[:, 0:hw]], axis=1) + bgate_ref[:, 0:D_LRU])
    i_gate = _sigmoid(jnp.concatenate([g_lo[:, hw:], g_hi[:, hw:]], axis=1) + bgate_ref[:, D_LRU:])
    sort_next()
    nlam = -lam_ref[...]
    softplus = jnp.maximum(nlam, 0.0) + jnp.log1p(jnp.exp(-jnp.abs(nlam)))
    log_a = (-LRU_C) * r_gate * softplus
    a = jnp.exp(log_a)
    sort_next()
    om = -jnp.tanh(log_a) * (a * a + 1.0)
    mult = jnp.where(om > 0.0, om * lax.rsqrt(om), 0.0)
    bterm = mult * i_gate * xc
    sort_next()
    h, h_last = _linear_scan(a, bterm, h_carry[...], scan_a, scan_b)
    h_carry[...] = h_last
    sort_next()
    o_lru = _rms(h * _gelu_tanh(ya), gl_ref[...])
    sort_next()

    ug = _gelu_tanh(u)
    vg = _gelu_tanh(v)
    sort_next()
    assert next(sorted_blocks, None) is None
    mu = jnp.mean(vg, axis=-1, keepdims=True)
    vcen = vg - mu
    var = jnp.mean(vcen * vcen, axis=-1, keepdims=True)
    vn = (vcen * lax.rsqrt(var + EPS) * lng_ref[...] + lnb_ref[...]).astype(BF16)
    ri = lax.broadcasted_iota(I32, (CHUNK, CHUNK), 0)
    ci = lax.broadcasted_iota(I32, (CHUNK, CHUNK), 1)
    causal = ri >= ci
    lane = lax.broadcasted_iota(I32, (1, 2 * SGU_HEAD_DIM), 1)
    first_half = lane < SGU_HEAD_DIM
    pair_w = []
    for p in range(SGU_HEADS // 2):
        w0 = jnp.where(causal, sguw_ref[2 * p], 0.0).astype(BF16)
        w1 = jnp.where(causal, sguw_ref[2 * p + 1], 0.0).astype(BF16)
        pair_w.append(jnp.concatenate([w0, w1], axis=1))
    chunks = []
    zero = jnp.zeros((), BF16)
    for n in range(t // CHUNK):
        cols = []
        for p in range(SGU_HEADS // 2):
            blk = vn[n * CHUNK:(n + 1) * CHUNK, p * LANES:(p + 1) * LANES]
            rhs = jnp.concatenate([jnp.where(first_half, blk, zero), jnp.where(first_half, zero, blk)], axis=0)
            cols.append(jnp.dot(pair_w[p], rhs, preferred_element_type=F32))
        chunks.append(jnp.concatenate(cols, axis=1) + sgub_ref[...])
    mixed = jnp.concatenate(chunks, axis=0)
    o_sgu = _rms(ug * mixed, gs_ref[...])

    heads = jnp.concatenate([o_lru, o_sgu], axis=1).astype(BF16)
    h1 = x + g1 * jnp.dot(heads, wout_ref[...], preferred_element_type=F32)
    h1_ref[...] = h1
    h1_prev[...] = h1


def _router_logits(h1, mod, n2g_ref, rwt_ref, rb_ref):
    sh2, sc2 = mod[3:4], mod[4:5]
    z2b = (_rms(h1, n2g_ref[...] * (1.0 + sc2)) + sh2).astype(BF16)
    logits = lax.dot_general(rwt_ref[...].astype(BF16), z2b, (((1,), (1,)), ((), ())),
                             preferred_element_type=F32) + rb_ref[...]
    return z2b, logits


def _route(logits, pos_ref, cnt_ref):
    t = logits.shape[1]
    eidx = lax.broadcasted_iota(I32, (N_EXPERTS, t), 0)
    work = logits
    sel = []
    tops = []
    for k in range(TOP_K):
        m = jnp.max(work, axis=0, keepdims=True)
        idx = jnp.min(jnp.where(work == m, eidx, N_EXPERTS), axis=0, keepdims=True)
        onehot = eidx == idx
        sel.append(onehot)
        tops.append(m)
        work = jnp.where(onehot, -jnp.inf, work)
    exps = [jnp.exp(tk - tops[0]) for tk in tops]
    denom = exps[0] + exps[1] + exps[2] + exps[3]
    chosen = jnp.zeros((N_EXPERTS, t), F32)
    gsel = jnp.zeros((N_EXPERTS, t), F32)
    for k in range(TOP_K):
        chosen = jnp.where(sel[k], 1.0, chosen)
        gsel = jnp.where(sel[k], exps[k] / denom, gsel)

    chosen_b = chosen.astype(BF16)
    si = lax.broadcasted_iota(I32, (t, t), 0)
    ti = lax.broadcasted_iota(I32, (t, t), 1)
    before = jnp.where(si < ti, 1.0, 0.0).astype(BF16)
    excl = jnp.dot(chosen_b, before, preferred_element_type=F32)
    cnt_col = jnp.sum(chosen, axis=1, keepdims=True).astype(I32)
    cnt_ref[...] = jnp.broadcast_to(cnt_col, cnt_ref.shape)
    run_len = ((cnt_col + (ROW_ALIGN - 1)) // ROW_ALIGN * ROW_ALIGN).astype(F32)
    er = lax.broadcasted_iota(I32, (N_EXPERTS, N_EXPERTS), 0)
    ec = lax.broadcasted_iota(I32, (N_EXPERTS, N_EXPERTS), 1)
    lower = jnp.where(ec < er, 1.0, 0.0).astype(BF16)
    run_start = jnp.dot(lower, jnp.broadcast_to(run_len, (N_EXPERTS, LANES)).astype(BF16),
                        preferred_element_type=F32)[:, 0:1]
    posmat = excl + run_start
    pos_rows = []
    for k in range(TOP_K):
        pk = jnp.sum(jnp.where(sel[k], posmat, 0.0), axis=0, keepdims=True).astype(I32)
        pos_ref[k:k + 1, :] = pk
        pos_rows.append(pk)

    g_hi = gsel.astype(BF16)
    rem = gsel - g_hi.astype(F32)
    g_mid = rem.astype(BF16)
    g_lo = (rem - g_mid.astype(F32)).astype(BF16)
    gp = jnp.concatenate([g_hi, g_mid, g_lo, jnp.zeros((PAYLOAD - 3 * N_EXPERTS, t), BF16)], axis=0)
    return pos_rows, gp


def _sort_rows(perm_blocks, z2b, gate_pieces, xs_ref, q0, q1):
    for q in range(q0, q1):
        rows = slice(q * PERM_BLOCK, (q + 1) * PERM_BLOCK)
        xs_ref[rows, 0:D_MODEL] = jnp.dot(perm_blocks[q], z2b, preferred_element_type=F32)
        xs_ref[rows, D_MODEL:] = lax.dot_general(perm_blocks[q], gate_pieces, (((1,), (1,)), ((), ())),
                                                 preferred_element_type=F32)


def _mixer_call(x, mod3, n1g, win, convw, convb, wgate, bgate, lam, lng, lnb, sguw, sgub_full, gl, gs, wout,
                n2g, rwt, rb):
    bsz, seq, d = x.shape
    t = SEQ_TILE
    tiles = seq // t
    n_tiles = bsz * tiles
    x2 = x.reshape(bsz * seq, d)

    def const(shape):
        return pl.BlockSpec(shape, lambda i: (0,) * len(shape))

    mixed = lambda i: jnp.minimum(i, n_tiles - 1)
    routed = lambda i: jnp.maximum(i - 1, 0)
    return pl.pallas_call(
        functools.partial(_mixer_kernel, tiles_per_seq=tiles),
        grid=(n_tiles + 1,),
        in_specs=[
            pl.BlockSpec((t, d), lambda i: (mixed(i), 0)),
            pl.BlockSpec((None, 6, d), lambda i: (mixed(i) // tiles, 0, 0)),
            pl.BlockSpec((None, 6, d), lambda i: (routed(i) // tiles, 0, 0)),
            const((1, d)),
            const((d, 2 * d)),
            const((CONV_WIDTH, D_LRU)),
            const((1, D_LRU)),
            const((2, D_LRU // 2, D_LRU)),
            const((1, 2 * D_LRU)),
            const((1, D_LRU)),
            const((1, D_SGU)),
            const((1, D_SGU)),
            const((SGU_HEADS, CHUNK, CHUNK)),
            const((CHUNK, D_SGU)),
            const((1, D_LRU)),
            const((1, D_SGU)),
            const((d, d)),
            const((1, d)),
            const((N_EXPERTS, d)),
            const((N_EXPERTS, 1)),
        ],
        out_specs=[
            pl.BlockSpec((t, d), lambda i: (i, 0)),
            pl.BlockSpec((TILE_ROWS, ROW_W), lambda i: (routed(i), 0)),
            pl.BlockSpec((TOP_K, t), lambda i: (0, routed(i))),
            pl.BlockSpec((None, N_EXPERTS, LANES), lambda i: (routed(i), 0, 0)),
        ],
        out_shape=[
            jax.ShapeDtypeStruct(((n_tiles + 1) * t, d), F32),
            jax.ShapeDtypeStruct((n_tiles * TILE_ROWS, ROW_W), F32),
            jax.ShapeDtypeStruct((TOP_K, n_tiles * t), I32),
            jax.ShapeDtypeStruct((n_tiles, N_EXPERTS, LANES), I32),
        ],
        scratch_shapes=[pltpu.VMEM((8, D_LRU), F32), pltpu.VMEM((1, D_LRU), F32),
                        pltpu.VMEM((D_LRU // LANES, SCAN_SEGMENTS * _scan_pitch(t), LANES), F32),
                        pltpu.VMEM((D_LRU // LANES, SCAN_SEGMENTS * _scan_pitch(t), LANES), F32),
                        pltpu.VMEM((t, d), F32)],
        compiler_params=pltpu.CompilerParams(
            dimension_semantics=("arbitrary",), vmem_limit_bytes=VMEM_LIMIT_BYTES),
        name="mixer_router",
    )(x2, mod3, mod3, n1g, win, convw, convb, wgate, bgate, lam, lng, lnb, sguw, sgub_full, gl, gs, wout, n2g, rwt,
      rb)


def _expert_kernel(be_ref, first_ref, nval_ref, jrow_ref, ilo_ref, ihi_ref, nexte_ref, nb_ref,
                   cnt_ref, cum_ref, base_ref, tailrow_ref, taillen_ref,
                   xs_hbm, wgu_hbm, bgu_ref, wd_hbm, bd_ref,
                   ys_hbm,
                   xbuf, ybuf, zbuf, wgu_st, wd_st, wgu_bf, wd_bf, gsem, ssem, zsem, wsem):
    r = ROW_BLOCK
    n_tiles = tailrow_ref.shape[0]
    nb = nb_ref[0]

    rows = lambda v: pl.multiple_of(v, ROW_ALIGN)

    def weight_copies(e):
        return (pltpu.make_async_copy(wgu_hbm.at[e], wgu_st, wsem.at[0]),
                pltpu.make_async_copy(wd_hbm.at[e], wd_st, wsem.at[1]))

    def tail_copy(i):
        n = rows(taillen_ref[i])
        return pltpu.make_async_copy(zbuf.at[pl.ds(0, n), :], ys_hbm.at[pl.ds(rows(tailrow_ref[i]), n), :],
                                     zsem.at[0])

    def run_of(blk, i):
        idx = i * N_EXPERTS + be_ref[blk]
        cu = cum_ref[idx]
        lo = jnp.maximum(cu, jrow_ref[blk])
        hi = jnp.minimum(cu + cnt_ref[idx], jrow_ref[blk] + r)
        return base_ref[idx] + (lo - cu), lo - jrow_ref[blk], hi - lo

    def for_each_run(blk, fn):
        def tile_body(i, c):
            src, dst, n = run_of(blk, i)

            @pl.when(n > 0)
            def _():
                fn(rows(src), rows(dst), rows(n))
            return c
        lax.fori_loop(ilo_ref[blk], ihi_ref[blk] + 1, tile_body, 0)

    def start_gather(dst_slot):
        def fn(hbm_row, buf_row, n):
            pltpu.make_async_copy(xs_hbm.at[pl.ds(hbm_row, n), :],
                                  xbuf.at[dst_slot, pl.ds(buf_row, n), :], gsem.at[dst_slot]).start()
        return fn

    def start_scatter(src_slot):
        def fn(hbm_row, buf_row, n):
            pltpu.make_async_copy(ybuf.at[src_slot, pl.ds(buf_row, n), :],
                                  ys_hbm.at[pl.ds(hbm_row, n), :], ssem.at[src_slot]).start()
        return fn

    def wait_gather(s, n):
        pltpu.make_async_copy(xs_hbm.at[pl.ds(0, rows(n)), :], xbuf.at[s, pl.ds(0, rows(n)), :], gsem.at[s]).wait()

    def wait_scatter(s, n):
        pltpu.make_async_copy(ybuf.at[s, pl.ds(0, rows(n)), :], ys_hbm.at[pl.ds(0, rows(n)), :], ssem.at[s]).wait()

    for c in weight_copies(be_ref[0]):
        c.start()
    xbuf[...] = jnp.zeros_like(xbuf)
    for_each_run(0, start_gather(0))
    zbuf[...] = jnp.zeros_like(zbuf)

    def start_tail(i, c):
        tail_copy(i).start()
        return c
    lax.fori_loop(0, n_tiles, start_tail, 0)

    def block(b, carry):
        slot = b & 1
        wait_gather(slot, nval_ref[b])

        @pl.when(b + 1 < nb)
        def _():
            for_each_run(b + 1, start_gather(1 - slot))

        @pl.when(first_ref[b] == 1)
        def _():
            for c in weight_copies(be_ref[b]):
                c.wait()
            wgu_bf[...] = wgu_st[...].astype(BF16)
            wd_bf[...] = wd_st[...].astype(BF16)

            @pl.when(nexte_ref[b] >= 0)
            def _():
                for c in weight_copies(nexte_ref[b]):
                    c.start()

        @pl.when(b >= 2)
        def _():
            wait_scatter(slot, nval_ref[jnp.maximum(b - 2, 0)])

        def expert_rows(m):
            xw = xbuf[slot, 0:m]
            xb = xw[:, 0:D_MODEL].astype(BF16)
            lane = lax.broadcasted_iota(I32, (1, PAYLOAD), 1)
            gate = jnp.sum(jnp.where((lane & (N_EXPERTS - 1)) == be_ref[b], xw[:, D_MODEL:], 0.0),
                           axis=1, keepdims=True)
            gu = jnp.dot(xb, wgu_bf[...], preferred_element_type=F32) + bgu_ref[be_ref[b]]
            g = jnp.minimum(gu[:, 0:D_MODEL], SWIGLU_LIMIT)
            lin = jnp.clip(gu[:, D_MODEL:], -SWIGLU_LIMIT, SWIGLU_LIMIT)
            act = g * _sigmoid(SWIGLU_ALPHA * g) * (lin + 1.0)
            y = jnp.dot(act.astype(BF16), wd_bf[...], preferred_element_type=F32) + bd_ref[be_ref[b]]
            ybuf[slot, 0:m] = _pack_bf16_pairs(y * gate)

        @pl.when(nval_ref[b] > r // 2)
        def _():
            expert_rows(r)

        @pl.when(nval_ref[b] <= r // 2)
        def _():
            expert_rows(r // 2)

        for_each_run(b, start_scatter(slot))
        return carry

    lax.fori_loop(0, nb, block, 0)

    last = nb - 1
    wait_scatter(last & 1, nval_ref[last])

    @pl.when(nb >= 2)
    def _():
        wait_scatter(1 - (last & 1), nval_ref[jnp.maximum(last - 1, 0)])

    def wait_tail(i, c):
        tail_copy(i).wait()
        return c
    lax.fori_loop(0, n_tiles, wait_tail, 0)


def _expert_call(tables, xs, w_gu, b_gu, w_down, b_down):
    nblk = tables[0].shape[0]
    r = ROW_BLOCK
    d = D_MODEL
    whole = lambda i, *_: (0, 0, 0)
    grid_spec = pltpu.PrefetchScalarGridSpec(
        num_scalar_prefetch=len(tables),
        grid=(1,),
        in_specs=[
            pl.BlockSpec(memory_space=pl.ANY),
            pl.BlockSpec(memory_space=pl.ANY),
            pl.BlockSpec((N_EXPERTS, 1, 2 * d), whole),
            pl.BlockSpec(memory_space=pl.ANY),
            pl.BlockSpec((N_EXPERTS, 1, d), whole),
        ],
        out_specs=pl.BlockSpec(memory_space=pl.ANY),
        scratch_shapes=[
            pltpu.VMEM((2, r, ROW_W), F32),
            pltpu.VMEM((2, r, d // 2), U32),
            pltpu.VMEM((ROW_ALIGN * N_EXPERTS, d // 2), U32),
            pltpu.VMEM((d, 2 * d), F32),
            pltpu.VMEM((d, d), F32),
            pltpu.VMEM((d, 2 * d), BF16),
            pltpu.VMEM((d, d), BF16),
            pltpu.SemaphoreType.DMA((2,)),
            pltpu.SemaphoreType.DMA((2,)),
            pltpu.SemaphoreType.DMA((1,)),
            pltpu.SemaphoreType.DMA((2,)),
        ],
    )
    return pl.pallas_call(
        _expert_kernel,
        grid_spec=grid_spec,
        out_shape=jax.ShapeDtypeStruct((xs.shape[0], d // 2), U32),
        compiler_params=pltpu.CompilerParams(
            dimension_semantics=("arbitrary",), vmem_limit_bytes=VMEM_LIMIT_BYTES),
        name="experts",
    )(*tables, xs, w_gu, b_gu, w_down, b_down)


def _final_kernel(h1_ref, mod_ref, ys_ref, pos_ref, fg_ref, o_ref):
    t = h1_ref.shape[0]
    g2 = mod_ref[5:6, :]
    pos_rows = [pos_ref[k:k + 1, :] for k in range(TOP_K)]
    perm = jnp.concatenate(_perm_blocks(pos_rows, TILE_ROWS, t), axis=0)
    tn = (((0,), (0,)), ((), ()))
    y_lo, y_hi = _unpack_bf16_pairs(ys_ref[...])
    moe = jnp.concatenate([lax.dot_general(perm, y_lo, tn, preferred_element_type=F32),
                           lax.dot_general(perm, y_hi, tn, preferred_element_type=F32)], axis=1)
    o_ref[...] = _rms(h1_ref[...] + g2 * moe, fg_ref[...])


def _final_call(h1, mod3, ys, pos, final_g, bsz, seq):
    d = h1.shape[1]
    n_tok = bsz * seq
    t = SEQ_TILE
    per_batch = seq // t
    out = pl.pallas_call(
        _final_kernel,
        grid=(n_tok // t,),
        in_specs=[
            pl.BlockSpec((t, d), lambda i: (i, 0)),
            pl.BlockSpec((None, 6, d), lambda i: (i // per_batch, 0, 0)),
            pl.BlockSpec((TILE_ROWS, d // 2), lambda i: (i, 0)),
            pl.BlockSpec((TOP_K, t), lambda i: (0, i)),
            pl.BlockSpec((1, d), lambda i: (0, 0)),
        ],
        out_specs=pl.BlockSpec((t, d), lambda i: (i, 0)),
        out_shape=jax.ShapeDtypeStruct((n_tok, d), F32),
        compiler_params=pltpu.CompilerParams(
            dimension_semantics=("arbitrary",), vmem_limit_bytes=VMEM_LIMIT_BYTES),
        name="combine_final",
    )(h1, mod3, ys, pos, final_g.reshape(1, d))
    return out.reshape(bsz, seq, d)


def _block_diag(w):
    h, i, o = w.shape
    eye = jnp.eye(h, dtype=w.dtype)
    return (w[:, :, None, :] * eye[:, None, :, None]).reshape(h * i, h * o)


def _route_tables(cnt):
    r = ROW_BLOCK
    n_tiles = cnt.shape[0]
    nblk = n_tiles * TILE_ROWS // r + N_EXPERTS
    cnt = (cnt + (ROW_ALIGN - 1)) // ROW_ALIGN * ROW_ALIGN
    ie = jnp.arange(N_EXPERTS, dtype=I32)
    it = jnp.arange(n_tiles, dtype=I32)
    e_before = (ie[:, None] < ie[None, :]).astype(I32)
    t_before = (it[:, None] < it[None, :]).astype(I32)
    total = jnp.sum(cnt, axis=0)
    cum = jnp.sum(t_before[:, :, None] * cnt[:, None, :], axis=0)
    seg_off = jnp.sum(cnt[:, :, None] * e_before[None, :, :], axis=1)
    base = seg_off + it[:, None] * TILE_ROWS
    nblk_e = (total + r - 1) // r
    blk_start = jnp.sum(nblk_e[:, None] * e_before, axis=0)
    blk_end = blk_start + nblk_e
    nb_used = jnp.sum(nblk_e)
    blk = jnp.arange(nblk, dtype=I32)
    used = blk < nb_used
    last_e = jnp.max(jnp.where(nblk_e > 0, ie, 0))
    be = jnp.minimum(jnp.sum((blk[:, None] >= blk_end[None, :]).astype(I32), axis=1), N_EXPERTS - 1)
    be = jnp.where(used, be, last_e)
    be_onehot = (be[:, None] == ie[None, :]).astype(I32)
    pick = lambda v: jnp.sum(be_onehot * v[None, :], axis=1)
    jrow = jnp.where(used, (blk - pick(blk_start)) * r, 0)
    nval = jnp.where(used, jnp.clip(pick(total) - jrow, 0, r), 0)
    first = jnp.concatenate([jnp.ones((1,), I32), (be[1:] != be[:-1]).astype(I32)])
    nxt = pick(blk_end)
    next_e = jnp.where(nxt < nb_used, jnp.sum((nxt[:, None] >= blk_end[None, :]).astype(I32), axis=1), -1)
    run_start = jnp.sum(be_onehot[:, None, :] * cum[None, :, :], axis=-1)
    run_end = run_start + jnp.sum(be_onehot[:, None, :] * cnt[None, :, :], axis=-1)
    ilo = jnp.sum((run_end <= jrow[:, None]).astype(I32), axis=1)
    ihi = n_tiles - 1 - jnp.sum((run_start >= (jrow + r)[:, None]).astype(I32), axis=1)
    used_rows = jnp.sum(cnt, axis=1)
    tail_row = jnp.arange(n_tiles, dtype=I32) * TILE_ROWS + used_rows
    tail_len = TILE_ROWS - used_rows
    i32 = lambda v: v.astype(I32)
    return (i32(be), i32(first), i32(nval), i32(jrow), i32(jnp.minimum(ilo, n_tiles - 1)), i32(ihi), i32(next_e),
            i32(nb_used).reshape(1), i32(cnt).reshape(-1), i32(cum).reshape(-1), i32(base).reshape(-1),
            i32(tail_row), i32(tail_len))


def kernel(x, c, ada_w, ada_b, norm1_g, w_in, conv_w, conv_b, lru_wr, lru_br, lru_wi, lru_bi, lru_lambda, sgu_ln_g, sgu_ln_b, sgu_w, sgu_b, gnorm_lru_g, gnorm_sgu_g, w_out, norm2_g, router_w, router_b, exp_w_gu, exp_b_gu, exp_w_down, exp_b_down, final_g):
    bsz, seq, d = x.shape
    depth = ada_w.shape[0]
    assert depth == 1 and d == D_MODEL and seq % SEQ_TILE == 0 and ROW_BLOCK >= ROW_ALIGN * N_EXPERTS
    l = 0
    mod = _adaln_call(c, ada_w[l], ada_b[l])
    mod3 = mod.reshape(bsz, 6, d)

    row = lambda v: v.reshape(1, -1)
    hw = D_LRU // 2
    wr_bd, wi_bd = _block_diag(lru_wr[l]), _block_diag(lru_wi[l])
    wgate = jnp.stack([jnp.concatenate([wr_bd[s:s + hw, s:s + hw], wi_bd[s:s + hw, s:s + hw]], axis=1)
                       for s in (0, hw)]).astype(BF16)
    bgate = jnp.concatenate([lru_br[l], lru_bi[l]]).reshape(1, -1)
    sgub_full = jnp.repeat(sgu_b[l].T, SGU_HEAD_DIM, axis=1)
    h1, xs, pos, cnt = _mixer_call(
        x, mod3, row(norm1_g[l]), w_in[l].astype(BF16), conv_w[l], row(conv_b[l]), wgate, bgate,
        row(lru_lambda[l]), row(sgu_ln_g[l]), row(sgu_ln_b[l]), sgu_w[l], sgub_full,
        row(gnorm_lru_g[l]), row(gnorm_sgu_g[l]), w_out[l].astype(BF16), row(norm2_g[l]),
        router_w[l].T, router_b[l].reshape(-1, 1))

    tables = _route_tables(cnt[:, :, 0])
    ys = _expert_call(tables, xs, exp_w_gu[l], exp_b_gu[l].reshape(N_EXPERTS, 1, -1),
                      exp_w_down[l], exp_b_down[l].reshape(N_EXPERTS, 1, -1))
    return _final_call(h1, mod3, ys, pos, final_g, bsz, seq)
```

```python
import functools

import jax
import jax.numpy as jnp
from jax import lax
from jax.experimental import pallas as pl
from jax.experimental.pallas import tpu as pltpu

F32 = jnp.float32
BF16 = jnp.bfloat16
I32 = jnp.int32
U32 = jnp.uint32

D_MODEL = 1024
D_LRU = 512
D_SGU = 512
LRU_HEADS = 8
CONV_WIDTH = 4
LRU_C = 8.0
SGU_HEADS = 8
SGU_HEAD_DIM = D_SGU // SGU_HEADS
CHUNK = 128
N_EXPERTS = 32
TOP_K = 4
SWIGLU_ALPHA = 1.702
SWIGLU_LIMIT = 7.0
EPS = 1e-6

LANES = 128
SEQ_TILE = 512
ROW_ALIGN = 8
TILE_ROWS = TOP_K * SEQ_TILE + ROW_ALIGN * N_EXPERTS
PAYLOAD = LANES
ROW_W = D_MODEL + PAYLOAD
ROW_BLOCK = 512
ROW_PATHS = 4
VMEM_LIMIT_BYTES = 56 * 1024 * 1024


def _sigmoid(x):
    return 1.0 / (1.0 + jnp.exp(-x))


def _gelu_tanh(x):
    return 0.5 * x * (1.0 + jnp.tanh(0.7978845608028654 * (x + 0.044715 * (x * x * x))))


def _rms(x, g):
    ms = jnp.mean(x * x, axis=-1, keepdims=True)
    return x * lax.rsqrt(ms + EPS) * g


def _pack_bf16_pairs(v):
    n = v.shape[1] // 2
    r = v.astype(BF16).astype(F32)
    bits = lax.bitcast_convert_type(r, U32)
    return (bits[:, 0:n] >> 16) | (bits[:, n:] & jnp.uint32(0xFFFF0000))


def _unpack_bf16_pairs(w):
    lo = lax.bitcast_convert_type(w << 16, F32).astype(BF16)
    hi = lax.bitcast_convert_type(w & jnp.uint32(0xFFFF0000), F32).astype(BF16)
    return lo, hi


PERM_BLOCK = 256


def _perm_blocks(pos_rows, n_rows, n_cols):
    blk = PERM_BLOCK
    rid = lax.broadcasted_iota(I32, (blk, n_cols), 0).astype(F32).astype(BF16)
    one = jnp.ones((), BF16)
    blocks = []
    for q in range(n_rows // blk):
        acc = jnp.zeros((blk, n_cols), BF16)
        for pk in pos_rows:
            in_blk = jnp.where(pk // blk == q, pk % blk, -1).astype(F32).astype(BF16)
            acc = jnp.where(rid == in_blk, one, acc)
        blocks.append(acc)
    return blocks


def _adaln_kernel(c_ref, w_ref, b_ref, o_ref):
    c = c_ref[...]
    ca = c * _sigmoid(c)
    o_ref[...] = jnp.dot(ca.astype(BF16), w_ref[...].astype(BF16), preferred_element_type=F32) + b_ref[...]


def _adaln_call(c, w, b):
    bsz, d = c.shape
    n_out = w.shape[1]
    return pl.pallas_call(
        _adaln_kernel,
        grid=(n_out // d,),
        in_specs=[
            pl.BlockSpec((bsz, d), lambda j: (0, 0)),
            pl.BlockSpec((d, d), lambda j: (0, j)),
            pl.BlockSpec((1, d), lambda j: (0, j)),
        ],
        out_specs=pl.BlockSpec((bsz, d), lambda j: (0, j)),
        out_shape=jax.ShapeDtypeStruct((bsz, n_out), F32),
        name="adaln",
    )(c, w, b.reshape(1, n_out))


SCAN_SEGMENTS = 8
SCAN_PITCH_PAD = 8


def _scan_pitch(t):
    return t // SCAN_SEGMENTS + SCAN_PITCH_PAD


def _linear_scan(a, b, h0, a_buf, b_buf):
    t, c = a.shape
    nseg = SCAN_SEGMENTS
    seg = t // nseg
    pitch = _scan_pitch(t)
    nlb = c // LANES
    for j in range(nlb):
        for s in range(nseg):
            a_buf[j, s * pitch:s * pitch + seg, :] = a[s * seg:(s + 1) * seg, j * LANES:(j + 1) * LANES]
            b_buf[j, s * pitch:s * pitch + seg, :] = b[s * seg:(s + 1) * seg, j * LANES:(j + 1) * LANES]

    def step(g, carry):
        hs, ps = carry
        new_h, new_p = [], []
        for j in range(nlb):
            view = (j, pl.ds(g, nseg, stride=pitch), slice(None))
            ag = a_buf[view]
            hj = ag * hs[j] + b_buf[view]
            pj = ag * ps[j]
            b_buf[view] = hj
            a_buf[view] = pj
            new_h.append(hj)
            new_p.append(pj)
        return tuple(new_h), tuple(new_p)

    init = (tuple(jnp.zeros((nseg, LANES), F32) for _ in range(nlb)),
            tuple(jnp.ones((nseg, LANES), F32) for _ in range(nlb)))
    h_end, p_end = lax.fori_loop(0, seg, step, init, unroll=True)
    h_end = jnp.concatenate(h_end, axis=1)
    p_end = jnp.concatenate(p_end, axis=1)
    state = h0
    out = []
    for s in range(nseg):
        h_loc = jnp.concatenate([b_buf[j, s * pitch:s * pitch + seg, :] for j in range(nlb)], axis=1)
        p_loc = jnp.concatenate([a_buf[j, s * pitch:s * pitch + seg, :] for j in range(nlb)], axis=1)
        out.append(h_loc + p_loc * state)
        state = h_end[s:s + 1] + p_end[s:s + 1] * state
    return jnp.concatenate(out, axis=0), state


def _mixer_kernel(x_ref, mod_ref, modp_ref, n1g_ref, win_ref, convw_ref, convb_ref, wgate_ref, bgate_ref, lam_ref,
                  lng_ref, lnb_ref, sguw_ref, sgub_ref, gl_ref, gs_ref, wout_ref, n2g_ref, rwt_ref, rb_ref,
                  h1_ref, xs_ref, pos_ref, cnt_ref,
                  xa_tail, h_carry, scan_a, scan_b, h1_prev, *, tiles_per_seq):
    t = x_ref.shape[0]
    i = pl.program_id(0)

    @pl.when(i == 0)
    def _():
        h1_prev[...] = jnp.zeros_like(h1_prev)

    @pl.when(i % tiles_per_seq == 0)
    def _():
        xa_tail[...] = jnp.zeros_like(xa_tail)
        h_carry[...] = jnp.zeros_like(h_carry)

    z2b, logits = _router_logits(h1_prev[...], modp_ref[...], n2g_ref, rwt_ref, rb_ref)
    pos_rows, gate_pieces = _route(logits, pos_ref, cnt_ref)

    mod = mod_ref[...]
    sh1, sc1, g1 = mod[0:1], mod[1:2], mod[2:3]

    x = x_ref[...]
    z = _rms(x, n1g_ref[...] * (1.0 + sc1)) + sh1
    proj = jnp.dot(z.astype(BF16), win_ref[...], preferred_element_type=F32)
    perm_blocks = _perm_blocks(pos_rows, TILE_ROWS, t)
    n_pb = len(perm_blocks)
    sorted_blocks = iter(range(n_pb))

    def sort_next():
        q = next(sorted_blocks)
        _sort_rows(perm_blocks, z2b, gate_pieces, xs_ref, q, q + 1)

    sort_next()
    xa = proj[:, 0:D_LRU]
    ya = proj[:, D_LRU:2 * D_LRU]
    u = proj[:, 2 * D_LRU:2 * D_LRU + D_SGU]
    v = proj[:, 2 * D_LRU + D_SGU:]

    tail = xa_tail[...]
    row8 = lax.broadcasted_iota(I32, (8, 1), 0)
    xc = xa * convw_ref[CONV_WIDTH - 1:CONV_WIDTH, :] + convb_ref[...]
    for sft in range(1, CONV_WIDTH):
        rolled = pltpu.roll(xa, sft, 0)
        head = jnp.where(row8 < sft, pltpu.roll(tail, sft, 0), rolled[0:8])
        shifted = jnp.concatenate([head, rolled[8:]], axis=0)
        xc = xc + shifted * convw_ref[CONV_WIDTH - 1 - sft:CONV_WIDTH - sft, :]
    xa_tail[...] = xa[t - 8:t]
    sort_next()

    xcb = xc.astype(BF16)
    hw = D_LRU // 2
    g_lo = jnp.dot(xcb[:, 0:hw], wgate_ref[0], preferred_element_type=F32)
    g_hi = jnp.dot(xcb[:, hw:], wgate_ref[1], preferred_element_type=F32)
    sort_next()
    r_gate = _sigmoid(jnp.concatenate([g_lo[:, 0:hw], g_hi[:, 0:hw]], axis=1) + bgate_ref[:, 0:D_LRU])
    i_gate = _sigmoid(jnp.concatenate([g_lo[:, hw:], g_hi[:, hw:]], axis=1) + bgate_ref[:, D_LRU:])
    sort_next()
    nlam = -lam_ref[...]
    softplus = jnp.maximum(nlam, 0.0) + jnp.log1p(jnp.exp(-jnp.abs(nlam)))
    log_a = (-LRU_C) * r_gate * softplus
    a = jnp.exp(log_a)
    sort_next()
    om = -jnp.tanh(log_a) * (a * a + 1.0)
    mult = jnp.where(om > 0.0, om * lax.rsqrt(om), 0.0)
    bterm = mult * i_gate * xc
    sort_next()
    h, h_last = _linear_scan(a, bterm, h_carry[...], scan_a, scan_b)
    h_carry[...] = h_last
    sort_next()
    o_lru = _rms(h * _gelu_tanh(ya), gl_ref[...])
    sort_next()

    ug = _gelu_tanh(u)
    vg = _gelu_tanh(v)
    sort_next()
    assert next(sorted_blocks, None) is None
    mu = jnp.mean(vg, axis=-1, keepdims=True)
    vcen = vg - mu
    var = jnp.mean(vcen * vcen, axis=-1, keepdims=True)
    vn = (vcen * lax.rsqrt(var + EPS) * lng_ref[...] + lnb_ref[...]).astype(BF16)
    ri = lax.broadcasted_iota(I32, (CHUNK, CHUNK), 0)
    ci = lax.broadcasted_iota(I32, (CHUNK, CHUNK), 1)
    causal = ri >= ci
    lane = lax.broadcasted_iota(I32, (1, 2 * SGU_HEAD_DIM), 1)
    first_half = lane < SGU_HEAD_DIM
    pair_w = []
    for p in range(SGU_HEADS // 2):
        w0 = jnp.where(causal, sguw_ref[2 * p], 0.0).astype(BF16)
        w1 = jnp.where(causal, sguw_ref[2 * p + 1], 0.0).astype(BF16)
        pair_w.append(jnp.concatenate([w0, w1], axis=1))
    chunks = []
    zero = jnp.zeros((), BF16)
    for n in range(t // CHUNK):
        cols = []
        for p in range(SGU_HEADS // 2):
            blk = vn[n * CHUNK:(n + 1) * CHUNK, p * LANES:(p + 1) * LANES]
            rhs = jnp.concatenate([jnp.where(first_half, blk, zero), jnp.where(first_half, zero, blk)], axis=0)
            cols.append(jnp.dot(pair_w[p], rhs, preferred_element_type=F32))
        chunks.append(jnp.concatenate(cols, axis=1) + sgub_ref[...])
    mixed = jnp.concatenate(chunks, axis=0)
    o_sgu = _rms(ug * mixed, gs_ref[...])

    heads = jnp.concatenate([o_lru, o_sgu], axis=1).astype(BF16)
    h1 = x + g1 * jnp.dot(heads, wout_ref[...], preferred_element_type=F32)
    h1_ref[...] = h1
    h1_prev[...] = h1


def _router_logits(h1, mod, n2g_ref, rwt_ref, rb_ref):
    sh2, sc2 = mod[3:4], mod[4:5]
    z2b = (_rms(h1, n2g_ref[...] * (1.0 + sc2)) + sh2).astype(BF16)
    logits = lax.dot_general(rwt_ref[...].astype(BF16), z2b, (((1,), (1,)), ((), ())),
                             preferred_element_type=F32) + rb_ref[...]
    return z2b, logits


def _route(logits, pos_ref, cnt_ref):
    t = logits.shape[1]
    eidx = lax.broadcasted_iota(I32, (N_EXPERTS, t), 0)
    work = logits
    sel = []
    tops = []
    for k in range(TOP_K):
        m = jnp.max(work, axis=0, keepdims=True)
        idx = jnp.min(jnp.where(work == m, eidx, N_EXPERTS), axis=0, keepdims=True)
        onehot = eidx == idx
        sel.append(onehot)
        tops.append(m)
        work = jnp.where(onehot, -jnp.inf, work)
    exps = [jnp.exp(tk - tops[0]) for tk in tops]
    denom = exps[0] + exps[1] + exps[2] + exps[3]
    chosen = jnp.zeros((N_EXPERTS, t), F32)
    gsel = jnp.zeros((N_EXPERTS, t), F32)
    for k in range(TOP_K):
        chosen = jnp.where(sel[k], 1.0, chosen)
        gsel = jnp.where(sel[k], exps[k] / denom, gsel)

    chosen_b = chosen.astype(BF16)
    si = lax.broadcasted_iota(I32, (t, t), 0)
    ti = lax.broadcasted_iota(I32, (t, t), 1)
    before = jnp.where(si < ti, 1.0, 0.0).astype(BF16)
    excl = jnp.dot(chosen_b, before, preferred_element_type=F32)
    cnt_col = jnp.sum(chosen, axis=1, keepdims=True).astype(I32)
    cnt_ref[...] = jnp.broadcast_to(cnt_col, cnt_ref.shape)
    run_len = ((cnt_col + (ROW_ALIGN - 1)) // ROW_ALIGN * ROW_ALIGN).astype(F32)
    er = lax.broadcasted_iota(I32, (N_EXPERTS, N_EXPERTS), 0)
    ec = lax.broadcasted_iota(I32, (N_EXPERTS, N_EXPERTS), 1)
    lower = jnp.where(ec < er, 1.0, 0.0).astype(BF16)
    run_start = jnp.dot(lower, jnp.broadcast_to(run_len, (N_EXPERTS, LANES)).astype(BF16),
                        preferred_element_type=F32)[:, 0:1]
    posmat = excl + run_start
    pos_rows = []
    for k in range(TOP_K):
        pk = jnp.sum(jnp.where(sel[k], posmat, 0.0), axis=0, keepdims=True).astype(I32)
        pos_ref[k:k + 1, :] = pk
        pos_rows.append(pk)

    g_hi = gsel.astype(BF16)
    rem = gsel - g_hi.astype(F32)
    g_mid = rem.astype(BF16)
    g_lo = (rem - g_mid.astype(F32)).astype(BF16)
    gp = jnp.concatenate([g_hi, g_mid, g_lo, jnp.zeros((PAYLOAD - 3 * N_EXPERTS, t), BF16)], axis=0)
    return pos_rows, gp


def _sort_rows(perm_blocks, z2b, gate_pieces, xs_ref, q0, q1):
    for q in range(q0, q1):
        rows = slice(q * PERM_BLOCK, (q + 1) * PERM_BLOCK)
        xs_ref[rows, 0:D_MODEL] = jnp.dot(perm_blocks[q], z2b, preferred_element_type=F32)
        xs_ref[rows, D_MODEL:] = lax.dot_general(perm_blocks[q], gate_pieces, (((1,), (1,)), ((), ())),
                                                 preferred_element_type=F32)


def _mixer_call(x, mod3, n1g, win, convw, convb, wgate, bgate, lam, lng, lnb, sguw, sgub_full, gl, gs, wout,
                n2g, rwt, rb):
    bsz, seq, d = x.shape
    t = SEQ_TILE
    tiles = seq // t
    n_tiles = bsz * tiles
    x2 = x.reshape(bsz * seq, d)

    def const(shape):
        return pl.BlockSpec(shape, lambda i: (0,) * len(shape))

    mixed = lambda i: jnp.minimum(i, n_tiles - 1)
    routed = lambda i: jnp.maximum(i - 1, 0)
    return pl.pallas_call(
        functools.partial(_mixer_kernel, tiles_per_seq=tiles),
        grid=(n_tiles + 1,),
        in_specs=[
            pl.BlockSpec((t, d), lambda i: (mixed(i), 0)),
            pl.BlockSpec((None, 6, d), lambda i: (mixed(i) // tiles, 0, 0)),
            pl.BlockSpec((None, 6, d), lambda i: (routed(i) // tiles, 0, 0)),
            const((1, d)),
            const((d, 2 * d)),
            const((CONV_WIDTH, D_LRU)),
            const((1, D_LRU)),
            const((2, D_LRU // 2, D_LRU)),
            const((1, 2 * D_LRU)),
            const((1, D_LRU)),
            const((1, D_SGU)),
            const((1, D_SGU)),
            const((SGU_HEADS, CHUNK, CHUNK)),
            const((CHUNK, D_SGU)),
            const((1, D_LRU)),
            const((1, D_SGU)),
            const((d, d)),
            const((1, d)),
            const((N_EXPERTS, d)),
            const((N_EXPERTS, 1)),
        ],
        out_specs=[
            pl.BlockSpec((t, d), lambda i: (i, 0)),
            pl.BlockSpec((TILE_ROWS, ROW_W), lambda i: (routed(i), 0)),
            pl.BlockSpec((TOP_K, t), lambda i: (0, routed(i))),
            pl.BlockSpec((None, N_EXPERTS, LANES), lambda i: (routed(i), 0, 0)),
        ],
        out_shape=[
            jax.ShapeDtypeStruct(((n_tiles + 1) * t, d), F32),
            jax.ShapeDtypeStruct((n_tiles * TILE_ROWS, ROW_W), F32),
            jax.ShapeDtypeStruct((TOP_K, n_tiles * t), I32),
            jax.ShapeDtypeStruct((n_tiles, N_EXPERTS, LANES), I32),
        ],
        scratch_shapes=[pltpu.VMEM((8, D_LRU), F32), pltpu.VMEM((1, D_LRU), F32),
                        pltpu.VMEM((D_LRU // LANES, SCAN_SEGMENTS * _scan_pitch(t), LANES), F32),
                        pltpu.VMEM((D_LRU // LANES, SCAN_SEGMENTS * _scan_pitch(t), LANES), F32),
                        pltpu.VMEM((t, d), F32)],
        compiler_params=pltpu.CompilerParams(
            dimension_semantics=("arbitrary",), vmem_limit_bytes=VMEM_LIMIT_BYTES),
        name="mixer_router",
    )(x2, mod3, mod3, n1g, win, convw, convb, wgate, bgate, lam, lng, lnb, sguw, sgub_full, gl, gs, wout, n2g, rwt,
      rb)


def _expert_kernel(be_ref, first_ref, nval_ref, jrow_ref, ilo_ref, ihi_ref, nexte_ref, nb_ref,
                   cnt_ref, cum_ref, base_ref, tailrow_ref, taillen_ref,
                   xs_hbm, wgu_hbm, bgu_ref, wd_hbm, bd_ref,
                   ys_hbm,
                   xbuf, ybuf, zbuf, wgu_st, wd_st, wgu_bf, wd_bf, gsem, ssem, zsem, wsem):
    r = ROW_BLOCK
    n_tiles = tailrow_ref.shape[0]
    nb = nb_ref[0]

    rows = lambda v: pl.multiple_of(v, ROW_ALIGN)

    def weight_copies(e):
        return (pltpu.make_async_copy(wgu_hbm.at[e], wgu_st, wsem.at[0]),
                pltpu.make_async_copy(wd_hbm.at[e], wd_st, wsem.at[1]))

    def tail_copy(i):
        n = rows(taillen_ref[i])
        return pltpu.make_async_copy(zbuf.at[pl.ds(0, n), :], ys_hbm.at[pl.ds(rows(tailrow_ref[i]), n), :],
                                     zsem.at[0])

    def run_of(blk, i):
        idx = i * N_EXPERTS + be_ref[blk]
        cu = cum_ref[idx]
        lo = jnp.maximum(cu, jrow_ref[blk])
        hi = jnp.minimum(cu + cnt_ref[idx], jrow_ref[blk] + r)
        return base_ref[idx] + (lo - cu), lo - jrow_ref[blk], hi - lo

    def for_each_run(blk, fn):
        def tile_body(i, c):
            src, dst, n = run_of(blk, i)

            @pl.when(n > 0)
            def _():
                fn(rows(src), rows(dst), rows(n))
            return c
        lax.fori_loop(ilo_ref[blk], ihi_ref[blk] + 1, tile_body, 0)

    def start_gather(dst_slot):
        def fn(hbm_row, buf_row, n):
            pltpu.make_async_copy(xs_hbm.at[pl.ds(hbm_row, n), :],
                                  xbuf.at[dst_slot, pl.ds(buf_row, n), :], gsem.at[dst_slot]).start()
        return fn

    def start_scatter(src_slot):
        def fn(hbm_row, buf_row, n):
            pltpu.make_async_copy(ybuf.at[src_slot, pl.ds(buf_row, n), :],
                                  ys_hbm.at[pl.ds(hbm_row, n), :], ssem.at[src_slot]).start()
        return fn

    def wait_gather(s, n):
        pltpu.make_async_copy(xs_hbm.at[pl.ds(0, rows(n)), :], xbuf.at[s, pl.ds(0, rows(n)), :], gsem.at[s]).wait()

    def wait_scatter(s, n):
        pltpu.make_async_copy(ybuf.at[s, pl.ds(0, rows(n)), :], ys_hbm.at[pl.ds(0, rows(n)), :], ssem.at[s]).wait()

    for c in weight_copies(be_ref[0]):
        c.start()
    xbuf[...] = jnp.zeros_like(xbuf)
    for_each_run(0, start_gather(0))
    zbuf[...] = jnp.zeros_like(zbuf)

    def start_tail(i, c):
        tail_copy(i).start()
        return c
    lax.fori_loop(0, n_tiles, start_tail, 0)

    def block(b, carry):
        slot = b & 1
        wait_gather(slot, nval_ref[b])

        @pl.when(b + 1 < nb)
        def _():
            for_each_run(b + 1, start_gather(1 - slot))

        @pl.when(first_ref[b] == 1)
        def _():
            for c in weight_copies(be_ref[b]):
                c.wait()
            wgu_bf[...] = wgu_st[...].astype(BF16)
            wd_bf[...] = wd_st[...].astype(BF16)

            @pl.when(nexte_ref[b] >= 0)
            def _():
                for c in weight_copies(nexte_ref[b]):
                    c.start()

        @pl.when(b >= 2)
        def _():
            wait_scatter(slot, nval_ref[jnp.maximum(b - 2, 0)])

        def expert_rows(m):
            xw = xbuf[slot, 0:m]
            xb = xw[:, 0:D_MODEL].astype(BF16)
            lane = lax.broadcasted_iota(I32, (1, PAYLOAD), 1)
            gate = jnp.sum(jnp.where((lane & (N_EXPERTS - 1)) == be_ref[b], xw[:, D_MODEL:], 0.0),
                           axis=1, keepdims=True)
            gu = jnp.dot(xb, wgu_bf[...], preferred_element_type=F32) + bgu_ref[be_ref[b]]
            g = jnp.minimum(gu[:, 0:D_MODEL], SWIGLU_LIMIT)
            lin = jnp.clip(gu[:, D_MODEL:], -SWIGLU_LIMIT, SWIGLU_LIMIT)
            act = g * _sigmoid(SWIGLU_ALPHA * g) * (lin + 1.0)
            y = jnp.dot(act.astype(BF16), wd_bf[...], preferred_element_type=F32) + bd_ref[be_ref[b]]
            ybuf[slot, 0:m] = _pack_bf16_pairs(y * gate)

        quarter = r // ROW_PATHS
        for k in range(1, ROW_PATHS + 1):
            @pl.when((nval_ref[b] > (k - 1) * quarter) & (nval_ref[b] <= k * quarter))
            def _():
                expert_rows(k * quarter)

        for_each_run(b, start_scatter(slot))
        return carry

    lax.fori_loop(0, nb, block, 0)

    last = nb - 1
    wait_scatter(last & 1, nval_ref[last])

    @pl.when(nb >= 2)
    def _():
        wait_scatter(1 - (last & 1), nval_ref[jnp.maximum(last - 1, 0)])

    def wait_tail(i, c):
        tail_copy(i).wait()
        return c
    lax.fori_loop(0, n_tiles, wait_tail, 0)


def _expert_call(tables, xs, w_gu, b_gu, w_down, b_down):
    nblk = tables[0].shape[0]
    r = ROW_BLOCK
    d = D_MODEL
    whole = lambda i, *_: (0, 0, 0)
    grid_spec = pltpu.PrefetchScalarGridSpec(
        num_scalar_prefetch=len(tables),
        grid=(1,),
        in_specs=[
            pl.BlockSpec(memory_space=pl.ANY),
            pl.BlockSpec(memory_space=pl.ANY),
            pl.BlockSpec((N_EXPERTS, 1, 2 * d), whole),
            pl.BlockSpec(memory_space=pl.ANY),
            pl.BlockSpec((N_EXPERTS, 1, d), whole),
        ],
        out_specs=pl.BlockSpec(memory_space=pl.ANY),
        scratch_shapes=[
            pltpu.VMEM((2, r, ROW_W), F32),
            pltpu.VMEM((2, r, d // 2), U32),
            pltpu.VMEM((ROW_ALIGN * N_EXPERTS, d // 2), U32),
            pltpu.VMEM((d, 2 * d), F32),
            pltpu.VMEM((d, d), F32),
            pltpu.VMEM((d, 2 * d), BF16),
            pltpu.VMEM((d, d), BF16),
            pltpu.SemaphoreType.DMA((2,)),
            pltpu.SemaphoreType.DMA((2,)),
            pltpu.SemaphoreType.DMA((1,)),
            pltpu.SemaphoreType.DMA((2,)),
        ],
    )
    return pl.pallas_call(
        _expert_kernel,
        grid_spec=grid_spec,
        out_shape=jax.ShapeDtypeStruct((xs.shape[0], d // 2), U32),
        compiler_params=pltpu.CompilerParams(
            dimension_semantics=("arbitrary",), vmem_limit_bytes=VMEM_LIMIT_BYTES),
        name="experts",
    )(*tables, xs, w_gu, b_gu, w_down, b_down)


def _final_kernel(h1_ref, mod_ref, ys_ref, pos_ref, fg_ref, o_ref):
    t = h1_ref.shape[0]
    g2 = mod_ref[5:6, :]
    pos_rows = [pos_ref[k:k + 1, :] for k in range(TOP_K)]
    perm = jnp.concatenate(_perm_blocks(pos_rows, TILE_ROWS, t), axis=0)
    tn = (((0,), (0,)), ((), ()))
    y_lo, y_hi = _unpack_bf16_pairs(ys_ref[...])
    moe = jnp.concatenate([lax.dot_general(perm, y_lo, tn, preferred_element_type=F32),
                           lax.dot_general(perm, y_hi, tn, preferred_element_type=F32)], axis=1)
    o_ref[...] = _rms(h1_ref[...] + g2 * moe, fg_ref[...])


def _final_call(h1, mod3, ys, pos, final_g, bsz, seq):
    d = h1.shape[1]
    n_tok = bsz * seq
    t = SEQ_TILE
    per_batch = seq // t
    out = pl.pallas_call(
        _final_kernel,
        grid=(n_tok // t,),
        in_specs=[
            pl.BlockSpec((t, d), lambda i: (i, 0)),
            pl.BlockSpec((None, 6, d), lambda i: (i // per_batch, 0, 0)),
            pl.BlockSpec((TILE_ROWS, d // 2), lambda i: (i, 0)),
            pl.BlockSpec((TOP_K, t), lambda i: (0, i)),
            pl.BlockSpec((1, d), lambda i: (0, 0)),
        ],
        out_specs=pl.BlockSpec((t, d), lambda i: (i, 0)),
        out_shape=jax.ShapeDtypeStruct((n_tok, d), F32),
        compiler_params=pltpu.CompilerParams(
            dimension_semantics=("arbitrary",), vmem_limit_bytes=VMEM_LIMIT_BYTES),
        name="combine_final",
    )(h1, mod3, ys, pos, final_g.reshape(1, d))
    return out.reshape(bsz, seq, d)


def _block_diag(w):
    h, i, o = w.shape
    eye = jnp.eye(h, dtype=w.dtype)
    return (w[:, :, None, :] * eye[:, None, :, None]).reshape(h * i, h * o)


def _route_tables(cnt):
    r = ROW_BLOCK
    n_tiles = cnt.shape[0]
    nblk = n_tiles * TILE_ROWS // r + N_EXPERTS
    cnt = (cnt + (ROW_ALIGN - 1)) // ROW_ALIGN * ROW_ALIGN
    ie = jnp.arange(N_EXPERTS, dtype=I32)
    it = jnp.arange(n_tiles, dtype=I32)
    e_before = (ie[:, None] < ie[None, :]).astype(I32)
    t_before = (it[:, None] < it[None, :]).astype(I32)
    total = jnp.sum(cnt, axis=0)
    cum = jnp.sum(t_before[:, :, None] * cnt[:, None, :], axis=0)
    seg_off = jnp.sum(cnt[:, :, None] * e_before[None, :, :], axis=1)
    base = seg_off + it[:, None] * TILE_ROWS
    nblk_e = (total + r - 1) // r
    blk_start = jnp.sum(nblk_e[:, None] * e_before, axis=0)
    blk_end = blk_start + nblk_e
    nb_used = jnp.sum(nblk_e)
    blk = jnp.arange(nblk, dtype=I32)
    used = blk < nb_used
    last_e = jnp.max(jnp.where(nblk_e > 0, ie, 0))
    be = jnp.minimum(jnp.sum((blk[:, None] >= blk_end[None, :]).astype(I32), axis=1), N_EXPERTS - 1)
    be = jnp.where(used, be, last_e)
    be_onehot = (be[:, None] == ie[None, :]).astype(I32)
    pick = lambda v: jnp.sum(be_onehot * v[None, :], axis=1)
    jrow = jnp.where(used, (blk - pick(blk_start)) * r, 0)
    nval = jnp.where(used, jnp.clip(pick(total) - jrow, 0, r), 0)
    first = jnp.concatenate([jnp.ones((1,), I32), (be[1:] != be[:-1]).astype(I32)])
    nxt = pick(blk_end)
    next_e = jnp.where(nxt < nb_used, jnp.sum((nxt[:, None] >= blk_end[None, :]).astype(I32), axis=1), -1)
    run_start = jnp.sum(be_onehot[:, None, :] * cum[None, :, :], axis=-1)
    run_end = run_start + jnp.sum(be_onehot[:, None, :] * cnt[None, :, :], axis=-1)
    ilo = jnp.sum((run_end <= jrow[:, None]).astype(I32), axis=1)
    ihi = n_tiles - 1 - jnp.sum((run_start >= (jrow + r)[:, None]).astype(I32), axis=1)
    used_rows = jnp.sum(cnt, axis=1)
    tail_row = jnp.arange(n_tiles, dtype=I32) * TILE_ROWS + used_rows
    tail_len = TILE_ROWS - used_rows
    i32 = lambda v: v.astype(I32)
    return (i32(be), i32(first), i32(nval), i32(jrow), i32(jnp.minimum(ilo, n_tiles - 1)), i32(ihi), i32(next_e),
            i32(nb_used).reshape(1), i32(cnt).reshape(-1), i32(cum).reshape(-1), i32(base).reshape(-1),
            i32(tail_row), i32(tail_len))


def kernel(x, c, ada_w, ada_b, norm1_g, w_in, conv_w, conv_b, lru_wr, lru_br, lru_wi, lru_bi, lru_lambda, sgu_ln_g, sgu_ln_b, sgu_w, sgu_b, gnorm_lru_g, gnorm_sgu_g, w_out, norm2_g, router_w, router_b, exp_w_gu, exp_b_gu, exp_w_down, exp_b_down, final_g):
    bsz, seq, d = x.shape
    depth = ada_w.shape[0]
    assert depth == 1 and d == D_MODEL and seq % SEQ_TILE == 0 and ROW_BLOCK >= ROW_ALIGN * N_EXPERTS
    l = 0
    mod = _adaln_call(c, ada_w[l], ada_b[l])
    mod3 = mod.reshape(bsz, 6, d)

    row = lambda v: v.reshape(1, -1)
    hw = D_LRU // 2
    wr_bd, wi_bd = _block_diag(lru_wr[l]), _block_diag(lru_wi[l])
    wgate = jnp.stack([jnp.concatenate([wr_bd[s:s + hw, s:s + hw], wi_bd[s:s + hw, s:s + hw]], axis=1)
                       for s in (0, hw)]).astype(BF16)
    bgate = jnp.concatenate([lru_br[l], lru_bi[l]]).reshape(1, -1)
    sgub_full = jnp.repeat(sgu_b[l].T, SGU_HEAD_DIM, axis=1)
    h1, xs, pos, cnt = _mixer_call(
        x, mod3, row(norm1_g[l]), w_in[l].astype(BF16), conv_w[l], row(conv_b[l]), wgate, bgate,
        row(lru_lambda[l]), row(sgu_ln_g[l]), row(sgu_ln_b[l]), sgu_w[l], sgub_full,
        row(gnorm_lru_g[l]), row(gnorm_sgu_g[l]), w_out[l].astype(BF16), row(norm2_g[l]),
        router_w[l].T, router_b[l].reshape(-1, 1))

    tables = _route_tables(cnt[:, :, 0])
    ys = _expert_call(tables, xs, exp_w_gu[l], exp_b_gu[l].reshape(N_EXPERTS, 1, -1),
                      exp_w_down[l], exp_b_down[l].reshape(N_EXPERTS, 1, -1))
    return _final_call(h1, mod3, ys, pos, final_g, bsz, seq)
```

```python
import functools

import jax
import jax.numpy as jnp
from jax import lax
from jax.experimental import pallas as pl
from jax.experimental.pallas import tpu as pltpu

F32 = jnp.float32
BF16 = jnp.bfloat16
I32 = jnp.int32
U32 = jnp.uint32

D_MODEL = 1024
D_LRU = 512
D_SGU = 512
LRU_HEADS = 8
CONV_WIDTH = 4
LRU_C = 8.0
SGU_HEADS = 8
SGU_HEAD_DIM = D_SGU // SGU_HEADS
CHUNK = 128
N_EXPERTS = 32
TOP_K = 4
SWIGLU_ALPHA = 1.702
SWIGLU_LIMIT = 7.0
EPS = 1e-6

LANES = 128
SEQ_TILE = 512
ROW_ALIGN = 8
TILE_ROWS = TOP_K * SEQ_TILE + ROW_ALIGN * N_EXPERTS
PAYLOAD = LANES
ROW_W = D_MODEL + PAYLOAD
ROW_BLOCK = 768
ROW_PATHS = 4
V7X_VMEM_BYTES = 64 * 1024 * 1024
VMEM_LIMIT_BYTES = V7X_VMEM_BYTES * 7 // 8


def _sigmoid(x):
    return 1.0 / (1.0 + jnp.exp(-x))


def _gelu_tanh(x):
    return 0.5 * x * (1.0 + jnp.tanh(0.7978845608028654 * (x + 0.044715 * (x * x * x))))


def _rms(x, g):
    ms = jnp.mean(x * x, axis=-1, keepdims=True)
    return x * lax.rsqrt(ms + EPS) * g


def _pack_bf16_pairs(v):
    n = v.shape[1] // 2
    r = v.astype(BF16).astype(F32)
    bits = lax.bitcast_convert_type(r, U32)
    return (bits[:, 0:n] >> 16) | (bits[:, n:] & jnp.uint32(0xFFFF0000))


def _unpack_bf16_pairs(w):
    lo = lax.bitcast_convert_type(w << 16, F32).astype(BF16)
    hi = lax.bitcast_convert_type(w & jnp.uint32(0xFFFF0000), F32).astype(BF16)
    return lo, hi


PERM_BLOCK = 256


def _perm_blocks(pos_rows, n_rows, n_cols):
    blk = PERM_BLOCK
    rid = lax.broadcasted_iota(I32, (blk, n_cols), 0).astype(F32).astype(BF16)
    one = jnp.ones((), BF16)
    blocks = []
    for q in range(n_rows // blk):
        acc = jnp.zeros((blk, n_cols), BF16)
        for pk in pos_rows:
            in_blk = jnp.where(pk // blk == q, pk % blk, -1).astype(F32).astype(BF16)
            acc = jnp.where(rid == in_blk, one, acc)
        blocks.append(acc)
    return blocks


def _adaln_kernel(c_ref, w_ref, b_ref, o_ref):
    c = c_ref[...]
    ca = c * _sigmoid(c)
    o_ref[...] = jnp.dot(ca.astype(BF16), w_ref[...].astype(BF16), preferred_element_type=F32) + b_ref[...]


def _adaln_call(c, w, b):
    bsz, d = c.shape
    n_out = w.shape[1]
    return pl.pallas_call(
        _adaln_kernel,
        grid=(n_out // d,),
        in_specs=[
            pl.BlockSpec((bsz, d), lambda j: (0, 0)),
            pl.BlockSpec((d, d), lambda j: (0, j)),
            pl.BlockSpec((1, d), lambda j: (0, j)),
        ],
        out_specs=pl.BlockSpec((bsz, d), lambda j: (0, j)),
        out_shape=jax.ShapeDtypeStruct((bsz, n_out), F32),
        name="adaln",
    )(c, w, b.reshape(1, n_out))


SCAN_SEGMENTS = 8
SCAN_PITCH_PAD = 8


def _scan_pitch(t):
    return t // SCAN_SEGMENTS + SCAN_PITCH_PAD


def _linear_scan(a, b, h0, a_buf, b_buf):
    t, c = a.shape
    nseg = SCAN_SEGMENTS
    seg = t // nseg
    pitch = _scan_pitch(t)
    nlb = c // LANES
    for j in range(nlb):
        for s in range(nseg):
            a_buf[j, s * pitch:s * pitch + seg, :] = a[s * seg:(s + 1) * seg, j * LANES:(j + 1) * LANES]
            b_buf[j, s * pitch:s * pitch + seg, :] = b[s * seg:(s + 1) * seg, j * LANES:(j + 1) * LANES]

    def step(g, carry):
        hs, ps = carry
        new_h, new_p = [], []
        for j in range(nlb):
            view = (j, pl.ds(g, nseg, stride=pitch), slice(None))
            ag = a_buf[view]
            hj = ag * hs[j] + b_buf[view]
            pj = ag * ps[j]
            b_buf[view] = hj
            a_buf[view] = pj
            new_h.append(hj)
            new_p.append(pj)
        return tuple(new_h), tuple(new_p)

    init = (tuple(jnp.zeros((nseg, LANES), F32) for _ in range(nlb)),
            tuple(jnp.ones((nseg, LANES), F32) for _ in range(nlb)))
    h_end, p_end = lax.fori_loop(0, seg, step, init, unroll=True)
    h_end = jnp.concatenate(h_end, axis=1)
    p_end = jnp.concatenate(p_end, axis=1)
    state = h0
    out = []
    for s in range(nseg):
        h_loc = jnp.concatenate([b_buf[j, s * pitch:s * pitch + seg, :] for j in range(nlb)], axis=1)
        p_loc = jnp.concatenate([a_buf[j, s * pitch:s * pitch + seg, :] for j in range(nlb)], axis=1)
        out.append(h_loc + p_loc * state)
        state = h_end[s:s + 1] + p_end[s:s + 1] * state
    return jnp.concatenate(out, axis=0), state


def _mixer_kernel(x_ref, mod_ref, modp_ref, n1g_ref, win_ref, convw_ref, convb_ref, wgate_ref, bgate_ref, lam_ref,
                  lng_ref, lnb_ref, sguw_ref, sgub_ref, gl_ref, gs_ref, wout_ref, n2g_ref, rwt_ref, rb_ref,
                  h1_ref, xs_ref, pos_ref, cnt_ref,
                  xa_tail, h_carry, scan_a, scan_b, h1_prev, *, tiles_per_seq):
    t = x_ref.shape[0]
    i = pl.program_id(0)

    @pl.when(i == 0)
    def _():
        h1_prev[...] = jnp.zeros_like(h1_prev)

    @pl.when(i % tiles_per_seq == 0)
    def _():
        xa_tail[...] = jnp.zeros_like(xa_tail)
        h_carry[...] = jnp.zeros_like(h_carry)

    z2b, logits = _router_logits(h1_prev[...], modp_ref[...], n2g_ref, rwt_ref, rb_ref)
    pos_rows, gate_pieces = _route(logits, pos_ref, cnt_ref)

    mod = mod_ref[...]
    sh1, sc1, g1 = mod[0:1], mod[1:2], mod[2:3]

    x = x_ref[...]
    z = _rms(x, n1g_ref[...] * (1.0 + sc1)) + sh1
    proj = jnp.dot(z.astype(BF16), win_ref[...], preferred_element_type=F32)
    perm_blocks = _perm_blocks(pos_rows, TILE_ROWS, t)
    n_pb = len(perm_blocks)
    sorted_blocks = iter(range(n_pb))

    def sort_next():
        q = next(sorted_blocks)
        _sort_rows(perm_blocks, z2b, gate_pieces, xs_ref, q, q + 1)

    sort_next()
    xa = proj[:, 0:D_LRU]
    ya = proj[:, D_LRU:2 * D_LRU]
    u = proj[:, 2 * D_LRU:2 * D_LRU + D_SGU]
    v = proj[:, 2 * D_LRU + D_SGU:]

    tail = xa_tail[...]
    row8 = lax.broadcasted_iota(I32, (8, 1), 0)
    xc = xa * convw_ref[CONV_WIDTH - 1:CONV_WIDTH, :] + convb_ref[...]
    for sft in range(1, CONV_WIDTH):
        rolled = pltpu.roll(xa, sft, 0)
        head = jnp.where(row8 < sft, pltpu.roll(tail, sft, 0), rolled[0:8])
        shifted = jnp.concatenate([head, rolled[8:]], axis=0)
        xc = xc + shifted * convw_ref[CONV_WIDTH - 1 - sft:CONV_WIDTH - sft, :]
    xa_tail[...] = xa[t - 8:t]
    sort_next()

    xcb = xc.astype(BF16)
    hw = D_LRU // 2
    g_lo = jnp.dot(xcb[:, 0:hw], wgate_ref[0], preferred_element_type=F32)
    g_hi = jnp.dot(xcb[:, hw:], wgate_ref[1], preferred_element_type=F32)
    sort_next()
    r_gate = _sigmoid(jnp.concatenate([g_lo[:, 0:hw], g_hi[:, 0:hw]], axis=1) + bgate_ref[:, 0:D_LRU])
    i_gate = _sigmoid(jnp.concatenate([g_lo[:, hw:], g_hi[:, hw:]], axis=1) + bgate_ref[:, D_LRU:])
    sort_next()
    nlam = -lam_ref[...]
    softplus = jnp.maximum(nlam, 0.0) + jnp.log1p(jnp.exp(-jnp.abs(nlam)))
    log_a = (-LRU_C) * r_gate * softplus
    a = jnp.exp(log_a)
    sort_next()
    om = -jnp.tanh(log_a) * (a * a + 1.0)
    mult = jnp.where(om > 0.0, om * lax.rsqrt(om), 0.0)
    bterm = mult * i_gate * xc
    sort_next()
    h, h_last = _linear_scan(a, bterm, h_carry[...], scan_a, scan_b)
    h_carry[...] = h_last
    sort_next()
    o_lru = _rms(h * _gelu_tanh(ya), gl_ref[...])
    sort_next()

    ug = _gelu_tanh(u)
    vg = _gelu_tanh(v)
    sort_next()
    assert next(sorted_blocks, None) is None
    mu = jnp.mean(vg, axis=-1, keepdims=True)
    vcen = vg - mu
    var = jnp.mean(vcen * vcen, axis=-1, keepdims=True)
    vn = (vcen * lax.rsqrt(var + EPS) * lng_ref[...] + lnb_ref[...]).astype(BF16)
    ri = lax.broadcasted_iota(I32, (CHUNK, CHUNK), 0)
    ci = lax.broadcasted_iota(I32, (CHUNK, CHUNK), 1)
    causal = ri >= ci
    lane = lax.broadcasted_iota(I32, (1, 2 * SGU_HEAD_DIM), 1)
    first_half = lane < SGU_HEAD_DIM
    pair_w = []
    for p in range(SGU_HEADS // 2):
        w0 = jnp.where(causal, sguw_ref[2 * p], 0.0).astype(BF16)
        w1 = jnp.where(causal, sguw_ref[2 * p + 1], 0.0).astype(BF16)
        pair_w.append(jnp.concatenate([w0, w1], axis=1))
    chunks = []
    zero = jnp.zeros((), BF16)
    for n in range(t // CHUNK):
        cols = []
        for p in range(SGU_HEADS // 2):
            blk = vn[n * CHUNK:(n + 1) * CHUNK, p * LANES:(p + 1) * LANES]
            rhs = jnp.concatenate([jnp.where(first_half, blk, zero), jnp.where(first_half, zero, blk)], axis=0)
            cols.append(jnp.dot(pair_w[p], rhs, preferred_element_type=F32))
        chunks.append(jnp.concatenate(cols, axis=1) + sgub_ref[...])
    mixed = jnp.concatenate(chunks, axis=0)
    o_sgu = _rms(ug * mixed, gs_ref[...])

    heads = jnp.concatenate([o_lru, o_sgu], axis=1).astype(BF16)
    h1 = x + g1 * jnp.dot(heads, wout_ref[...], preferred_element_type=F32)
    h1_ref[...] = h1
    h1_prev[...] = h1


def _router_logits(h1, mod, n2g_ref, rwt_ref, rb_ref):
    sh2, sc2 = mod[3:4], mod[4:5]
    z2b = (_rms(h1, n2g_ref[...] * (1.0 + sc2)) + sh2).astype(BF16)
    logits = lax.dot_general(rwt_ref[...].astype(BF16), z2b, (((1,), (1,)), ((), ())),
                             preferred_element_type=F32) + rb_ref[...]
    return z2b, logits


def _route(logits, pos_ref, cnt_ref):
    t = logits.shape[1]
    eidx = lax.broadcasted_iota(I32, (N_EXPERTS, t), 0)
    work = logits
    sel = []
    tops = []
    for k in range(TOP_K):
        m = jnp.max(work, axis=0, keepdims=True)
        idx = jnp.min(jnp.where(work == m, eidx, N_EXPERTS), axis=0, keepdims=True)
        onehot = eidx == idx
        sel.append(onehot)
        tops.append(m)
        work = jnp.where(onehot, -jnp.inf, work)
    exps = [jnp.exp(tk - tops[0]) for tk in tops]
    denom = exps[0] + exps[1] + exps[2] + exps[3]
    chosen = jnp.zeros((N_EXPERTS, t), F32)
    gsel = jnp.zeros((N_EXPERTS, t), F32)
    for k in range(TOP_K):
        chosen = jnp.where(sel[k], 1.0, chosen)
        gsel = jnp.where(sel[k], exps[k] / denom, gsel)

    chosen_b = chosen.astype(BF16)
    si = lax.broadcasted_iota(I32, (t, t), 0)
    ti = lax.broadcasted_iota(I32, (t, t), 1)
    before = jnp.where(si < ti, 1.0, 0.0).astype(BF16)
    excl = jnp.dot(chosen_b, before, preferred_element_type=F32)
    cnt_col = jnp.sum(chosen, axis=1, keepdims=True).astype(I32)
    cnt_ref[...] = jnp.broadcast_to(cnt_col, cnt_ref.shape)
    run_len = ((cnt_col + (ROW_ALIGN - 1)) // ROW_ALIGN * ROW_ALIGN).astype(F32)
    er = lax.broadcasted_iota(I32, (N_EXPERTS, N_EXPERTS), 0)
    ec = lax.broadcasted_iota(I32, (N_EXPERTS, N_EXPERTS), 1)
    lower = jnp.where(ec < er, 1.0, 0.0).astype(BF16)
    run_start = jnp.dot(lower, jnp.broadcast_to(run_len, (N_EXPERTS, LANES)).astype(BF16),
                        preferred_element_type=F32)[:, 0:1]
    posmat = excl + run_start
    pos_rows = []
    for k in range(TOP_K):
        pk = jnp.sum(jnp.where(sel[k], posmat, 0.0), axis=0, keepdims=True).astype(I32)
        pos_ref[k:k + 1, :] = pk
        pos_rows.append(pk)

    g_hi = gsel.astype(BF16)
    rem = gsel - g_hi.astype(F32)
    g_mid = rem.astype(BF16)
    g_lo = (rem - g_mid.astype(F32)).astype(BF16)
    gp = jnp.concatenate([g_hi, g_mid, g_lo, jnp.zeros((PAYLOAD - 3 * N_EXPERTS, t), BF16)], axis=0)
    return pos_rows, gp


def _sort_rows(perm_blocks, z2b, gate_pieces, xs_ref, q0, q1):
    for q in range(q0, q1):
        rows = slice(q * PERM_BLOCK, (q + 1) * PERM_BLOCK)
        xs_ref[rows, 0:D_MODEL] = jnp.dot(perm_blocks[q], z2b, preferred_element_type=F32)
        xs_ref[rows, D_MODEL:] = lax.dot_general(perm_blocks[q], gate_pieces, (((1,), (1,)), ((), ())),
                                                 preferred_element_type=F32)


def _mixer_call(x, mod3, n1g, win, convw, convb, wgate, bgate, lam, lng, lnb, sguw, sgub_full, gl, gs, wout,
                n2g, rwt, rb):
    bsz, seq, d = x.shape
    t = SEQ_TILE
    tiles = seq // t
    n_tiles = bsz * tiles
    x2 = x.reshape(bsz * seq, d)

    def const(shape):
        return pl.BlockSpec(shape, lambda i: (0,) * len(shape))

    mixed = lambda i: jnp.minimum(i, n_tiles - 1)
    routed = lambda i: jnp.maximum(i - 1, 0)
    return pl.pallas_call(
        functools.partial(_mixer_kernel, tiles_per_seq=tiles),
        grid=(n_tiles + 1,),
        in_specs=[
            pl.BlockSpec((t, d), lambda i: (mixed(i), 0)),
            pl.BlockSpec((None, 6, d), lambda i: (mixed(i) // tiles, 0, 0)),
            pl.BlockSpec((None, 6, d), lambda i: (routed(i) // tiles, 0, 0)),
            const((1, d)),
            const((d, 2 * d)),
            const((CONV_WIDTH, D_LRU)),
            const((1, D_LRU)),
            const((2, D_LRU // 2, D_LRU)),
            const((1, 2 * D_LRU)),
            const((1, D_LRU)),
            const((1, D_SGU)),
            const((1, D_SGU)),
            const((SGU_HEADS, CHUNK, CHUNK)),
            const((CHUNK, D_SGU)),
            const((1, D_LRU)),
            const((1, D_SGU)),
            const((d, d)),
            const((1, d)),
            const((N_EXPERTS, d)),
            const((N_EXPERTS, 1)),
        ],
        out_specs=[
            pl.BlockSpec((t, d), lambda i: (i, 0)),
            pl.BlockSpec((TILE_ROWS, ROW_W), lambda i: (routed(i), 0)),
            pl.BlockSpec((TOP_K, t), lambda i: (0, routed(i))),
            pl.BlockSpec((None, N_EXPERTS, LANES), lambda i: (routed(i), 0, 0)),
        ],
        out_shape=[
            jax.ShapeDtypeStruct(((n_tiles + 1) * t, d), F32),
            jax.ShapeDtypeStruct((n_tiles * TILE_ROWS, ROW_W), F32),
            jax.ShapeDtypeStruct((TOP_K, n_tiles * t), I32),
            jax.ShapeDtypeStruct((n_tiles, N_EXPERTS, LANES), I32),
        ],
        scratch_shapes=[pltpu.VMEM((8, D_LRU), F32), pltpu.VMEM((1, D_LRU), F32),
                        pltpu.VMEM((D_LRU // LANES, SCAN_SEGMENTS * _scan_pitch(t), LANES), F32),
                        pltpu.VMEM((D_LRU // LANES, SCAN_SEGMENTS * _scan_pitch(t), LANES), F32),
                        pltpu.VMEM((t, d), F32)],
        compiler_params=pltpu.CompilerParams(
            dimension_semantics=("arbitrary",), vmem_limit_bytes=VMEM_LIMIT_BYTES),
        name="mixer_router",
    )(x2, mod3, mod3, n1g, win, convw, convb, wgate, bgate, lam, lng, lnb, sguw, sgub_full, gl, gs, wout, n2g, rwt,
      rb)


def _expert_kernel(be_ref, first_ref, nval_ref, jrow_ref, ilo_ref, ihi_ref, nexte_ref, nb_ref,
                   runsrc_ref, rundst_ref, runn_ref, tailrow_ref, taillen_ref,
                   xs_hbm, wgu_hbm, bgu_ref, wd_hbm, bd_ref,
                   ys_hbm,
                   xbuf, ybuf, zbuf, wgu_st, wd_st, wgu_bf, wd_bf, gsem, ssem, zsem, wsem):
    r = ROW_BLOCK
    n_tiles = tailrow_ref.shape[0]
    nb = nb_ref[0]

    rows = lambda v: pl.multiple_of(v, ROW_ALIGN)

    def weight_copies(e):
        return (pltpu.make_async_copy(wgu_hbm.at[e], wgu_st, wsem.at[0]),
                pltpu.make_async_copy(wd_hbm.at[e], wd_st, wsem.at[1]))

    def tail_copy(i):
        n = rows(taillen_ref[i])
        return pltpu.make_async_copy(zbuf.at[pl.ds(0, n), :], ys_hbm.at[pl.ds(rows(tailrow_ref[i]), n), :],
                                     zsem.at[0])

    def for_each_run(blk, fn):
        first = ilo_ref[blk]
        count = ihi_ref[blk] - first + 1
        last_entry = runn_ref.shape[0] - 1

        def pair_body(p, c):
            k0 = blk * n_tiles + first + 2 * p
            k1 = jnp.minimum(k0 + 1, last_entry)
            n0 = runn_ref[k0]
            n1 = jnp.where(2 * p + 1 < count, runn_ref[k1], 0)
            s0, d0, s1, d1 = runsrc_ref[k0], rundst_ref[k0], runsrc_ref[k1], rundst_ref[k1]

            @pl.when(n0 > 0)
            def _():
                fn(rows(s0), rows(d0), rows(n0))

            @pl.when(n1 > 0)
            def _():
                fn(rows(s1), rows(d1), rows(n1))
            return c
        lax.fori_loop(0, (count + 1) // 2, pair_body, 0)

    def start_gather(dst_slot):
        def fn(hbm_row, buf_row, n):
            pltpu.make_async_copy(xs_hbm.at[pl.ds(hbm_row, n), :],
                                  xbuf.at[dst_slot, pl.ds(buf_row, n), :], gsem.at[dst_slot]).start()
        return fn

    def start_scatter(src_slot):
        def fn(hbm_row, buf_row, n):
            pltpu.make_async_copy(ybuf.at[src_slot, pl.ds(buf_row, n), :],
                                  ys_hbm.at[pl.ds(hbm_row, n), :], ssem.at[src_slot]).start()
        return fn

    def wait_gather(s, n):
        pltpu.make_async_copy(xs_hbm.at[pl.ds(0, rows(n)), :], xbuf.at[s, pl.ds(0, rows(n)), :], gsem.at[s]).wait()

    def wait_scatter(s, n):
        pltpu.make_async_copy(ybuf.at[s, pl.ds(0, rows(n)), :], ys_hbm.at[pl.ds(0, rows(n)), :], ssem.at[s]).wait()

    for c in weight_copies(be_ref[0]):
        c.start()
    xbuf[...] = jnp.zeros_like(xbuf)
    for_each_run(0, start_gather(0))
    zbuf[...] = jnp.zeros_like(zbuf)

    def start_tail(i, c):
        tail_copy(i).start()
        return c
    lax.fori_loop(0, n_tiles, start_tail, 0)

    def block(b, carry):
        slot = b & 1
        wait_gather(slot, nval_ref[b])

        @pl.when(b + 1 < nb)
        def _():
            for_each_run(b + 1, start_gather(1 - slot))

        @pl.when(first_ref[b] == 1)
        def _():
            for c in weight_copies(be_ref[b]):
                c.wait()
            wgu_bf[...] = wgu_st[...].astype(BF16)
            wd_bf[...] = wd_st[...].astype(BF16)

            @pl.when(nexte_ref[b] >= 0)
            def _():
                for c in weight_copies(nexte_ref[b]):
                    c.start()

        @pl.when(b >= 2)
        def _():
            wait_scatter(slot, nval_ref[jnp.maximum(b - 2, 0)])

        def expert_rows(m):
            xw = xbuf[slot, 0:m]
            xb = xw[:, 0:D_MODEL].astype(BF16)
            lane = lax.broadcasted_iota(I32, (1, PAYLOAD), 1)
            gate = jnp.sum(jnp.where((lane & (N_EXPERTS - 1)) == be_ref[b], xw[:, D_MODEL:], 0.0),
                           axis=1, keepdims=True)
            gu = jnp.dot(xb, wgu_bf[...], preferred_element_type=F32) + bgu_ref[be_ref[b]]
            g = jnp.minimum(gu[:, 0:D_MODEL], SWIGLU_LIMIT)
            lin = jnp.clip(gu[:, D_MODEL:], -SWIGLU_LIMIT, SWIGLU_LIMIT)
            act = g * _sigmoid(SWIGLU_ALPHA * g) * (lin + 1.0)
            y = jnp.dot(act.astype(BF16), wd_bf[...], preferred_element_type=F32) + bd_ref[be_ref[b]]
            ybuf[slot, 0:m] = _pack_bf16_pairs(y * gate)

        quarter = r // ROW_PATHS
        for k in range(1, ROW_PATHS + 1):
            @pl.when((nval_ref[b] > (k - 1) * quarter) & (nval_ref[b] <= k * quarter))
            def _():
                expert_rows(k * quarter)

        for_each_run(b, start_scatter(slot))
        return carry

    lax.fori_loop(0, nb, block, 0)

    last = nb - 1
    wait_scatter(last & 1, nval_ref[last])

    @pl.when(nb >= 2)
    def _():
        wait_scatter(1 - (last & 1), nval_ref[jnp.maximum(last - 1, 0)])

    def wait_tail(i, c):
        tail_copy(i).wait()
        return c
    lax.fori_loop(0, n_tiles, wait_tail, 0)


def _expert_call(tables, xs, w_gu, b_gu, w_down, b_down):
    nblk = tables[0].shape[0]
    r = ROW_BLOCK
    d = D_MODEL
    whole = lambda i, *_: (0, 0, 0)
    grid_spec = pltpu.PrefetchScalarGridSpec(
        num_scalar_prefetch=len(tables),
        grid=(1,),
        in_specs=[
            pl.BlockSpec(memory_space=pl.ANY),
            pl.BlockSpec(memory_space=pl.ANY),
            pl.BlockSpec((N_EXPERTS, 1, 2 * d), whole),
            pl.BlockSpec(memory_space=pl.ANY),
            pl.BlockSpec((N_EXPERTS, 1, d), whole),
        ],
        out_specs=pl.BlockSpec(memory_space=pl.ANY),
        scratch_shapes=[
            pltpu.VMEM((2, r, ROW_W), F32),
            pltpu.VMEM((2, r, d // 2), U32),
            pltpu.VMEM((ROW_ALIGN * N_EXPERTS, d // 2), U32),
            pltpu.VMEM((d, 2 * d), F32),
            pltpu.VMEM((d, d), F32),
            pltpu.VMEM((d, 2 * d), BF16),
            pltpu.VMEM((d, d), BF16),
            pltpu.SemaphoreType.DMA((2,)),
            pltpu.SemaphoreType.DMA((2,)),
            pltpu.SemaphoreType.DMA((1,)),
            pltpu.SemaphoreType.DMA((2,)),
        ],
    )
    return pl.pallas_call(
        _expert_kernel,
        grid_spec=grid_spec,
        out_shape=jax.ShapeDtypeStruct((xs.shape[0], d // 2), U32),
        compiler_params=pltpu.CompilerParams(
            dimension_semantics=("arbitrary",), vmem_limit_bytes=VMEM_LIMIT_BYTES),
        name="experts",
    )(*tables, xs, w_gu, b_gu, w_down, b_down)


def _final_kernel(h1_ref, mod_ref, ys_ref, pos_ref, fg_ref, o_ref):
    t = h1_ref.shape[0]
    g2 = mod_ref[5:6, :]
    pos_rows = [pos_ref[k:k + 1, :] for k in range(TOP_K)]
    perm = jnp.concatenate(_perm_blocks(pos_rows, TILE_ROWS, t), axis=0)
    tn = (((0,), (0,)), ((), ()))
    y_lo, y_hi = _unpack_bf16_pairs(ys_ref[...])
    moe = jnp.concatenate([lax.dot_general(perm, y_lo, tn, preferred_element_type=F32),
                           lax.dot_general(perm, y_hi, tn, preferred_element_type=F32)], axis=1)
    o_ref[...] = _rms(h1_ref[...] + g2 * moe, fg_ref[...])


def _final_call(h1, mod3, ys, pos, final_g, bsz, seq):
    d = h1.shape[1]
    n_tok = bsz * seq
    t = SEQ_TILE
    per_batch = seq // t
    out = pl.pallas_call(
        _final_kernel,
        grid=(n_tok // t,),
        in_specs=[
            pl.BlockSpec((t, d), lambda i: (i, 0)),
            pl.BlockSpec((None, 6, d), lambda i: (i // per_batch, 0, 0)),
            pl.BlockSpec((TILE_ROWS, d // 2), lambda i: (i, 0)),
            pl.BlockSpec((TOP_K, t), lambda i: (0, i)),
            pl.BlockSpec((1, d), lambda i: (0, 0)),
        ],
        out_specs=pl.BlockSpec((t, d), lambda i: (i, 0)),
        out_shape=jax.ShapeDtypeStruct((n_tok, d), F32),
        compiler_params=pltpu.CompilerParams(
            dimension_semantics=("arbitrary",), vmem_limit_bytes=VMEM_LIMIT_BYTES),
        name="combine_final",
    )(h1, mod3, ys, pos, final_g.reshape(1, d))
    return out.reshape(bsz, seq, d)


def _block_diag(w):
    h, i, o = w.shape
    eye = jnp.eye(h, dtype=w.dtype)
    return (w[:, :, None, :] * eye[:, None, :, None]).reshape(h * i, h * o)


def _route_tables(cnt):
    r = ROW_BLOCK
    n_tiles = cnt.shape[0]
    nblk = n_tiles * TILE_ROWS // r + N_EXPERTS
    cnt = (cnt + (ROW_ALIGN - 1)) // ROW_ALIGN * ROW_ALIGN
    ie = jnp.arange(N_EXPERTS, dtype=I32)
    it = jnp.arange(n_tiles, dtype=I32)
    e_before = (ie[:, None] < ie[None, :]).astype(I32)
    t_before = (it[:, None] < it[None, :]).astype(I32)
    total = jnp.sum(cnt, axis=0)
    cum = jnp.sum(t_before[:, :, None] * cnt[:, None, :], axis=0)
    seg_off = jnp.sum(cnt[:, :, None] * e_before[None, :, :], axis=1)
    base = seg_off + it[:, None] * TILE_ROWS
    nblk_e = (total + r - 1) // r
    blk_start = jnp.sum(nblk_e[:, None] * e_before, axis=0)
    blk_end = blk_start + nblk_e
    nb_used = jnp.sum(nblk_e)
    blk = jnp.arange(nblk, dtype=I32)
    used = blk < nb_used
    last_e = jnp.max(jnp.where(nblk_e > 0, ie, 0))
    be = jnp.minimum(jnp.sum((blk[:, None] >= blk_end[None, :]).astype(I32), axis=1), N_EXPERTS - 1)
    be = jnp.where(used, be, last_e)
    be_onehot = (be[:, None] == ie[None, :]).astype(I32)
    pick = lambda v: jnp.sum(be_onehot * v[None, :], axis=1)
    jrow = jnp.where(used, (blk - pick(blk_start)) * r, 0)
    nval = jnp.where(used, jnp.clip(pick(total) - jrow, 0, r), 0)
    first = jnp.concatenate([jnp.ones((1,), I32), (be[1:] != be[:-1]).astype(I32)])
    nxt = pick(blk_end)
    next_e = jnp.where(nxt < nb_used, jnp.sum((nxt[:, None] >= blk_end[None, :]).astype(I32), axis=1), -1)
    run_start = jnp.sum(be_onehot[:, None, :] * cum[None, :, :], axis=-1)
    run_end = run_start + jnp.sum(be_onehot[:, None, :] * cnt[None, :, :], axis=-1)
    ilo = jnp.sum((run_end <= jrow[:, None]).astype(I32), axis=1)
    ihi = n_tiles - 1 - jnp.sum((run_start >= (jrow + r)[:, None]).astype(I32), axis=1)
    lo = jnp.maximum(run_start, jrow[:, None])
    hi = jnp.minimum(run_end, (jrow + r)[:, None])
    run_n = jnp.where(used[:, None], jnp.clip(hi - lo, 0, r), 0)
    run_src = jnp.sum(be_onehot[:, None, :] * base[None, :, :], axis=-1) + (lo - run_start)
    run_dst = lo - jrow[:, None]
    used_rows = jnp.sum(cnt, axis=1)
    tail_row = jnp.arange(n_tiles, dtype=I32) * TILE_ROWS + used_rows
    tail_len = TILE_ROWS - used_rows
    i32 = lambda v: v.astype(I32)
    return (i32(be), i32(first), i32(nval), i32(jrow), i32(jnp.minimum(ilo, n_tiles - 1)), i32(ihi), i32(next_e),
            i32(nb_used).reshape(1), i32(run_src).reshape(-1), i32(run_dst).reshape(-1), i32(run_n).reshape(-1),
            i32(tail_row), i32(tail_len))


def kernel(x, c, ada_w, ada_b, norm1_g, w_in, conv_w, conv_b, lru_wr, lru_br, lru_wi, lru_bi, lru_lambda, sgu_ln_g, sgu_ln_b, sgu_w, sgu_b, gnorm_lru_g, gnorm_sgu_g, w_out, norm2_g, router_w, router_b, exp_w_gu, exp_b_gu, exp_w_down, exp_b_down, final_g):
    bsz, seq, d = x.shape
    depth = ada_w.shape[0]
    assert depth == 1 and d == D_MODEL and seq % SEQ_TILE == 0 and ROW_BLOCK >= ROW_ALIGN * N_EXPERTS
    l = 0
    mod = _adaln_call(c, ada_w[l], ada_b[l])
    mod3 = mod.reshape(bsz, 6, d)

    row = lambda v: v.reshape(1, -1)
    hw = D_LRU // 2
    wr_bd, wi_bd = _block_diag(lru_wr[l]), _block_diag(lru_wi[l])
    wgate = jnp.stack([jnp.concatenate([wr_bd[s:s + hw, s:s + hw], wi_bd[s:s + hw, s:s + hw]], axis=1)
                       for s in (0, hw)]).astype(BF16)
    bgate = jnp.concatenate([lru_br[l], lru_bi[l]]).reshape(1, -1)
    sgub_full = jnp.repeat(sgu_b[l].T, SGU_HEAD_DIM, axis=1)
    h1, xs, pos, cnt = _mixer_call(
        x, mod3, row(norm1_g[l]), w_in[l].astype(BF16), conv_w[l], row(conv_b[l]), wgate, bgate,
        row(lru_lambda[l]), row(sgu_ln_g[l]), row(sgu_ln_b[l]), sgu_w[l], sgub_full,
        row(gnorm_lru_g[l]), row(gnorm_sgu_g[l]), w_out[l].astype(BF16), row(norm2_g[l]),
        router_w[l].T, router_b[l].reshape(-1, 1))

    tables = _route_tables(cnt[:, :, 0])
    ys = _expert_call(tables, xs, exp_w_gu[l], exp_b_gu[l].reshape(N_EXPERTS, 1, -1),
                      exp_w_down[l], exp_b_down[l].reshape(N_EXPERTS, 1, -1))
    return _final_call(h1, mod3, ys, pos, final_g, bsz, seq)
```

```python
import functools

import jax
import jax.numpy as jnp
from jax import lax
from jax.experimental import pallas as pl
from jax.experimental.pallas import tpu as pltpu

F32 = jnp.float32
BF16 = jnp.bfloat16
I32 = jnp.int32
U32 = jnp.uint32

D_MODEL = 1024
D_LRU = 512
D_SGU = 512
LRU_HEADS = 8
CONV_WIDTH = 4
LRU_C = 8.0
SGU_HEADS = 8
SGU_HEAD_DIM = D_SGU // SGU_HEADS
CHUNK = 128
N_EXPERTS = 32
TOP_K = 4
SWIGLU_ALPHA = 1.702
SWIGLU_LIMIT = 7.0
EPS = 1e-6

LANES = 128
SEQ_TILE = 512
ROW_ALIGN = 8
TILE_ROWS = TOP_K * SEQ_TILE + ROW_ALIGN * N_EXPERTS
PAYLOAD = LANES
ROW_W = D_MODEL + PAYLOAD
ROW_BLOCK = 768
ROW_PATHS = 4
V7X_VMEM_BYTES = 64 * 1024 * 1024
VMEM_LIMIT_BYTES = V7X_VMEM_BYTES * 7 // 8


def _sigmoid(x):
    return 1.0 / (1.0 + jnp.exp(-x))


def _gelu_tanh(x):
    return 0.5 * x * (1.0 + jnp.tanh(0.7978845608028654 * (x + 0.044715 * (x * x * x))))


def _rms(x, g):
    ms = jnp.mean(x * x, axis=-1, keepdims=True)
    return x * lax.rsqrt(ms + EPS) * g


def _pack_bf16_pairs(v):
    n = v.shape[1] // 2
    r = v.astype(BF16).astype(F32)
    bits = lax.bitcast_convert_type(r, U32)
    return (bits[:, 0:n] >> 16) | (bits[:, n:] & jnp.uint32(0xFFFF0000))


def _unpack_bf16_pairs(w):
    lo = lax.bitcast_convert_type(w << 16, F32).astype(BF16)
    hi = lax.bitcast_convert_type(w & jnp.uint32(0xFFFF0000), F32).astype(BF16)
    return lo, hi


PERM_BLOCK = 256


def _perm_blocks(pos_rows, n_rows, n_cols):
    blk = PERM_BLOCK
    rid = lax.broadcasted_iota(I32, (blk, n_cols), 0).astype(F32).astype(BF16)
    one = jnp.ones((), BF16)
    blocks = []
    for q in range(n_rows // blk):
        acc = jnp.zeros((blk, n_cols), BF16)
        for pk in pos_rows:
            in_blk = jnp.where(pk // blk == q, pk % blk, -1).astype(F32).astype(BF16)
            acc = jnp.where(rid == in_blk, one, acc)
        blocks.append(acc)
    return blocks


def _adaln_kernel(c_ref, w_ref, b_ref, o_ref):
    c = c_ref[...]
    ca = c * _sigmoid(c)
    o_ref[...] = jnp.dot(ca.astype(BF16), w_ref[...].astype(BF16), preferred_element_type=F32) + b_ref[...]


def _adaln_call(c, w, b):
    bsz, d = c.shape
    n_out = w.shape[1]
    return pl.pallas_call(
        _adaln_kernel,
        grid=(n_out // d,),
        in_specs=[
            pl.BlockSpec((bsz, d), lambda j: (0, 0)),
            pl.BlockSpec((d, d), lambda j: (0, j)),
            pl.BlockSpec((1, d), lambda j: (0, j)),
        ],
        out_specs=pl.BlockSpec((bsz, d), lambda j: (0, j)),
        out_shape=jax.ShapeDtypeStruct((bsz, n_out), F32),
        name="adaln",
    )(c, w, b.reshape(1, n_out))


SCAN_SEGMENTS = 8
SCAN_PITCH_PAD = 8


def _scan_pitch(t):
    return t // SCAN_SEGMENTS + SCAN_PITCH_PAD


def _linear_scan(a, b, h0, a_buf, b_buf):
    t, c = a.shape
    nseg = SCAN_SEGMENTS
    seg = t // nseg
    pitch = _scan_pitch(t)
    nlb = c // LANES
    for j in range(nlb):
        for s in range(nseg):
            a_buf[j, s * pitch:s * pitch + seg, :] = a[s * seg:(s + 1) * seg, j * LANES:(j + 1) * LANES]
            b_buf[j, s * pitch:s * pitch + seg, :] = b[s * seg:(s + 1) * seg, j * LANES:(j + 1) * LANES]

    def step(g, carry):
        hs, ps = carry
        new_h, new_p = [], []
        for j in range(nlb):
            view = (j, pl.ds(g, nseg, stride=pitch), slice(None))
            ag = a_buf[view]
            hj = ag * hs[j] + b_buf[view]
            pj = ag * ps[j]
            b_buf[view] = hj
            a_buf[view] = pj
            new_h.append(hj)
            new_p.append(pj)
        return tuple(new_h), tuple(new_p)

    init = (tuple(jnp.zeros((nseg, LANES), F32) for _ in range(nlb)),
            tuple(jnp.ones((nseg, LANES), F32) for _ in range(nlb)))
    h_end, p_end = lax.fori_loop(0, seg, step, init, unroll=True)
    h_end = jnp.concatenate(h_end, axis=1)
    p_end = jnp.concatenate(p_end, axis=1)
    state = h0
    out = []
    for s in range(nseg):
        h_loc = jnp.concatenate([b_buf[j, s * pitch:s * pitch + seg, :] for j in range(nlb)], axis=1)
        p_loc = jnp.concatenate([a_buf[j, s * pitch:s * pitch + seg, :] for j in range(nlb)], axis=1)
        out.append(h_loc + p_loc * state)
        state = h_end[s:s + 1] + p_end[s:s + 1] * state
    return jnp.concatenate(out, axis=0), state


def _mixer_kernel(x_ref, mod_ref, modp_ref, n1g_ref, win_ref, convw_ref, convb_ref, wgate_ref, bgate_ref, lam_ref,
                  lng_ref, lnb_ref, sguw_ref, sgub_ref, gl_ref, gs_ref, wout_ref, n2g_ref, rwt_ref, rb_ref,
                  h1_ref, xs_ref, pos_ref, cnt_ref,
                  xa_tail, h_carry, scan_a, scan_b, h1_prev, *, tiles_per_seq):
    t = x_ref.shape[0]
    i = pl.program_id(0)

    @pl.when(i == 0)
    def _():
        h1_prev[...] = jnp.zeros_like(h1_prev)

    @pl.when(i % tiles_per_seq == 0)
    def _():
        xa_tail[...] = jnp.zeros_like(xa_tail)
        h_carry[...] = jnp.zeros_like(h_carry)

    z2b, logits = _router_logits(h1_prev[...], modp_ref[...], n2g_ref, rwt_ref, rb_ref)
    pos_rows, gate_pieces = _route(logits, pos_ref, cnt_ref)

    mod = mod_ref[...]
    sh1, sc1, g1 = mod[0:1], mod[1:2], mod[2:3]

    x = x_ref[...]
    z = _rms(x, n1g_ref[...] * (1.0 + sc1)) + sh1
    proj = jnp.dot(z.astype(BF16), win_ref[...], preferred_element_type=F32)
    perm_blocks = _perm_blocks(pos_rows, TILE_ROWS, t)
    n_pb = len(perm_blocks)
    sorted_blocks = iter(range(n_pb))

    def sort_next():
        q = next(sorted_blocks)
        _sort_rows(perm_blocks, z2b, gate_pieces, xs_ref, q, q + 1)

    sort_next()
    xa = proj[:, 0:D_LRU]
    ya = proj[:, D_LRU:2 * D_LRU]
    u = proj[:, 2 * D_LRU:2 * D_LRU + D_SGU]
    v = proj[:, 2 * D_LRU + D_SGU:]

    tail = xa_tail[...]
    row8 = lax.broadcasted_iota(I32, (8, 1), 0)
    xc = xa * convw_ref[CONV_WIDTH - 1:CONV_WIDTH, :] + convb_ref[...]
    for sft in range(1, CONV_WIDTH):
        rolled = pltpu.roll(xa, sft, 0)
        head = jnp.where(row8 < sft, pltpu.roll(tail, sft, 0), rolled[0:8])
        shifted = jnp.concatenate([head, rolled[8:]], axis=0)
        xc = xc + shifted * convw_ref[CONV_WIDTH - 1 - sft:CONV_WIDTH - sft, :]
    xa_tail[...] = xa[t - 8:t]
    sort_next()

    xcb = xc.astype(BF16)
    hw = D_LRU // 2
    g_lo = jnp.dot(xcb[:, 0:hw], wgate_ref[0], preferred_element_type=F32)
    g_hi = jnp.dot(xcb[:, hw:], wgate_ref[1], preferred_element_type=F32)
    sort_next()
    r_gate = _sigmoid(jnp.concatenate([g_lo[:, 0:hw], g_hi[:, 0:hw]], axis=1) + bgate_ref[:, 0:D_LRU])
    i_gate = _sigmoid(jnp.concatenate([g_lo[:, hw:], g_hi[:, hw:]], axis=1) + bgate_ref[:, D_LRU:])
    sort_next()
    nlam = -lam_ref[...]
    softplus = jnp.maximum(nlam, 0.0) + jnp.log1p(jnp.exp(-jnp.abs(nlam)))
    log_a = (-LRU_C) * r_gate * softplus
    a = jnp.exp(log_a)
    sort_next()
    om = -jnp.tanh(log_a) * (a * a + 1.0)
    mult = jnp.where(om > 0.0, om * lax.rsqrt(om), 0.0)
    bterm = mult * i_gate * xc
    sort_next()
    h, h_last = _linear_scan(a, bterm, h_carry[...], scan_a, scan_b)
    h_carry[...] = h_last
    sort_next()
    o_lru = _rms(h * _gelu_tanh(ya), gl_ref[...])
    sort_next()

    ug = _gelu_tanh(u)
    vg = _gelu_tanh(v)
    sort_next()
    assert next(sorted_blocks, None) is None
    mu = jnp.mean(vg, axis=-1, keepdims=True)
    vcen = vg - mu
    var = jnp.mean(vcen * vcen, axis=-1, keepdims=True)
    vn = (vcen * lax.rsqrt(var + EPS) * lng_ref[...] + lnb_ref[...]).astype(BF16)
    ri = lax.broadcasted_iota(I32, (CHUNK, CHUNK), 0)
    ci = lax.broadcasted_iota(I32, (CHUNK, CHUNK), 1)
    causal = ri >= ci
    lane = lax.broadcasted_iota(I32, (1, 2 * SGU_HEAD_DIM), 1)
    first_half = lane < SGU_HEAD_DIM
    pair_w = []
    for p in range(SGU_HEADS // 2):
        w0 = jnp.where(causal, sguw_ref[2 * p], 0.0).astype(BF16)
        w1 = jnp.where(causal, sguw_ref[2 * p + 1], 0.0).astype(BF16)
        pair_w.append(jnp.concatenate([w0, w1], axis=1))
    chunks = []
    zero = jnp.zeros((), BF16)
    for n in range(t // CHUNK):
        cols = []
        for p in range(SGU_HEADS // 2):
            blk = vn[n * CHUNK:(n + 1) * CHUNK, p * LANES:(p + 1) * LANES]
            rhs = jnp.concatenate([jnp.where(first_half, blk, zero), jnp.where(first_half, zero, blk)], axis=0)
            cols.append(jnp.dot(pair_w[p], rhs, preferred_element_type=F32))
        chunks.append(jnp.concatenate(cols, axis=1) + sgub_ref[...])
    mixed = jnp.concatenate(chunks, axis=0)
    o_sgu = _rms(ug * mixed, gs_ref[...])

    heads = jnp.concatenate([o_lru, o_sgu], axis=1).astype(BF16)
    h1 = x + g1 * jnp.dot(heads, wout_ref[...], preferred_element_type=F32)
    h1_ref[...] = h1
    h1_prev[...] = h1


def _router_logits(h1, mod, n2g_ref, rwt_ref, rb_ref):
    sh2, sc2 = mod[3:4], mod[4:5]
    z2b = (_rms(h1, n2g_ref[...] * (1.0 + sc2)) + sh2).astype(BF16)
    logits = lax.dot_general(rwt_ref[...].astype(BF16), z2b, (((1,), (1,)), ((), ())),
                             preferred_element_type=F32) + rb_ref[...]
    return z2b, logits


def _route(logits, pos_ref, cnt_ref):
    t = logits.shape[1]
    eidx = lax.broadcasted_iota(I32, (N_EXPERTS, t), 0)
    work = logits
    sel = []
    tops = []
    for k in range(TOP_K):
        m = jnp.max(work, axis=0, keepdims=True)
        idx = jnp.min(jnp.where(work == m, eidx, N_EXPERTS), axis=0, keepdims=True)
        onehot = eidx == idx
        sel.append(onehot)
        tops.append(m)
        work = jnp.where(onehot, -jnp.inf, work)
    exps = [jnp.exp(tk - tops[0]) for tk in tops]
    denom = exps[0] + exps[1] + exps[2] + exps[3]
    chosen = jnp.zeros((N_EXPERTS, t), F32)
    gsel = jnp.zeros((N_EXPERTS, t), F32)
    for k in range(TOP_K):
        chosen = jnp.where(sel[k], 1.0, chosen)
        gsel = jnp.where(sel[k], exps[k] / denom, gsel)

    chosen_b = chosen.astype(BF16)
    si = lax.broadcasted_iota(I32, (t, t), 0)
    ti = lax.broadcasted_iota(I32, (t, t), 1)
    before = jnp.where(si < ti, 1.0, 0.0).astype(BF16)
    excl = jnp.dot(chosen_b, before, preferred_element_type=F32)
    cnt_col = jnp.sum(chosen, axis=1, keepdims=True).astype(I32)
    cnt_ref[...] = jnp.broadcast_to(cnt_col, cnt_ref.shape)
    run_len = ((cnt_col + (ROW_ALIGN - 1)) // ROW_ALIGN * ROW_ALIGN).astype(F32)
    er = lax.broadcasted_iota(I32, (N_EXPERTS, N_EXPERTS), 0)
    ec = lax.broadcasted_iota(I32, (N_EXPERTS, N_EXPERTS), 1)
    lower = jnp.where(ec < er, 1.0, 0.0).astype(BF16)
    run_start = jnp.dot(lower, jnp.broadcast_to(run_len, (N_EXPERTS, LANES)).astype(BF16),
                        preferred_element_type=F32)[:, 0:1]
    posmat = excl + run_start
    pos_rows = []
    for k in range(TOP_K):
        pk = jnp.sum(jnp.where(sel[k], posmat, 0.0), axis=0, keepdims=True).astype(I32)
        pos_ref[k:k + 1, :] = pk
        pos_rows.append(pk)

    g_hi = gsel.astype(BF16)
    rem = gsel - g_hi.astype(F32)
    g_mid = rem.astype(BF16)
    g_lo = (rem - g_mid.astype(F32)).astype(BF16)
    gp = jnp.concatenate([g_hi, g_mid, g_lo, jnp.zeros((PAYLOAD - 3 * N_EXPERTS, t), BF16)], axis=0)
    return pos_rows, gp


def _sort_rows(perm_blocks, z2b, gate_pieces, xs_ref, q0, q1):
    for q in range(q0, q1):
        rows = slice(q * PERM_BLOCK, (q + 1) * PERM_BLOCK)
        xs_ref[rows, 0:D_MODEL] = jnp.dot(perm_blocks[q], z2b, preferred_element_type=F32)
        xs_ref[rows, D_MODEL:] = lax.dot_general(perm_blocks[q], gate_pieces, (((1,), (1,)), ((), ())),
                                                 preferred_element_type=F32)


def _mixer_call(x, mod3, n1g, win, convw, convb, wgate, bgate, lam, lng, lnb, sguw, sgub_full, gl, gs, wout,
                n2g, rwt, rb):
    bsz, seq, d = x.shape
    t = SEQ_TILE
    tiles = seq // t
    n_tiles = bsz * tiles
    x2 = x.reshape(bsz * seq, d)

    def const(shape):
        return pl.BlockSpec(shape, lambda i: (0,) * len(shape))

    mixed = lambda i: jnp.minimum(i, n_tiles - 1)
    routed = lambda i: jnp.maximum(i - 1, 0)
    return pl.pallas_call(
        functools.partial(_mixer_kernel, tiles_per_seq=tiles),
        grid=(n_tiles + 1,),
        in_specs=[
            pl.BlockSpec((t, d), lambda i: (mixed(i), 0)),
            pl.BlockSpec((None, 6, d), lambda i: (mixed(i) // tiles, 0, 0)),
            pl.BlockSpec((None, 6, d), lambda i: (routed(i) // tiles, 0, 0)),
            const((1, d)),
            const((d, 2 * d)),
            const((CONV_WIDTH, D_LRU)),
            const((1, D_LRU)),
            const((2, D_LRU // 2, D_LRU)),
            const((1, 2 * D_LRU)),
            const((1, D_LRU)),
            const((1, D_SGU)),
            const((1, D_SGU)),
            const((SGU_HEADS, CHUNK, CHUNK)),
            const((CHUNK, D_SGU)),
            const((1, D_LRU)),
            const((1, D_SGU)),
            const((d, d)),
            const((1, d)),
            const((N_EXPERTS, d)),
            const((N_EXPERTS, 1)),
        ],
        out_specs=[
            pl.BlockSpec((t, d), lambda i: (i, 0)),
            pl.BlockSpec((TILE_ROWS, ROW_W), lambda i: (routed(i), 0)),
            pl.BlockSpec((TOP_K, t), lambda i: (0, routed(i))),
            pl.BlockSpec((None, N_EXPERTS, LANES), lambda i: (routed(i), 0, 0)),
        ],
        out_shape=[
            jax.ShapeDtypeStruct(((n_tiles + 1) * t, d), F32),
            jax.ShapeDtypeStruct((n_tiles * TILE_ROWS, ROW_W), F32),
            jax.ShapeDtypeStruct((TOP_K, n_tiles * t), I32),
            jax.ShapeDtypeStruct((n_tiles, N_EXPERTS, LANES), I32),
        ],
        scratch_shapes=[pltpu.VMEM((8, D_LRU), F32), pltpu.VMEM((1, D_LRU), F32),
                        pltpu.VMEM((D_LRU // LANES, SCAN_SEGMENTS * _scan_pitch(t), LANES), F32),
                        pltpu.VMEM((D_LRU // LANES, SCAN_SEGMENTS * _scan_pitch(t), LANES), F32),
                        pltpu.VMEM((t, d), F32)],
        compiler_params=pltpu.CompilerParams(
            dimension_semantics=("arbitrary",), vmem_limit_bytes=VMEM_LIMIT_BYTES),
        name="mixer_router",
    )(x2, mod3, mod3, n1g, win, convw, convb, wgate, bgate, lam, lng, lnb, sguw, sgub_full, gl, gs, wout, n2g, rwt,
      rb)


def _expert_kernel(be_ref, first_ref, nval_ref, jrow_ref, ilo_ref, ihi_ref, nexte_ref, nb_ref,
                   runsrc_ref, rundst_ref, runn_ref, tailrow_ref, taillen_ref,
                   xs_hbm, wgu_hbm, bgu_ref, wd_hbm, bd_ref,
                   ys_hbm,
                   xbuf, ybuf, zbuf, wgu_st, wd_st, wgu_bf, wd_bf, gsem, ssem, zsem, wsem):
    r = ROW_BLOCK
    n_tiles = tailrow_ref.shape[0]
    nb = nb_ref[0]

    rows = lambda v: pl.multiple_of(v, ROW_ALIGN)

    def weight_copies(e):
        return (pltpu.make_async_copy(wgu_hbm.at[e], wgu_st, wsem.at[0]),
                pltpu.make_async_copy(wd_hbm.at[e], wd_st, wsem.at[1]))

    def tail_copy(i):
        n = rows(taillen_ref[i])
        return pltpu.make_async_copy(zbuf.at[pl.ds(0, n), :], ys_hbm.at[pl.ds(rows(tailrow_ref[i]), n), :],
                                     zsem.at[0])

    def for_each_run(blk, fn):
        first = ilo_ref[blk]
        count = ihi_ref[blk] - first + 1
        last_entry = runn_ref.shape[0] - 1

        def pair_body(p, c):
            k0 = blk * n_tiles + first + 2 * p
            k1 = jnp.minimum(k0 + 1, last_entry)
            n0 = runn_ref[k0]
            n1 = jnp.where(2 * p + 1 < count, runn_ref[k1], 0)
            s0, d0, s1, d1 = runsrc_ref[k0], rundst_ref[k0], runsrc_ref[k1], rundst_ref[k1]

            @pl.when(n0 > 0)
            def _():
                fn(rows(s0), rows(d0), rows(n0))

            @pl.when(n1 > 0)
            def _():
                fn(rows(s1), rows(d1), rows(n1))
            return c
        lax.fori_loop(0, (count + 1) // 2, pair_body, 0)

    def start_gather(dst_slot):
        def fn(hbm_row, buf_row, n):
            pltpu.make_async_copy(xs_hbm.at[pl.ds(hbm_row, n), :],
                                  xbuf.at[dst_slot, pl.ds(buf_row, n), :], gsem.at[dst_slot]).start()
        return fn

    def start_scatter(src_slot):
        def fn(hbm_row, buf_row, n):
            pltpu.make_async_copy(ybuf.at[src_slot, pl.ds(buf_row, n), :],
                                  ys_hbm.at[pl.ds(hbm_row, n), :], ssem.at[src_slot]).start(priority=1)
        return fn

    def wait_gather(s, n):
        pltpu.make_async_copy(xs_hbm.at[pl.ds(0, rows(n)), :], xbuf.at[s, pl.ds(0, rows(n)), :], gsem.at[s]).wait()

    def wait_scatter(s, n):
        pltpu.make_async_copy(ybuf.at[s, pl.ds(0, rows(n)), :], ys_hbm.at[pl.ds(0, rows(n)), :], ssem.at[s]).wait()

    for c in weight_copies(be_ref[0]):
        c.start()
    xbuf[...] = jnp.zeros_like(xbuf)
    for_each_run(0, start_gather(0))
    zbuf[...] = jnp.zeros_like(zbuf)

    def start_tail(i, c):
        tail_copy(i).start()
        return c
    lax.fori_loop(0, n_tiles, start_tail, 0)

    def block(b, carry):
        slot = b & 1
        wait_gather(slot, nval_ref[b])

        @pl.when(b + 1 < nb)
        def _():
            for_each_run(b + 1, start_gather(1 - slot))

        @pl.when(first_ref[b] == 1)
        def _():
            for c in weight_copies(be_ref[b]):
                c.wait()
            wgu_bf[...] = wgu_st[...].astype(BF16)
            wd_bf[...] = wd_st[...].astype(BF16)

            @pl.when(nexte_ref[b] >= 0)
            def _():
                for c in weight_copies(nexte_ref[b]):
                    c.start()

        @pl.when(b >= 2)
        def _():
            wait_scatter(slot, nval_ref[jnp.maximum(b - 2, 0)])

        def expert_rows(m):
            xw = xbuf[slot, 0:m]
            xb = xw[:, 0:D_MODEL].astype(BF16)
            lane = lax.broadcasted_iota(I32, (1, PAYLOAD), 1)
            gate = jnp.sum(jnp.where((lane & (N_EXPERTS - 1)) == be_ref[b], xw[:, D_MODEL:], 0.0),
                           axis=1, keepdims=True)
            gu = jnp.dot(xb, wgu_bf[...], preferred_element_type=F32) + bgu_ref[be_ref[b]]
            g = jnp.minimum(gu[:, 0:D_MODEL], SWIGLU_LIMIT)
            lin = jnp.clip(gu[:, D_MODEL:], -SWIGLU_LIMIT, SWIGLU_LIMIT)
            act = g * _sigmoid(SWIGLU_ALPHA * g) * (lin + 1.0)
            y = jnp.dot(act.astype(BF16), wd_bf[...], preferred_element_type=F32) + bd_ref[be_ref[b]]
            ybuf[slot, 0:m] = _pack_bf16_pairs(y * gate)

        quarter = r // ROW_PATHS
        for k in range(1, ROW_PATHS + 1):
            @pl.when((nval_ref[b] > (k - 1) * quarter) & (nval_ref[b] <= k * quarter))
            def _():
                expert_rows(k * quarter)

        for_each_run(b, start_scatter(slot))
        return carry

    lax.fori_loop(0, nb, block, 0)

    last = nb - 1
    wait_scatter(last & 1, nval_ref[last])

    @pl.when(nb >= 2)
    def _():
        wait_scatter(1 - (last & 1), nval_ref[jnp.maximum(last - 1, 0)])

    def wait_tail(i, c):
        tail_copy(i).wait()
        return c
    lax.fori_loop(0, n_tiles, wait_tail, 0)


def _expert_call(tables, xs, w_gu, b_gu, w_down, b_down):
    nblk = tables[0].shape[0]
    r = ROW_BLOCK
    d = D_MODEL
    whole = lambda i, *_: (0, 0, 0)
    grid_spec = pltpu.PrefetchScalarGridSpec(
        num_scalar_prefetch=len(tables),
        grid=(1,),
        in_specs=[
            pl.BlockSpec(memory_space=pl.ANY),
            pl.BlockSpec(memory_space=pl.ANY),
            pl.BlockSpec((N_EXPERTS, 1, 2 * d), whole),
            pl.BlockSpec(memory_space=pl.ANY),
            pl.BlockSpec((N_EXPERTS, 1, d), whole),
        ],
        out_specs=pl.BlockSpec(memory_space=pl.ANY),
        scratch_shapes=[
            pltpu.VMEM((2, r, ROW_W), F32),
            pltpu.VMEM((2, r, d // 2), U32),
            pltpu.VMEM((ROW_ALIGN * N_EXPERTS, d // 2), U32),
            pltpu.VMEM((d, 2 * d), F32),
            pltpu.VMEM((d, d), F32),
            pltpu.VMEM((d, 2 * d), BF16),
            pltpu.VMEM((d, d), BF16),
            pltpu.SemaphoreType.DMA((2,)),
            pltpu.SemaphoreType.DMA((2,)),
            pltpu.SemaphoreType.DMA((1,)),
            pltpu.SemaphoreType.DMA((2,)),
        ],
    )
    return pl.pallas_call(
        _expert_kernel,
        grid_spec=grid_spec,
        out_shape=jax.ShapeDtypeStruct((xs.shape[0], d // 2), U32),
        compiler_params=pltpu.CompilerParams(
            dimension_semantics=("arbitrary",), vmem_limit_bytes=VMEM_LIMIT_BYTES),
        name="experts",
    )(*tables, xs, w_gu, b_gu, w_down, b_down)


def _final_kernel(h1_ref, mod_ref, ys_ref, pos_ref, fg_ref, o_ref):
    t = h1_ref.shape[0]
    g2 = mod_ref[5:6, :]
    pos_rows = [pos_ref[k:k + 1, :] for k in range(TOP_K)]
    perm = jnp.concatenate(_perm_blocks(pos_rows, TILE_ROWS, t), axis=0)
    tn = (((0,), (0,)), ((), ()))
    y_lo, y_hi = _unpack_bf16_pairs(ys_ref[...])
    moe = jnp.concatenate([lax.dot_general(perm, y_lo, tn, preferred_element_type=F32),
                           lax.dot_general(perm, y_hi, tn, preferred_element_type=F32)], axis=1)
    o_ref[...] = _rms(h1_ref[...] + g2 * moe, fg_ref[...])


def _final_call(h1, mod3, ys, pos, final_g, bsz, seq):
    d = h1.shape[1]
    n_tok = bsz * seq
    t = SEQ_TILE
    per_batch = seq // t
    out = pl.pallas_call(
        _final_kernel,
        grid=(n_tok // t,),
        in_specs=[
            pl.BlockSpec((t, d), lambda i: (i, 0)),
            pl.BlockSpec((None, 6, d), lambda i: (i // per_batch, 0, 0)),
            pl.BlockSpec((TILE_ROWS, d // 2), lambda i: (i, 0)),
            pl.BlockSpec((TOP_K, t), lambda i: (0, i)),
            pl.BlockSpec((1, d), lambda i: (0, 0)),
        ],
        out_specs=pl.BlockSpec((t, d), lambda i: (i, 0)),
        out_shape=jax.ShapeDtypeStruct((n_tok, d), F32),
        compiler_params=pltpu.CompilerParams(
            dimension_semantics=("arbitrary",), vmem_limit_bytes=VMEM_LIMIT_BYTES),
        name="combine_final",
    )(h1, mod3, ys, pos, final_g.reshape(1, d))
    return out.reshape(bsz, seq, d)


def _block_diag(w):
    h, i, o = w.shape
    eye = jnp.eye(h, dtype=w.dtype)
    return (w[:, :, None, :] * eye[:, None, :, None]).reshape(h * i, h * o)


def _route_tables(cnt):
    r = ROW_BLOCK
    n_tiles = cnt.shape[0]
    nblk = n_tiles * TILE_ROWS // r + N_EXPERTS
    cnt = (cnt + (ROW_ALIGN - 1)) // ROW_ALIGN * ROW_ALIGN
    ie = jnp.arange(N_EXPERTS, dtype=I32)
    it = jnp.arange(n_tiles, dtype=I32)
    e_before = (ie[:, None] < ie[None, :]).astype(I32)
    t_before = (it[:, None] < it[None, :]).astype(I32)
    total = jnp.sum(cnt, axis=0)
    cum = jnp.sum(t_before[:, :, None] * cnt[:, None, :], axis=0)
    seg_off = jnp.sum(cnt[:, :, None] * e_before[None, :, :], axis=1)
    base = seg_off + it[:, None] * TILE_ROWS
    nblk_e = (total + r - 1) // r
    blk_start = jnp.sum(nblk_e[:, None] * e_before, axis=0)
    blk_end = blk_start + nblk_e
    nb_used = jnp.sum(nblk_e)
    blk = jnp.arange(nblk, dtype=I32)
    used = blk < nb_used
    last_e = jnp.max(jnp.where(nblk_e > 0, ie, 0))
    be = jnp.minimum(jnp.sum((blk[:, None] >= blk_end[None, :]).astype(I32), axis=1), N_EXPERTS - 1)
    be = jnp.where(used, be, last_e)
    be_onehot = (be[:, None] == ie[None, :]).astype(I32)
    pick = lambda v: jnp.sum(be_onehot * v[None, :], axis=1)
    jrow = jnp.where(used, (blk - pick(blk_start)) * r, 0)
    nval = jnp.where(used, jnp.clip(pick(total) - jrow, 0, r), 0)
    first = jnp.concatenate([jnp.ones((1,), I32), (be[1:] != be[:-1]).astype(I32)])
    nxt = pick(blk_end)
    next_e = jnp.where(nxt < nb_used, jnp.sum((nxt[:, None] >= blk_end[None, :]).astype(I32), axis=1), -1)
    run_start = jnp.sum(be_onehot[:, None, :] * cum[None, :, :], axis=-1)
    run_end = run_start + jnp.sum(be_onehot[:, None, :] * cnt[None, :, :], axis=-1)
    ilo = jnp.sum((run_end <= jrow[:, None]).astype(I32), axis=1)
    ihi = n_tiles - 1 - jnp.sum((run_start >= (jrow + r)[:, None]).astype(I32), axis=1)
    lo = jnp.maximum(run_start, jrow[:, None])
    hi = jnp.minimum(run_end, (jrow + r)[:, None])
    run_n = jnp.where(used[:, None], jnp.clip(hi - lo, 0, r), 0)
    run_src = jnp.sum(be_onehot[:, None, :] * base[None, :, :], axis=-1) + (lo - run_start)
    run_dst = lo - jrow[:, None]
    used_rows = jnp.sum(cnt, axis=1)
    tail_row = jnp.arange(n_tiles, dtype=I32) * TILE_ROWS + used_rows
    tail_len = TILE_ROWS - used_rows
    i32 = lambda v: v.astype(I32)
    return (i32(be), i32(first), i32(nval), i32(jrow), i32(jnp.minimum(ilo, n_tiles - 1)), i32(ihi), i32(next_e),
            i32(nb_used).reshape(1), i32(run_src).reshape(-1), i32(run_dst).reshape(-1), i32(run_n).reshape(-1),
            i32(tail_row), i32(tail_len))


def kernel(x, c, ada_w, ada_b, norm1_g, w_in, conv_w, conv_b, lru_wr, lru_br, lru_wi, lru_bi, lru_lambda, sgu_ln_g, sgu_ln_b, sgu_w, sgu_b, gnorm_lru_g, gnorm_sgu_g, w_out, norm2_g, router_w, router_b, exp_w_gu, exp_b_gu, exp_w_down, exp_b_down, final_g):
    bsz, seq, d = x.shape
    depth = ada_w.shape[0]
    assert depth == 1 and d == D_MODEL and seq % SEQ_TILE == 0 and ROW_BLOCK >= ROW_ALIGN * N_EXPERTS
    l = 0
    mod = _adaln_call(c, ada_w[l], ada_b[l])
    mod3 = mod.reshape(bsz, 6, d)

    row = lambda v: v.reshape(1, -1)
    hw = D_LRU // 2
    wr_bd, wi_bd = _block_diag(lru_wr[l]), _block_diag(lru_wi[l])
    wgate = jnp.stack([jnp.concatenate([wr_bd[s:s + hw, s:s + hw], wi_bd[s:s + hw, s:s + hw]], axis=1)
                       for s in (0, hw)]).astype(BF16)
    bgate = jnp.concatenate([lru_br[l], lru_bi[l]]).reshape(1, -1)
    sgub_full = jnp.repeat(sgu_b[l].T, SGU_HEAD_DIM, axis=1)
    h1, xs, pos, cnt = _mixer_call(
        x, mod3, row(norm1_g[l]), w_in[l].astype(BF16), conv_w[l], row(conv_b[l]), wgate, bgate,
        row(lru_lambda[l]), row(sgu_ln_g[l]), row(sgu_ln_b[l]), sgu_w[l], sgub_full,
        row(gnorm_lru_g[l]), row(gnorm_sgu_g[l]), w_out[l].astype(BF16), row(norm2_g[l]),
        router_w[l].T, router_b[l].reshape(-1, 1))

    tables = _route_tables(cnt[:, :, 0])
    ys = _expert_call(tables, xs, exp_w_gu[l], exp_b_gu[l].reshape(N_EXPERTS, 1, -1),
                      exp_w_down[l], exp_b_down[l].reshape(N_EXPERTS, 1, -1))
    return _final_call(h1, mod3, ys, pos, final_g, bsz, seq)
```

```python
import functools

import jax
import jax.numpy as jnp
from jax import lax
from jax.experimental import pallas as pl
from jax.experimental.pallas import tpu as pltpu

F32 = jnp.float32
BF16 = jnp.bfloat16
I32 = jnp.int32
U32 = jnp.uint32

D_MODEL = 1024
D_LRU = 512
D_SGU = 512
LRU_HEADS = 8
CONV_WIDTH = 4
LRU_C = 8.0
SGU_HEADS = 8
SGU_HEAD_DIM = D_SGU // SGU_HEADS
CHUNK = 128
N_EXPERTS = 32
TOP_K = 4
SWIGLU_ALPHA = 1.702
SWIGLU_LIMIT = 7.0
EPS = 1e-6

LANES = 128
SEQ_TILE = 512
FINAL_TILES_PER_STEP = 2
ROW_ALIGN = 8
TILE_ROWS = TOP_K * SEQ_TILE + ROW_ALIGN * N_EXPERTS
PAYLOAD = LANES
ROW_W = D_MODEL + PAYLOAD
ROW_BLOCK = 768
ROW_PATHS = 4
V7X_VMEM_BYTES = 64 * 1024 * 1024
VMEM_LIMIT_BYTES = V7X_VMEM_BYTES * 7 // 8


def _sigmoid(x):
    return 1.0 / (1.0 + jnp.exp(-x))


def _gelu_tanh(x):
    return 0.5 * x * (1.0 + jnp.tanh(0.7978845608028654 * (x + 0.044715 * (x * x * x))))


def _rms(x, g):
    ms = jnp.mean(x * x, axis=-1, keepdims=True)
    return x * lax.rsqrt(ms + EPS) * g


def _pack_bf16_pairs(v):
    n = v.shape[1] // 2
    r = v.astype(BF16).astype(F32)
    bits = lax.bitcast_convert_type(r, U32)
    return (bits[:, 0:n] >> 16) | (bits[:, n:] & jnp.uint32(0xFFFF0000))


def _unpack_bf16_pairs(w):
    lo = lax.bitcast_convert_type(w << 16, F32).astype(BF16)
    hi = lax.bitcast_convert_type(w & jnp.uint32(0xFFFF0000), F32).astype(BF16)
    return lo, hi


PERM_BLOCK = 256


def _perm_blocks(pos_rows, n_rows, n_cols):
    blk = PERM_BLOCK
    rid = lax.broadcasted_iota(I32, (blk, n_cols), 0).astype(F32).astype(BF16)
    one = jnp.ones((), BF16)
    blocks = []
    for q in range(n_rows // blk):
        acc = jnp.zeros((blk, n_cols), BF16)
        for pk in pos_rows:
            in_blk = jnp.where(pk // blk == q, pk % blk, -1).astype(F32).astype(BF16)
            acc = jnp.where(rid == in_blk, one, acc)
        blocks.append(acc)
    return blocks


def _adaln_kernel(c_ref, w_ref, b_ref, o_ref):
    c = c_ref[...]
    ca = c * _sigmoid(c)
    o_ref[...] = jnp.dot(ca.astype(BF16), w_ref[...].astype(BF16), preferred_element_type=F32) + b_ref[...]


def _adaln_call(c, w, b):
    bsz, d = c.shape
    n_out = w.shape[1]
    return pl.pallas_call(
        _adaln_kernel,
        grid=(n_out // d,),
        in_specs=[
            pl.BlockSpec((bsz, d), lambda j: (0, 0)),
            pl.BlockSpec((d, d), lambda j: (0, j)),
            pl.BlockSpec((1, d), lambda j: (0, j)),
        ],
        out_specs=pl.BlockSpec((bsz, d), lambda j: (0, j)),
        out_shape=jax.ShapeDtypeStruct((bsz, n_out), F32),
        name="adaln",
    )(c, w, b.reshape(1, n_out))


SCAN_SEGMENTS = 8
SCAN_PITCH_PAD = 8


def _scan_pitch(t):
    return t // SCAN_SEGMENTS + SCAN_PITCH_PAD


def _linear_scan(a, b, h0, a_buf, b_buf):
    t, c = a.shape
    nseg = SCAN_SEGMENTS
    seg = t // nseg
    pitch = _scan_pitch(t)
    nlb = c // LANES
    for j in range(nlb):
        for s in range(nseg):
            a_buf[j, s * pitch:s * pitch + seg, :] = a[s * seg:(s + 1) * seg, j * LANES:(j + 1) * LANES]
            b_buf[j, s * pitch:s * pitch + seg, :] = b[s * seg:(s + 1) * seg, j * LANES:(j + 1) * LANES]

    def step(g, carry):
        hs, ps = carry
        new_h, new_p = [], []
        for j in range(nlb):
            view = (j, pl.ds(g, nseg, stride=pitch), slice(None))
            ag = a_buf[view]
            hj = ag * hs[j] + b_buf[view]
            pj = ag * ps[j]
            b_buf[view] = hj
            a_buf[view] = pj
            new_h.append(hj)
            new_p.append(pj)
        return tuple(new_h), tuple(new_p)

    init = (tuple(jnp.zeros((nseg, LANES), F32) for _ in range(nlb)),
            tuple(jnp.ones((nseg, LANES), F32) for _ in range(nlb)))
    h_end, p_end = lax.fori_loop(0, seg, step, init, unroll=True)
    h_end = jnp.concatenate(h_end, axis=1)
    p_end = jnp.concatenate(p_end, axis=1)
    state = h0
    out = []
    for s in range(nseg):
        h_loc = jnp.concatenate([b_buf[j, s * pitch:s * pitch + seg, :] for j in range(nlb)], axis=1)
        p_loc = jnp.concatenate([a_buf[j, s * pitch:s * pitch + seg, :] for j in range(nlb)], axis=1)
        out.append(h_loc + p_loc * state)
        state = h_end[s:s + 1] + p_end[s:s + 1] * state
    return jnp.concatenate(out, axis=0), state


def _mixer_kernel(x_ref, mod_ref, modp_ref, n1g_ref, win_ref, convw_ref, convb_ref, wgate_ref, bgate_ref, lam_ref,
                  lng_ref, lnb_ref, sguw_ref, sgub_ref, gl_ref, gs_ref, wout_ref, n2g_ref, rwt_ref, rb_ref,
                  h1_ref, xs_ref, pos_ref, cnt_ref,
                  xa_tail, h_carry, scan_a, scan_b, h1_prev, *, tiles_per_seq):
    t = x_ref.shape[0]
    i = pl.program_id(0)

    @pl.when(i == 0)
    def _():
        h1_prev[...] = jnp.zeros_like(h1_prev)

    @pl.when(i % tiles_per_seq == 0)
    def _():
        xa_tail[...] = jnp.zeros_like(xa_tail)
        h_carry[...] = jnp.zeros_like(h_carry)

    z2b, logits = _router_logits(h1_prev[...], modp_ref[...], n2g_ref, rwt_ref, rb_ref)
    pos_rows, gate_pieces = _route(logits, pos_ref, cnt_ref)

    mod = mod_ref[...]
    sh1, sc1, g1 = mod[0:1], mod[1:2], mod[2:3]

    x = x_ref[...]
    z = _rms(x, n1g_ref[...] * (1.0 + sc1)) + sh1
    proj = jnp.dot(z.astype(BF16), win_ref[...], preferred_element_type=F32)
    perm_blocks = _perm_blocks(pos_rows, TILE_ROWS, t)
    n_pb = len(perm_blocks)
    sorted_blocks = iter(range(n_pb))

    def sort_next():
        q = next(sorted_blocks)
        _sort_rows(perm_blocks, z2b, gate_pieces, xs_ref, q, q + 1)

    sort_next()
    xa = proj[:, 0:D_LRU]
    ya = proj[:, D_LRU:2 * D_LRU]
    u = proj[:, 2 * D_LRU:2 * D_LRU + D_SGU]
    v = proj[:, 2 * D_LRU + D_SGU:]

    tail = xa_tail[...]
    row8 = lax.broadcasted_iota(I32, (8, 1), 0)
    xc = xa * convw_ref[CONV_WIDTH - 1:CONV_WIDTH, :] + convb_ref[...]
    for sft in range(1, CONV_WIDTH):
        rolled = pltpu.roll(xa, sft, 0)
        head = jnp.where(row8 < sft, pltpu.roll(tail, sft, 0), rolled[0:8])
        shifted = jnp.concatenate([head, rolled[8:]], axis=0)
        xc = xc + shifted * convw_ref[CONV_WIDTH - 1 - sft:CONV_WIDTH - sft, :]
    xa_tail[...] = xa[t - 8:t]
    sort_next()

    xcb = xc.astype(BF16)
    hw = D_LRU // 2
    g_lo = jnp.dot(xcb[:, 0:hw], wgate_ref[0], preferred_element_type=F32)
    g_hi = jnp.dot(xcb[:, hw:], wgate_ref[1], preferred_element_type=F32)
    sort_next()
    r_gate = _sigmoid(jnp.concatenate([g_lo[:, 0:hw], g_hi[:, 0:hw]], axis=1) + bgate_ref[:, 0:D_LRU])
    i_gate = _sigmoid(jnp.concatenate([g_lo[:, hw:], g_hi[:, hw:]], axis=1) + bgate_ref[:, D_LRU:])
    sort_next()
    nlam = -lam_ref[...]
    softplus = jnp.maximum(nlam, 0.0) + jnp.log1p(jnp.exp(-jnp.abs(nlam)))
    log_a = (-LRU_C) * r_gate * softplus
    a = jnp.exp(log_a)
    sort_next()
    om = -jnp.tanh(log_a) * (a * a + 1.0)
    mult = jnp.where(om > 0.0, om * lax.rsqrt(om), 0.0)
    bterm = mult * i_gate * xc
    sort_next()
    h, h_last = _linear_scan(a, bterm, h_carry[...], scan_a, scan_b)
    h_carry[...] = h_last
    sort_next()
    o_lru = _rms(h * _gelu_tanh(ya), gl_ref[...])
    sort_next()

    ug = _gelu_tanh(u)
    vg = _gelu_tanh(v)
    sort_next()
    assert next(sorted_blocks, None) is None
    mu = jnp.mean(vg, axis=-1, keepdims=True)
    vcen = vg - mu
    var = jnp.mean(vcen * vcen, axis=-1, keepdims=True)
    vn = (vcen * lax.rsqrt(var + EPS) * lng_ref[...] + lnb_ref[...]).astype(BF16)
    ri = lax.broadcasted_iota(I32, (CHUNK, CHUNK), 0)
    ci = lax.broadcasted_iota(I32, (CHUNK, CHUNK), 1)
    causal = ri >= ci
    lane = lax.broadcasted_iota(I32, (1, 2 * SGU_HEAD_DIM), 1)
    first_half = lane < SGU_HEAD_DIM
    pair_w = []
    for p in range(SGU_HEADS // 2):
        w0 = jnp.where(causal, sguw_ref[2 * p], 0.0).astype(BF16)
        w1 = jnp.where(causal, sguw_ref[2 * p + 1], 0.0).astype(BF16)
        pair_w.append(jnp.concatenate([w0, w1], axis=1))
    chunks = []
    zero = jnp.zeros((), BF16)
    for n in range(t // CHUNK):
        cols = []
        for p in range(SGU_HEADS // 2):
            blk = vn[n * CHUNK:(n + 1) * CHUNK, p * LANES:(p + 1) * LANES]
            rhs = jnp.concatenate([jnp.where(first_half, blk, zero), jnp.where(first_half, zero, blk)], axis=0)
            cols.append(jnp.dot(pair_w[p], rhs, preferred_element_type=F32))
        chunks.append(jnp.concatenate(cols, axis=1) + sgub_ref[...])
    mixed = jnp.concatenate(chunks, axis=0)
    o_sgu = _rms(ug * mixed, gs_ref[...])

    heads = jnp.concatenate([o_lru, o_sgu], axis=1).astype(BF16)
    h1 = x + g1 * jnp.dot(heads, wout_ref[...], preferred_element_type=F32)
    h1_ref[...] = h1
    h1_prev[...] = h1


def _router_logits(h1, mod, n2g_ref, rwt_ref, rb_ref):
    sh2, sc2 = mod[3:4], mod[4:5]
    z2b = (_rms(h1, n2g_ref[...] * (1.0 + sc2)) + sh2).astype(BF16)
    logits = lax.dot_general(rwt_ref[...].astype(BF16), z2b, (((1,), (1,)), ((), ())),
                             preferred_element_type=F32) + rb_ref[...]
    return z2b, logits


def _route(logits, pos_ref, cnt_ref):
    t = logits.shape[1]
    eidx = lax.broadcasted_iota(I32, (N_EXPERTS, t), 0)
    work = logits
    sel = []
    tops = []
    for k in range(TOP_K):
        m = jnp.max(work, axis=0, keepdims=True)
        idx = jnp.min(jnp.where(work == m, eidx, N_EXPERTS), axis=0, keepdims=True)
        onehot = eidx == idx
        sel.append(onehot)
        tops.append(m)
        work = jnp.where(onehot, -jnp.inf, work)
    exps = [jnp.exp(tk - tops[0]) for tk in tops]
    denom = exps[0] + exps[1] + exps[2] + exps[3]
    chosen = jnp.zeros((N_EXPERTS, t), F32)
    gsel = jnp.zeros((N_EXPERTS, t), F32)
    for k in range(TOP_K):
        chosen = jnp.where(sel[k], 1.0, chosen)
        gsel = jnp.where(sel[k], exps[k] / denom, gsel)

    chosen_b = chosen.astype(BF16)
    si = lax.broadcasted_iota(I32, (t, t), 0)
    ti = lax.broadcasted_iota(I32, (t, t), 1)
    before = jnp.where(si < ti, 1.0, 0.0).astype(BF16)
    excl = jnp.dot(chosen_b, before, preferred_element_type=F32)
    cnt_col = jnp.sum(chosen, axis=1, keepdims=True).astype(I32)
    cnt_ref[...] = jnp.broadcast_to(cnt_col, cnt_ref.shape)
    run_len = ((cnt_col + (ROW_ALIGN - 1)) // ROW_ALIGN * ROW_ALIGN).astype(F32)
    er = lax.broadcasted_iota(I32, (N_EXPERTS, N_EXPERTS), 0)
    ec = lax.broadcasted_iota(I32, (N_EXPERTS, N_EXPERTS), 1)
    lower = jnp.where(ec < er, 1.0, 0.0).astype(BF16)
    run_start = jnp.dot(lower, jnp.broadcast_to(run_len, (N_EXPERTS, LANES)).astype(BF16),
                        preferred_element_type=F32)[:, 0:1]
    posmat = excl + run_start
    pos_rows = []
    for k in range(TOP_K):
        pk = jnp.sum(jnp.where(sel[k], posmat, 0.0), axis=0, keepdims=True).astype(I32)
        pos_ref[k:k + 1, :] = pk
        pos_rows.append(pk)

    g_hi = gsel.astype(BF16)
    rem = gsel - g_hi.astype(F32)
    g_mid = rem.astype(BF16)
    g_lo = (rem - g_mid.astype(F32)).astype(BF16)
    gp = jnp.concatenate([g_hi, g_mid, g_lo, jnp.zeros((PAYLOAD - 3 * N_EXPERTS, t), BF16)], axis=0)
    return pos_rows, gp


def _sort_rows(perm_blocks, z2b, gate_pieces, xs_ref, q0, q1):
    for q in range(q0, q1):
        rows = slice(q * PERM_BLOCK, (q + 1) * PERM_BLOCK)
        xs_ref[rows, 0:D_MODEL] = jnp.dot(perm_blocks[q], z2b, preferred_element_type=F32)
        xs_ref[rows, D_MODEL:] = lax.dot_general(perm_blocks[q], gate_pieces, (((1,), (1,)), ((), ())),
                                                 preferred_element_type=F32)


def _mixer_call(x, mod3, n1g, win, convw, convb, wgate, bgate, lam, lng, lnb, sguw, sgub_full, gl, gs, wout,
                n2g, rwt, rb):
    bsz, seq, d = x.shape
    t = SEQ_TILE
    tiles = seq // t
    n_tiles = bsz * tiles
    x2 = x.reshape(bsz * seq, d)

    def const(shape):
        return pl.BlockSpec(shape, lambda i: (0,) * len(shape))

    mixed = lambda i: jnp.minimum(i, n_tiles - 1)
    routed = lambda i: jnp.maximum(i - 1, 0)
    return pl.pallas_call(
        functools.partial(_mixer_kernel, tiles_per_seq=tiles),
        grid=(n_tiles + 1,),
        in_specs=[
            pl.BlockSpec((t, d), lambda i: (mixed(i), 0)),
            pl.BlockSpec((None, 6, d), lambda i: (mixed(i) // tiles, 0, 0)),
            pl.BlockSpec((None, 6, d), lambda i: (routed(i) // tiles, 0, 0)),
            const((1, d)),
            const((d, 2 * d)),
            const((CONV_WIDTH, D_LRU)),
            const((1, D_LRU)),
            const((2, D_LRU // 2, D_LRU)),
            const((1, 2 * D_LRU)),
            const((1, D_LRU)),
            const((1, D_SGU)),
            const((1, D_SGU)),
            const((SGU_HEADS, CHUNK, CHUNK)),
            const((CHUNK, D_SGU)),
            const((1, D_LRU)),
            const((1, D_SGU)),
            const((d, d)),
            const((1, d)),
            const((N_EXPERTS, d)),
            const((N_EXPERTS, 1)),
        ],
        out_specs=[
            pl.BlockSpec((t, d), lambda i: (i, 0)),
            pl.BlockSpec((TILE_ROWS, ROW_W), lambda i: (routed(i), 0)),
            pl.BlockSpec((TOP_K, t), lambda i: (0, routed(i))),
            pl.BlockSpec((None, N_EXPERTS, LANES), lambda i: (routed(i), 0, 0)),
        ],
        out_shape=[
            jax.ShapeDtypeStruct(((n_tiles + 1) * t, d), F32),
            jax.ShapeDtypeStruct((n_tiles * TILE_ROWS, ROW_W), F32),
            jax.ShapeDtypeStruct((TOP_K, n_tiles * t), I32),
            jax.ShapeDtypeStruct((n_tiles, N_EXPERTS, LANES), I32),
        ],
        scratch_shapes=[pltpu.VMEM((8, D_LRU), F32), pltpu.VMEM((1, D_LRU), F32),
                        pltpu.VMEM((D_LRU // LANES, SCAN_SEGMENTS * _scan_pitch(t), LANES), F32),
                        pltpu.VMEM((D_LRU // LANES, SCAN_SEGMENTS * _scan_pitch(t), LANES), F32),
                        pltpu.VMEM((t, d), F32)],
        compiler_params=pltpu.CompilerParams(
            dimension_semantics=("arbitrary",), vmem_limit_bytes=VMEM_LIMIT_BYTES),
        name="mixer_router",
    )(x2, mod3, mod3, n1g, win, convw, convb, wgate, bgate, lam, lng, lnb, sguw, sgub_full, gl, gs, wout, n2g, rwt,
      rb)


def _expert_kernel(be_ref, first_ref, nval_ref, jrow_ref, ilo_ref, ihi_ref, nexte_ref, nb_ref,
                   runsrc_ref, rundst_ref, runn_ref, tailrow_ref, taillen_ref,
                   xs_hbm, wgu_hbm, bgu_ref, wd_hbm, bd_ref,
                   ys_hbm,
                   xbuf, ybuf, zbuf, wgu_st, wd_st, wgu_bf, wd_bf, gsem, ssem, zsem, wsem):
    r = ROW_BLOCK
    n_tiles = tailrow_ref.shape[0]
    nb = nb_ref[0]

    rows = lambda v: pl.multiple_of(v, ROW_ALIGN)

    def weight_copies(e):
        return (pltpu.make_async_copy(wgu_hbm.at[e], wgu_st, wsem.at[0]),
                pltpu.make_async_copy(wd_hbm.at[e], wd_st, wsem.at[1]))

    def tail_copy(i):
        n = rows(taillen_ref[i])
        return pltpu.make_async_copy(zbuf.at[pl.ds(0, n), :], ys_hbm.at[pl.ds(rows(tailrow_ref[i]), n), :],
                                     zsem.at[0])

    def for_each_run(blk, fn):
        first = ilo_ref[blk]
        count = ihi_ref[blk] - first + 1
        last_entry = runn_ref.shape[0] - 1

        def pair_body(p, c):
            k0 = blk * n_tiles + first + 2 * p
            k1 = jnp.minimum(k0 + 1, last_entry)
            n0 = runn_ref[k0]
            n1 = jnp.where(2 * p + 1 < count, runn_ref[k1], 0)
            s0, d0, s1, d1 = runsrc_ref[k0], rundst_ref[k0], runsrc_ref[k1], rundst_ref[k1]

            @pl.when(n0 > 0)
            def _():
                fn(rows(s0), rows(d0), rows(n0))

            @pl.when(n1 > 0)
            def _():
                fn(rows(s1), rows(d1), rows(n1))
            return c
        lax.fori_loop(0, (count + 1) // 2, pair_body, 0)

    def start_gather(dst_slot):
        def fn(hbm_row, buf_row, n):
            pltpu.make_async_copy(xs_hbm.at[pl.ds(hbm_row, n), :],
                                  xbuf.at[dst_slot, pl.ds(buf_row, n), :], gsem.at[dst_slot]).start()
        return fn

    def start_scatter(src_slot):
        def fn(hbm_row, buf_row, n):
            pltpu.make_async_copy(ybuf.at[src_slot, pl.ds(buf_row, n), :],
                                  ys_hbm.at[pl.ds(hbm_row, n), :], ssem.at[src_slot]).start()
        return fn

    def wait_gather(s, n):
        pltpu.make_async_copy(xs_hbm.at[pl.ds(0, rows(n)), :], xbuf.at[s, pl.ds(0, rows(n)), :], gsem.at[s]).wait()

    def wait_scatter(s, n):
        pltpu.make_async_copy(ybuf.at[s, pl.ds(0, rows(n)), :], ys_hbm.at[pl.ds(0, rows(n)), :], ssem.at[s]).wait()

    for c in weight_copies(be_ref[0]):
        c.start()
    xbuf[...] = jnp.zeros_like(xbuf)
    for_each_run(0, start_gather(0))
    zbuf[...] = jnp.zeros_like(zbuf)

    def start_tail(i, c):
        tail_copy(i).start()
        return c
    lax.fori_loop(0, n_tiles, start_tail, 0)

    def block(b, carry):
        slot = b & 1
        wait_gather(slot, nval_ref[b])

        @pl.when(b + 1 < nb)
        def _():
            for_each_run(b + 1, start_gather(1 - slot))

        @pl.when(first_ref[b] == 1)
        def _():
            for c in weight_copies(be_ref[b]):
                c.wait()
            wgu_bf[...] = wgu_st[...].astype(BF16)
            wd_bf[...] = wd_st[...].astype(BF16)

            @pl.when(nexte_ref[b] >= 0)
            def _():
                for c in weight_copies(nexte_ref[b]):
                    c.start()

        @pl.when(b >= 2)
        def _():
            wait_scatter(slot, nval_ref[jnp.maximum(b - 2, 0)])

        def expert_rows(m):
            xw = xbuf[slot, 0:m]
            xb = xw[:, 0:D_MODEL].astype(BF16)
            lane = lax.broadcasted_iota(I32, (1, PAYLOAD), 1)
            gate = jnp.sum(jnp.where((lane & (N_EXPERTS - 1)) == be_ref[b], xw[:, D_MODEL:], 0.0),
                           axis=1, keepdims=True)
            gu = jnp.dot(xb, wgu_bf[...], preferred_element_type=F32) + bgu_ref[be_ref[b]]
            g = jnp.minimum(gu[:, 0:D_MODEL], SWIGLU_LIMIT)
            lin = jnp.clip(gu[:, D_MODEL:], -SWIGLU_LIMIT, SWIGLU_LIMIT)
            act = g * _sigmoid(SWIGLU_ALPHA * g) * (lin + 1.0)
            y = jnp.dot(act.astype(BF16), wd_bf[...], preferred_element_type=F32) + bd_ref[be_ref[b]]
            ybuf[slot, 0:m] = _pack_bf16_pairs(y * gate)

        quarter = r // ROW_PATHS
        for k in range(1, ROW_PATHS + 1):
            @pl.when((nval_ref[b] > (k - 1) * quarter) & (nval_ref[b] <= k * quarter))
            def _():
                expert_rows(k * quarter)

        for_each_run(b, start_scatter(slot))
        return carry

    lax.fori_loop(0, nb, block, 0)

    last = nb - 1
    wait_scatter(last & 1, nval_ref[last])

    @pl.when(nb >= 2)
    def _():
        wait_scatter(1 - (last & 1), nval_ref[jnp.maximum(last - 1, 0)])

    def wait_tail(i, c):
        tail_copy(i).wait()
        return c
    lax.fori_loop(0, n_tiles, wait_tail, 0)


def _expert_call(tables, xs, w_gu, b_gu, w_down, b_down):
    nblk = tables[0].shape[0]
    r = ROW_BLOCK
    d = D_MODEL
    whole = lambda i, *_: (0, 0, 0)
    grid_spec = pltpu.PrefetchScalarGridSpec(
        num_scalar_prefetch=len(tables),
        grid=(1,),
        in_specs=[
            pl.BlockSpec(memory_space=pl.ANY),
            pl.BlockSpec(memory_space=pl.ANY),
            pl.BlockSpec((N_EXPERTS, 1, 2 * d), whole),
            pl.BlockSpec(memory_space=pl.ANY),
            pl.BlockSpec((N_EXPERTS, 1, d), whole),
        ],
        out_specs=pl.BlockSpec(memory_space=pl.ANY),
        scratch_shapes=[
            pltpu.VMEM((2, r, ROW_W), F32),
            pltpu.VMEM((2, r, d // 2), U32),
            pltpu.VMEM((ROW_ALIGN * N_EXPERTS, d // 2), U32),
            pltpu.VMEM((d, 2 * d), F32),
            pltpu.VMEM((d, d), F32),
            pltpu.VMEM((d, 2 * d), BF16),
            pltpu.VMEM((d, d), BF16),
            pltpu.SemaphoreType.DMA((2,)),
            pltpu.SemaphoreType.DMA((2,)),
            pltpu.SemaphoreType.DMA((1,)),
            pltpu.SemaphoreType.DMA((2,)),
        ],
    )
    return pl.pallas_call(
        _expert_kernel,
        grid_spec=grid_spec,
        out_shape=jax.ShapeDtypeStruct((xs.shape[0], d // 2), U32),
        compiler_params=pltpu.CompilerParams(
            dimension_semantics=("arbitrary",), vmem_limit_bytes=VMEM_LIMIT_BYTES),
        name="experts",
    )(*tables, xs, w_gu, b_gu, w_down, b_down)


def _final_kernel(h1_ref, mod_ref, ys_ref, pos_ref, fg_ref, o_ref):
    t = SEQ_TILE
    g2 = mod_ref[5:6, :]
    tn = (((0,), (0,)), ((), ()))
    for s in range(h1_ref.shape[0] // t):
        tok = slice(s * t, (s + 1) * t)
        pos_rows = [pos_ref[k:k + 1, tok] for k in range(TOP_K)]
        perm = jnp.concatenate(_perm_blocks(pos_rows, TILE_ROWS, t), axis=0)
        y_lo, y_hi = _unpack_bf16_pairs(ys_ref[s * TILE_ROWS:(s + 1) * TILE_ROWS, :])
        moe = jnp.concatenate([lax.dot_general(perm, y_lo, tn, preferred_element_type=F32),
                               lax.dot_general(perm, y_hi, tn, preferred_element_type=F32)], axis=1)
        o_ref[tok, :] = _rms(h1_ref[tok, :] + g2 * moe, fg_ref[...])


def _final_call(h1, mod3, ys, pos, final_g, bsz, seq):
    d = h1.shape[1]
    n_tok = bsz * seq
    per_step = FINAL_TILES_PER_STEP
    t = per_step * SEQ_TILE
    per_batch = seq // t
    out = pl.pallas_call(
        _final_kernel,
        grid=(n_tok // t,),
        in_specs=[
            pl.BlockSpec((t, d), lambda i: (i, 0)),
            pl.BlockSpec((None, 6, d), lambda i: (i // per_batch, 0, 0)),
            pl.BlockSpec((per_step * TILE_ROWS, d // 2), lambda i: (i, 0)),
            pl.BlockSpec((TOP_K, t), lambda i: (0, i)),
            pl.BlockSpec((1, d), lambda i: (0, 0)),
        ],
        out_specs=pl.BlockSpec((t, d), lambda i: (i, 0)),
        out_shape=jax.ShapeDtypeStruct((n_tok, d), F32),
        compiler_params=pltpu.CompilerParams(
            dimension_semantics=("arbitrary",), vmem_limit_bytes=VMEM_LIMIT_BYTES),
        name="combine_final",
    )(h1, mod3, ys, pos, final_g.reshape(1, d))
    return out.reshape(bsz, seq, d)


def _block_diag(w):
    h, i, o = w.shape
    eye = jnp.eye(h, dtype=w.dtype)
    return (w[:, :, None, :] * eye[:, None, :, None]).reshape(h * i, h * o)


def _route_tables(cnt):
    r = ROW_BLOCK
    n_tiles = cnt.shape[0]
    nblk = n_tiles * TILE_ROWS // r + N_EXPERTS
    cnt = (cnt + (ROW_ALIGN - 1)) // ROW_ALIGN * ROW_ALIGN
    ie = jnp.arange(N_EXPERTS, dtype=I32)
    it = jnp.arange(n_tiles, dtype=I32)
    e_before = (ie[:, None] < ie[None, :]).astype(I32)
    t_before = (it[:, None] < it[None, :]).astype(I32)
    total = jnp.sum(cnt, axis=0)
    cum = jnp.sum(t_before[:, :, None] * cnt[:, None, :], axis=0)
    seg_off = jnp.sum(cnt[:, :, None] * e_before[None, :, :], axis=1)
    base = seg_off + it[:, None] * TILE_ROWS
    nblk_e = (total + r - 1) // r
    blk_start = jnp.sum(nblk_e[:, None] * e_before, axis=0)
    blk_end = blk_start + nblk_e
    nb_used = jnp.sum(nblk_e)
    blk = jnp.arange(nblk, dtype=I32)
    used = blk < nb_used
    last_e = jnp.max(jnp.where(nblk_e > 0, ie, 0))
    be = jnp.minimum(jnp.sum((blk[:, None] >= blk_end[None, :]).astype(I32), axis=1), N_EXPERTS - 1)
    be = jnp.where(used, be, last_e)
    be_onehot = (be[:, None] == ie[None, :]).astype(I32)
    pick = lambda v: jnp.sum(be_onehot * v[None, :], axis=1)
    jrow = jnp.where(used, (blk - pick(blk_start)) * r, 0)
    nval = jnp.where(used, jnp.clip(pick(total) - jrow, 0, r), 0)
    first = jnp.concatenate([jnp.ones((1,), I32), (be[1:] != be[:-1]).astype(I32)])
    nxt = pick(blk_end)
    next_e = jnp.where(nxt < nb_used, jnp.sum((nxt[:, None] >= blk_end[None, :]).astype(I32), axis=1), -1)
    run_start = jnp.sum(be_onehot[:, None, :] * cum[None, :, :], axis=-1)
    run_end = run_start + jnp.sum(be_onehot[:, None, :] * cnt[None, :, :], axis=-1)
    ilo = jnp.sum((run_end <= jrow[:, None]).astype(I32), axis=1)
    ihi = n_tiles - 1 - jnp.sum((run_start >= (jrow + r)[:, None]).astype(I32), axis=1)
    lo = jnp.maximum(run_start, jrow[:, None])
    hi = jnp.minimum(run_end, (jrow + r)[:, None])
    run_n = jnp.where(used[:, None], jnp.clip(hi - lo, 0, r), 0)
    run_src = jnp.sum(be_onehot[:, None, :] * base[None, :, :], axis=-1) + (lo - run_start)
    run_dst = lo - jrow[:, None]
    used_rows = jnp.sum(cnt, axis=1)
    tail_row = jnp.arange(n_tiles, dtype=I32) * TILE_ROWS + used_rows
    tail_len = TILE_ROWS - used_rows
    i32 = lambda v: v.astype(I32)
    return (i32(be), i32(first), i32(nval), i32(jrow), i32(jnp.minimum(ilo, n_tiles - 1)), i32(ihi), i32(next_e),
            i32(nb_used).reshape(1), i32(run_src).reshape(-1), i32(run_dst).reshape(-1), i32(run_n).reshape(-1),
            i32(tail_row), i32(tail_len))


def kernel(x, c, ada_w, ada_b, norm1_g, w_in, conv_w, conv_b, lru_wr, lru_br, lru_wi, lru_bi, lru_lambda, sgu_ln_g, sgu_ln_b, sgu_w, sgu_b, gnorm_lru_g, gnorm_sgu_g, w_out, norm2_g, router_w, router_b, exp_w_gu, exp_b_gu, exp_w_down, exp_b_down, final_g):
    bsz, seq, d = x.shape
    depth = ada_w.shape[0]
    assert depth == 1 and d == D_MODEL and seq % SEQ_TILE == 0 and ROW_BLOCK >= ROW_ALIGN * N_EXPERTS
    l = 0
    mod = _adaln_call(c, ada_w[l], ada_b[l])
    mod3 = mod.reshape(bsz, 6, d)

    row = lambda v: v.reshape(1, -1)
    hw = D_LRU // 2
    wr_bd, wi_bd = _block_diag(lru_wr[l]), _block_diag(lru_wi[l])
    wgate = jnp.stack([jnp.concatenate([wr_bd[s:s + hw, s:s + hw], wi_bd[s:s + hw, s:s + hw]], axis=1)
                       for s in (0, hw)]).astype(BF16)
    bgate = jnp.concatenate([lru_br[l], lru_bi[l]]).reshape(1, -1)
    sgub_full = jnp.repeat(sgu_b[l].T, SGU_HEAD_DIM, axis=1)
    h1, xs, pos, cnt = _mixer_call(
        x, mod3, row(norm1_g[l]), w_in[l].astype(BF16), conv_w[l], row(conv_b[l]), wgate, bgate,
        row(lru_lambda[l]), row(sgu_ln_g[l]), row(sgu_ln_b[l]), sgu_w[l], sgub_full,
        row(gnorm_lru_g[l]), row(gnorm_sgu_g[l]), w_out[l].astype(BF16), row(norm2_g[l]),
        router_w[l].T, router_b[l].reshape(-1, 1))

    tables = _route_tables(cnt[:, :, 0])
    ys = _expert_call(tables, xs, exp_w_gu[l], exp_b_gu[l].reshape(N_EXPERTS, 1, -1),
                      exp_w_down[l], exp_b_down[l].reshape(N_EXPERTS, 1, -1))
    return _final_call(h1, mod3, ys, pos, final_g, bsz, seq)
```

```python
import functools

import jax
import jax.numpy as jnp
from jax import lax
from jax.experimental import pallas as pl
from jax.experimental.pallas import tpu as pltpu

F32 = jnp.float32
BF16 = jnp.bfloat16
I32 = jnp.int32
U32 = jnp.uint32

D_MODEL = 1024
D_LRU = 512
D_SGU = 512
LRU_HEADS = 8
CONV_WIDTH = 4
LRU_C = 8.0
SGU_HEADS = 8
SGU_HEAD_DIM = D_SGU // SGU_HEADS
CHUNK = 128
N_EXPERTS = 32
TOP_K = 4
SWIGLU_ALPHA = 1.702
SWIGLU_LIMIT = 7.0
EPS = 1e-6

LANES = 128
SEQ_TILE = 512
FINAL_TILES_PER_STEP = 2
ROW_ALIGN = 8
TILE_ROWS = TOP_K * SEQ_TILE + ROW_ALIGN * N_EXPERTS
PAYLOAD = LANES
ROW_W = D_MODEL + PAYLOAD
ROW_BLOCK = 768
ROW_PATHS = 3
V7X_VMEM_BYTES = 64 * 1024 * 1024
VMEM_LIMIT_BYTES = V7X_VMEM_BYTES * 7 // 8


def _sigmoid(x):
    return 1.0 / (1.0 + jnp.exp(-x))


def _gelu_tanh(x):
    return 0.5 * x * (1.0 + jnp.tanh(0.7978845608028654 * (x + 0.044715 * (x * x * x))))


def _rms(x, g):
    ms = jnp.mean(x * x, axis=-1, keepdims=True)
    return x * lax.rsqrt(ms + EPS) * g


def _pack_bf16_pairs(v):
    n = v.shape[1] // 2
    r = v.astype(BF16).astype(F32)
    bits = lax.bitcast_convert_type(r, U32)
    return (bits[:, 0:n] >> 16) | (bits[:, n:] & jnp.uint32(0xFFFF0000))


def _unpack_bf16_pairs(w):
    lo = lax.bitcast_convert_type(w << 16, F32).astype(BF16)
    hi = lax.bitcast_convert_type(w & jnp.uint32(0xFFFF0000), F32).astype(BF16)
    return lo, hi


PERM_BLOCK = 256


def _perm_blocks(pos_rows, n_rows, n_cols):
    blk = PERM_BLOCK
    rid = lax.broadcasted_iota(I32, (blk, n_cols), 0).astype(F32).astype(BF16)
    one = jnp.ones((), BF16)
    blocks = []
    for q in range(n_rows // blk):
        acc = jnp.zeros((blk, n_cols), BF16)
        for pk in pos_rows:
            in_blk = jnp.where(pk // blk == q, pk % blk, -1).astype(F32).astype(BF16)
            acc = jnp.where(rid == in_blk, one, acc)
        blocks.append(acc)
    return blocks


def _adaln_kernel(c_ref, w_ref, b_ref, o_ref):
    c = c_ref[...]
    ca = c * _sigmoid(c)
    o_ref[...] = jnp.dot(ca.astype(BF16), w_ref[...].astype(BF16), preferred_element_type=F32) + b_ref[...]


def _adaln_call(c, w, b):
    bsz, d = c.shape
    n_out = w.shape[1]
    return pl.pallas_call(
        _adaln_kernel,
        grid=(n_out // d,),
        in_specs=[
            pl.BlockSpec((bsz, d), lambda j: (0, 0)),
            pl.BlockSpec((d, d), lambda j: (0, j)),
            pl.BlockSpec((1, d), lambda j: (0, j)),
        ],
        out_specs=pl.BlockSpec((bsz, d), lambda j: (0, j)),
        out_shape=jax.ShapeDtypeStruct((bsz, n_out), F32),
        name="adaln",
    )(c, w, b.reshape(1, n_out))


SCAN_SEGMENTS = 8
SCAN_PITCH_PAD = 8


def _scan_pitch(t):
    return t // SCAN_SEGMENTS + SCAN_PITCH_PAD


def _linear_scan(a, b, h0, a_buf, b_buf):
    t, c = a.shape
    nseg = SCAN_SEGMENTS
    seg = t // nseg
    pitch = _scan_pitch(t)
    nlb = c // LANES
    for j in range(nlb):
        for s in range(nseg):
            a_buf[j, s * pitch:s * pitch + seg, :] = a[s * seg:(s + 1) * seg, j * LANES:(j + 1) * LANES]
            b_buf[j, s * pitch:s * pitch + seg, :] = b[s * seg:(s + 1) * seg, j * LANES:(j + 1) * LANES]

    def step(g, carry):
        hs, ps = carry
        new_h, new_p = [], []
        for j in range(nlb):
            view = (j, pl.ds(g, nseg, stride=pitch), slice(None))
            ag = a_buf[view]
            hj = ag * hs[j] + b_buf[view]
            pj = ag * ps[j]
            b_buf[view] = hj
            a_buf[view] = pj
            new_h.append(hj)
            new_p.append(pj)
        return tuple(new_h), tuple(new_p)

    init = (tuple(jnp.zeros((nseg, LANES), F32) for _ in range(nlb)),
            tuple(jnp.ones((nseg, LANES), F32) for _ in range(nlb)))
    h_end, p_end = lax.fori_loop(0, seg, step, init, unroll=True)
    h_end = jnp.concatenate(h_end, axis=1)
    p_end = jnp.concatenate(p_end, axis=1)
    state = h0
    out = []
    for s in range(nseg):
        h_loc = jnp.concatenate([b_buf[j, s * pitch:s * pitch + seg, :] for j in range(nlb)], axis=1)
        p_loc = jnp.concatenate([a_buf[j, s * pitch:s * pitch + seg, :] for j in range(nlb)], axis=1)
        out.append(h_loc + p_loc * state)
        state = h_end[s:s + 1] + p_end[s:s + 1] * state
    return jnp.concatenate(out, axis=0), state


def _mixer_kernel(x_ref, mod_ref, modp_ref, n1g_ref, win_ref, convw_ref, convb_ref, wgate_ref, bgate_ref, lam_ref,
                  lng_ref, lnb_ref, sguw_ref, sgub_ref, gl_ref, gs_ref, wout_ref, n2g_ref, rwt_ref, rb_ref,
                  h1_ref, xs_ref, pos_ref, cnt_ref,
                  xa_tail, h_carry, scan_a, scan_b, h1_prev, *, tiles_per_seq):
    t = x_ref.shape[0]
    i = pl.program_id(0)

    @pl.when(i == 0)
    def _():
        h1_prev[...] = jnp.zeros_like(h1_prev)

    @pl.when(i % tiles_per_seq == 0)
    def _():
        xa_tail[...] = jnp.zeros_like(xa_tail)
        h_carry[...] = jnp.zeros_like(h_carry)

    z2b, logits = _router_logits(h1_prev[...], modp_ref[...], n2g_ref, rwt_ref, rb_ref)
    pos_rows, gate_pieces = _route(logits, pos_ref, cnt_ref)

    mod = mod_ref[...]
    sh1, sc1, g1 = mod[0:1], mod[1:2], mod[2:3]

    x = x_ref[...]
    z = _rms(x, n1g_ref[...] * (1.0 + sc1)) + sh1
    proj = jnp.dot(z.astype(BF16), win_ref[...], preferred_element_type=F32)
    perm_blocks = _perm_blocks(pos_rows, TILE_ROWS, t)
    n_pb = len(perm_blocks)
    sorted_blocks = iter(range(n_pb))

    def sort_next():
        q = next(sorted_blocks)
        _sort_rows(perm_blocks, z2b, gate_pieces, xs_ref, q, q + 1)

    sort_next()
    xa = proj[:, 0:D_LRU]
    ya = proj[:, D_LRU:2 * D_LRU]
    u = proj[:, 2 * D_LRU:2 * D_LRU + D_SGU]
    v = proj[:, 2 * D_LRU + D_SGU:]

    tail = xa_tail[...]
    row8 = lax.broadcasted_iota(I32, (8, 1), 0)
    xc = xa * convw_ref[CONV_WIDTH - 1:CONV_WIDTH, :] + convb_ref[...]
    for sft in range(1, CONV_WIDTH):
        rolled = pltpu.roll(xa, sft, 0)
        head = jnp.where(row8 < sft, pltpu.roll(tail, sft, 0), rolled[0:8])
        shifted = jnp.concatenate([head, rolled[8:]], axis=0)
        xc = xc + shifted * convw_ref[CONV_WIDTH - 1 - sft:CONV_WIDTH - sft, :]
    xa_tail[...] = xa[t - 8:t]
    sort_next()

    xcb = xc.astype(BF16)
    hw = D_LRU // 2
    g_lo = jnp.dot(xcb[:, 0:hw], wgate_ref[0], preferred_element_type=F32)
    g_hi = jnp.dot(xcb[:, hw:], wgate_ref[1], preferred_element_type=F32)
    sort_next()
    r_gate = _sigmoid(jnp.concatenate([g_lo[:, 0:hw], g_hi[:, 0:hw]], axis=1) + bgate_ref[:, 0:D_LRU])
    i_gate = _sigmoid(jnp.concatenate([g_lo[:, hw:], g_hi[:, hw:]], axis=1) + bgate_ref[:, D_LRU:])
    sort_next()
    nlam = -lam_ref[...]
    softplus = jnp.maximum(nlam, 0.0) + jnp.log1p(jnp.exp(-jnp.abs(nlam)))
    log_a = (-LRU_C) * r_gate * softplus
    a = jnp.exp(log_a)
    sort_next()
    om = -jnp.tanh(log_a) * (a * a + 1.0)
    mult = jnp.where(om > 0.0, om * lax.rsqrt(om), 0.0)
    bterm = mult * i_gate * xc
    sort_next()
    h, h_last = _linear_scan(a, bterm, h_carry[...], scan_a, scan_b)
    h_carry[...] = h_last
    sort_next()
    o_lru = _rms(h * _gelu_tanh(ya), gl_ref[...])
    sort_next()

    ug = _gelu_tanh(u)
    vg = _gelu_tanh(v)
    sort_next()
    assert next(sorted_blocks, None) is None
    mu = jnp.mean(vg, axis=-1, keepdims=True)
    vcen = vg - mu
    var = jnp.mean(vcen * vcen, axis=-1, keepdims=True)
    vn = (vcen * lax.rsqrt(var + EPS) * lng_ref[...] + lnb_ref[...]).astype(BF16)
    ri = lax.broadcasted_iota(I32, (CHUNK, CHUNK), 0)
    ci = lax.broadcasted_iota(I32, (CHUNK, CHUNK), 1)
    causal = ri >= ci
    lane = lax.broadcasted_iota(I32, (1, 2 * SGU_HEAD_DIM), 1)
    first_half = lane < SGU_HEAD_DIM
    pair_w = []
    for p in range(SGU_HEADS // 2):
        w0 = jnp.where(causal, sguw_ref[2 * p], 0.0).astype(BF16)
        w1 = jnp.where(causal, sguw_ref[2 * p + 1], 0.0).astype(BF16)
        pair_w.append(jnp.concatenate([w0, w1], axis=1))
    chunks = []
    zero = jnp.zeros((), BF16)
    for n in range(t // CHUNK):
        cols = []
        for p in range(SGU_HEADS // 2):
            blk = vn[n * CHUNK:(n + 1) * CHUNK, p * LANES:(p + 1) * LANES]
            rhs = jnp.concatenate([jnp.where(first_half, blk, zero), jnp.where(first_half, zero, blk)], axis=0)
            cols.append(jnp.dot(pair_w[p], rhs, preferred_element_type=F32))
        chunks.append(jnp.concatenate(cols, axis=1) + sgub_ref[...])
    mixed = jnp.concatenate(chunks, axis=0)
    o_sgu = _rms(ug * mixed, gs_ref[...])

    heads = jnp.concatenate([o_lru, o_sgu], axis=1).astype(BF16)
    h1 = x + g1 * jnp.dot(heads, wout_ref[...], preferred_element_type=F32)
    h1_ref[...] = h1
    h1_prev[...] = h1


def _router_logits(h1, mod, n2g_ref, rwt_ref, rb_ref):
    sh2, sc2 = mod[3:4], mod[4:5]
    z2b = (_rms(h1, n2g_ref[...] * (1.0 + sc2)) + sh2).astype(BF16)
    logits = lax.dot_general(rwt_ref[...].astype(BF16), z2b, (((1,), (1,)), ((), ())),
                             preferred_element_type=F32) + rb_ref[...]
    return z2b, logits


def _route(logits, pos_ref, cnt_ref):
    t = logits.shape[1]
    eidx = lax.broadcasted_iota(I32, (N_EXPERTS, t), 0)
    work = logits
    sel = []
    tops = []
    for k in range(TOP_K):
        m = jnp.max(work, axis=0, keepdims=True)
        idx = jnp.min(jnp.where(work == m, eidx, N_EXPERTS), axis=0, keepdims=True)
        onehot = eidx == idx
        sel.append(onehot)
        tops.append(m)
        work = jnp.where(onehot, -jnp.inf, work)
    exps = [jnp.exp(tk - tops[0]) for tk in tops]
    denom = exps[0] + exps[1] + exps[2] + exps[3]
    chosen = jnp.zeros((N_EXPERTS, t), F32)
    gsel = jnp.zeros((N_EXPERTS, t), F32)
    for k in range(TOP_K):
        chosen = jnp.where(sel[k], 1.0, chosen)
        gsel = jnp.where(sel[k], exps[k] / denom, gsel)

    chosen_b = chosen.astype(BF16)
    si = lax.broadcasted_iota(I32, (t, t), 0)
    ti = lax.broadcasted_iota(I32, (t, t), 1)
    before = jnp.where(si < ti, 1.0, 0.0).astype(BF16)
    excl = jnp.dot(chosen_b, before, preferred_element_type=F32)
    cnt_col = jnp.sum(chosen, axis=1, keepdims=True).astype(I32)
    cnt_ref[...] = jnp.broadcast_to(cnt_col, cnt_ref.shape)
    run_len = ((cnt_col + (ROW_ALIGN - 1)) // ROW_ALIGN * ROW_ALIGN).astype(F32)
    er = lax.broadcasted_iota(I32, (N_EXPERTS, N_EXPERTS), 0)
    ec = lax.broadcasted_iota(I32, (N_EXPERTS, N_EXPERTS), 1)
    lower = jnp.where(ec < er, 1.0, 0.0).astype(BF16)
    run_start = jnp.dot(lower, jnp.broadcast_to(run_len, (N_EXPERTS, LANES)).astype(BF16),
                        preferred_element_type=F32)[:, 0:1]
    posmat = excl + run_start
    pos_rows = []
    for k in range(TOP_K):
        pk = jnp.sum(jnp.where(sel[k], posmat, 0.0), axis=0, keepdims=True).astype(I32)
        pos_ref[k:k + 1, :] = pk
        pos_rows.append(pk)

    g_hi = gsel.astype(BF16)
    rem = gsel - g_hi.astype(F32)
    g_mid = rem.astype(BF16)
    g_lo = (rem - g_mid.astype(F32)).astype(BF16)
    gp = jnp.concatenate([g_hi, g_mid, g_lo, jnp.zeros((PAYLOAD - 3 * N_EXPERTS, t), BF16)], axis=0)
    return pos_rows, gp


def _sort_rows(perm_blocks, z2b, gate_pieces, xs_ref, q0, q1):
    for q in range(q0, q1):
        rows = slice(q * PERM_BLOCK, (q + 1) * PERM_BLOCK)
        xs_ref[rows, 0:D_MODEL] = jnp.dot(perm_blocks[q], z2b, preferred_element_type=F32)
        xs_ref[rows, D_MODEL:] = lax.dot_general(perm_blocks[q], gate_pieces, (((1,), (1,)), ((), ())),
                                                 preferred_element_type=F32)


def _mixer_call(x, mod3, n1g, win, convw, convb, wgate, bgate, lam, lng, lnb, sguw, sgub_full, gl, gs, wout,
                n2g, rwt, rb):
    bsz, seq, d = x.shape
    t = SEQ_TILE
    tiles = seq // t
    n_tiles = bsz * tiles
    x2 = x.reshape(bsz * seq, d)

    def const(shape):
        return pl.BlockSpec(shape, lambda i: (0,) * len(shape))

    mixed = lambda i: jnp.minimum(i, n_tiles - 1)
    routed = lambda i: jnp.maximum(i - 1, 0)
    return pl.pallas_call(
        functools.partial(_mixer_kernel, tiles_per_seq=tiles),
        grid=(n_tiles + 1,),
        in_specs=[
            pl.BlockSpec((t, d), lambda i: (mixed(i), 0)),
            pl.BlockSpec((None, 6, d), lambda i: (mixed(i) // tiles, 0, 0)),
            pl.BlockSpec((None, 6, d), lambda i: (routed(i) // tiles, 0, 0)),
            const((1, d)),
            const((d, 2 * d)),
            const((CONV_WIDTH, D_LRU)),
            const((1, D_LRU)),
            const((2, D_LRU // 2, D_LRU)),
            const((1, 2 * D_LRU)),
            const((1, D_LRU)),
            const((1, D_SGU)),
            const((1, D_SGU)),
            const((SGU_HEADS, CHUNK, CHUNK)),
            const((CHUNK, D_SGU)),
            const((1, D_LRU)),
            const((1, D_SGU)),
            const((d, d)),
            const((1, d)),
            const((N_EXPERTS, d)),
            const((N_EXPERTS, 1)),
        ],
        out_specs=[
            pl.BlockSpec((t, d), lambda i: (i, 0)),
            pl.BlockSpec((TILE_ROWS, ROW_W), lambda i: (routed(i), 0)),
            pl.BlockSpec((TOP_K, t), lambda i: (0, routed(i))),
            pl.BlockSpec((None, N_EXPERTS, LANES), lambda i: (routed(i), 0, 0)),
        ],
        out_shape=[
            jax.ShapeDtypeStruct(((n_tiles + 1) * t, d), F32),
            jax.ShapeDtypeStruct((n_tiles * TILE_ROWS, ROW_W), F32),
            jax.ShapeDtypeStruct((TOP_K, n_tiles * t), I32),
            jax.ShapeDtypeStruct((n_tiles, N_EXPERTS, LANES), I32),
        ],
        scratch_shapes=[pltpu.VMEM((8, D_LRU), F32), pltpu.VMEM((1, D_LRU), F32),
                        pltpu.VMEM((D_LRU // LANES, SCAN_SEGMENTS * _scan_pitch(t), LANES), F32),
                        pltpu.VMEM((D_LRU // LANES, SCAN_SEGMENTS * _scan_pitch(t), LANES), F32),
                        pltpu.VMEM((t, d), F32)],
        compiler_params=pltpu.CompilerParams(
            dimension_semantics=("arbitrary",), vmem_limit_bytes=VMEM_LIMIT_BYTES),
        name="mixer_router",
    )(x2, mod3, mod3, n1g, win, convw, convb, wgate, bgate, lam, lng, lnb, sguw, sgub_full, gl, gs, wout, n2g, rwt,
      rb)


def _expert_kernel(be_ref, first_ref, nval_ref, jrow_ref, ilo_ref, ihi_ref, nexte_ref, nb_ref,
                   runsrc_ref, rundst_ref, runn_ref, tailrow_ref, taillen_ref,
                   xs_hbm, wgu_hbm, bgu_ref, wd_hbm, bd_ref,
                   ys_hbm,
                   xbuf, ybuf, zbuf, wgu_st, wd_st, wgu_bf, wd_bf, gsem, ssem, zsem, wsem):
    r = ROW_BLOCK
    n_tiles = tailrow_ref.shape[0]
    nb = nb_ref[0]

    rows = lambda v: pl.multiple_of(v, ROW_ALIGN)

    def weight_copies(e):
        return (pltpu.make_async_copy(wgu_hbm.at[e], wgu_st, wsem.at[0]),
                pltpu.make_async_copy(wd_hbm.at[e], wd_st, wsem.at[1]))

    def tail_copy(i):
        n = rows(taillen_ref[i])
        return pltpu.make_async_copy(zbuf.at[pl.ds(0, n), :], ys_hbm.at[pl.ds(rows(tailrow_ref[i]), n), :],
                                     zsem.at[0])

    def for_each_run(blk, fn):
        first = ilo_ref[blk]
        count = ihi_ref[blk] - first + 1
        last_entry = runn_ref.shape[0] - 1

        def pair_body(p, c):
            k0 = blk * n_tiles + first + 2 * p
            k1 = jnp.minimum(k0 + 1, last_entry)
            n0 = runn_ref[k0]
            n1 = jnp.where(2 * p + 1 < count, runn_ref[k1], 0)
            s0, d0, s1, d1 = runsrc_ref[k0], rundst_ref[k0], runsrc_ref[k1], rundst_ref[k1]

            @pl.when(n0 > 0)
            def _():
                fn(rows(s0), rows(d0), rows(n0))

            @pl.when(n1 > 0)
            def _():
                fn(rows(s1), rows(d1), rows(n1))
            return c
        lax.fori_loop(0, (count + 1) // 2, pair_body, 0)

    def start_gather(dst_slot):
        def fn(hbm_row, buf_row, n):
            pltpu.make_async_copy(xs_hbm.at[pl.ds(hbm_row, n), :],
                                  xbuf.at[dst_slot, pl.ds(buf_row, n), :], gsem.at[dst_slot]).start()
        return fn

    def start_scatter(src_slot):
        def fn(hbm_row, buf_row, n):
            pltpu.make_async_copy(ybuf.at[src_slot, pl.ds(buf_row, n), :],
                                  ys_hbm.at[pl.ds(hbm_row, n), :], ssem.at[src_slot]).start()
        return fn

    def wait_gather(s, n):
        pltpu.make_async_copy(xs_hbm.at[pl.ds(0, rows(n)), :], xbuf.at[s, pl.ds(0, rows(n)), :], gsem.at[s]).wait()

    def wait_scatter(s, n):
        pltpu.make_async_copy(ybuf.at[s, pl.ds(0, rows(n)), :], ys_hbm.at[pl.ds(0, rows(n)), :], ssem.at[s]).wait()

    for c in weight_copies(be_ref[0]):
        c.start()
    xbuf[...] = jnp.zeros_like(xbuf)
    for_each_run(0, start_gather(0))
    zbuf[...] = jnp.zeros_like(zbuf)

    def start_tail(i, c):
        tail_copy(i).start()
        return c
    lax.fori_loop(0, n_tiles, start_tail, 0)

    def block(b, carry):
        slot = b & 1
        wait_gather(slot, nval_ref[b])

        @pl.when(b + 1 < nb)
        def _():
            for_each_run(b + 1, start_gather(1 - slot))

        @pl.when(first_ref[b] == 1)
        def _():
            for c in weight_copies(be_ref[b]):
                c.wait()
            wgu_bf[...] = wgu_st[...].astype(BF16)
            wd_bf[...] = wd_st[...].astype(BF16)

            @pl.when(nexte_ref[b] >= 0)
            def _():
                for c in weight_copies(nexte_ref[b]):
                    c.start()

        @pl.when(b >= 2)
        def _():
            wait_scatter(slot, nval_ref[jnp.maximum(b - 2, 0)])

        def expert_rows(m):
            xw = xbuf[slot, 0:m]
            xb = xw[:, 0:D_MODEL].astype(BF16)
            lane = lax.broadcasted_iota(I32, (1, PAYLOAD), 1)
            gate = jnp.sum(jnp.where((lane & (N_EXPERTS - 1)) == be_ref[b], xw[:, D_MODEL:], 0.0),
                           axis=1, keepdims=True)
            gu = jnp.dot(xb, wgu_bf[...], preferred_element_type=F32) + bgu_ref[be_ref[b]]
            g = jnp.minimum(gu[:, 0:D_MODEL], SWIGLU_LIMIT)
            lin = jnp.clip(gu[:, D_MODEL:], -SWIGLU_LIMIT, SWIGLU_LIMIT)
            act = g * _sigmoid(SWIGLU_ALPHA * g) * (lin + 1.0)
            y = jnp.dot(act.astype(BF16), wd_bf[...], preferred_element_type=F32) + bd_ref[be_ref[b]]
            ybuf[slot, 0:m] = _pack_bf16_pairs(y * gate)

        quarter = r // ROW_PATHS
        for k in range(1, ROW_PATHS + 1):
            @pl.when((nval_ref[b] > (k - 1) * quarter) & (nval_ref[b] <= k * quarter))
            def _():
                expert_rows(k * quarter)

        for_each_run(b, start_scatter(slot))
        return carry

    lax.fori_loop(0, nb, block, 0)

    last = nb - 1
    wait_scatter(last & 1, nval_ref[last])

    @pl.when(nb >= 2)
    def _():
        wait_scatter(1 - (last & 1), nval_ref[jnp.maximum(last - 1, 0)])

    def wait_tail(i, c):
        tail_copy(i).wait()
        return c
    lax.fori_loop(0, n_tiles, wait_tail, 0)


def _expert_call(tables, xs, w_gu, b_gu, w_down, b_down):
    nblk = tables[0].shape[0]
    r = ROW_BLOCK
    d = D_MODEL
    whole = lambda i, *_: (0, 0, 0)
    grid_spec = pltpu.PrefetchScalarGridSpec(
        num_scalar_prefetch=len(tables),
        grid=(1,),
        in_specs=[
            pl.BlockSpec(memory_space=pl.ANY),
            pl.BlockSpec(memory_space=pl.ANY),
            pl.BlockSpec((N_EXPERTS, 1, 2 * d), whole),
            pl.BlockSpec(memory_space=pl.ANY),
            pl.BlockSpec((N_EXPERTS, 1, d), whole),
        ],
        out_specs=pl.BlockSpec(memory_space=pl.ANY),
        scratch_shapes=[
            pltpu.VMEM((2, r, ROW_W), F32),
            pltpu.VMEM((2, r, d // 2), U32),
            pltpu.VMEM((ROW_ALIGN * N_EXPERTS, d // 2), U32),
            pltpu.VMEM((d, 2 * d), F32),
            pltpu.VMEM((d, d), F32),
            pltpu.VMEM((d, 2 * d), BF16),
            pltpu.VMEM((d, d), BF16),
            pltpu.SemaphoreType.DMA((2,)),
            pltpu.SemaphoreType.DMA((2,)),
            pltpu.SemaphoreType.DMA((1,)),
            pltpu.SemaphoreType.DMA((2,)),
        ],
    )
    return pl.pallas_call(
        _expert_kernel,
        grid_spec=grid_spec,
        out_shape=jax.ShapeDtypeStruct((xs.shape[0], d // 2), U32),
        compiler_params=pltpu.CompilerParams(
            dimension_semantics=("arbitrary",), vmem_limit_bytes=VMEM_LIMIT_BYTES),
        name="experts",
    )(*tables, xs, w_gu, b_gu, w_down, b_down)


def _final_kernel(h1_ref, mod_ref, ys_ref, pos_ref, fg_ref, o_ref):
    t = SEQ_TILE
    g2 = mod_ref[5:6, :]
    tn = (((0,), (0,)), ((), ()))
    for s in range(h1_ref.shape[0] // t):
        tok = slice(s * t, (s + 1) * t)
        pos_rows = [pos_ref[k:k + 1, tok] for k in range(TOP_K)]
        perm = jnp.concatenate(_perm_blocks(pos_rows, TILE_ROWS, t), axis=0)
        y_lo, y_hi = _unpack_bf16_pairs(ys_ref[s * TILE_ROWS:(s + 1) * TILE_ROWS, :])
        moe = jnp.concatenate([lax.dot_general(perm, y_lo, tn, preferred_element_type=F32),
                               lax.dot_general(perm, y_hi, tn, preferred_element_type=F32)], axis=1)
        o_ref[tok, :] = _rms(h1_ref[tok, :] + g2 * moe, fg_ref[...])


def _final_call(h1, mod3, ys, pos, final_g, bsz, seq):
    d = h1.shape[1]
    n_tok = bsz * seq
    per_step = FINAL_TILES_PER_STEP
    t = per_step * SEQ_TILE
    per_batch = seq // t
    out = pl.pallas_call(
        _final_kernel,
        grid=(n_tok // t,),
        in_specs=[
            pl.BlockSpec((t, d), lambda i: (i, 0)),
            pl.BlockSpec((None, 6, d), lambda i: (i // per_batch, 0, 0)),
            pl.BlockSpec((per_step * TILE_ROWS, d // 2), lambda i: (i, 0)),
            pl.BlockSpec((TOP_K, t), lambda i: (0, i)),
            pl.BlockSpec((1, d), lambda i: (0, 0)),
        ],
        out_specs=pl.BlockSpec((t, d), lambda i: (i, 0)),
        out_shape=jax.ShapeDtypeStruct((n_tok, d), F32),
        compiler_params=pltpu.CompilerParams(
            dimension_semantics=("arbitrary",), vmem_limit_bytes=VMEM_LIMIT_BYTES),
        name="combine_final",
    )(h1, mod3, ys, pos, final_g.reshape(1, d))
    return out.reshape(bsz, seq, d)


def _block_diag(w):
    h, i, o = w.shape
    eye = jnp.eye(h, dtype=w.dtype)
    return (w[:, :, None, :] * eye[:, None, :, None]).reshape(h * i, h * o)


def _route_tables(cnt):
    r = ROW_BLOCK
    n_tiles = cnt.shape[0]
    nblk = n_tiles * TILE_ROWS // r + N_EXPERTS
    cnt = (cnt + (ROW_ALIGN - 1)) // ROW_ALIGN * ROW_ALIGN
    ie = jnp.arange(N_EXPERTS, dtype=I32)
    it = jnp.arange(n_tiles, dtype=I32)
    e_before = (ie[:, None] < ie[None, :]).astype(I32)
    t_before = (it[:, None] < it[None, :]).astype(I32)
    total = jnp.sum(cnt, axis=0)
    cum = jnp.sum(t_before[:, :, None] * cnt[:, None, :], axis=0)
    seg_off = jnp.sum(cnt[:, :, None] * e_before[None, :, :], axis=1)
    base = seg_off + it[:, None] * TILE_ROWS
    nblk_e = (total + r - 1) // r
    blk_start = jnp.sum(nblk_e[:, None] * e_before, axis=0)
    blk_end = blk_start + nblk_e
    nb_used = jnp.sum(nblk_e)
    blk = jnp.arange(nblk, dtype=I32)
    used = blk < nb_used
    last_e = jnp.max(jnp.where(nblk_e > 0, ie, 0))
    be = jnp.minimum(jnp.sum((blk[:, None] >= blk_end[None, :]).astype(I32), axis=1), N_EXPERTS - 1)
    be = jnp.where(used, be, last_e)
    be_onehot = (be[:, None] == ie[None, :]).astype(I32)
    pick = lambda v: jnp.sum(be_onehot * v[None, :], axis=1)
    jrow = jnp.where(used, (blk - pick(blk_start)) * r, 0)
    nval = jnp.where(used, jnp.clip(pick(total) - jrow, 0, r), 0)
    first = jnp.concatenate([jnp.ones((1,), I32), (be[1:] != be[:-1]).astype(I32)])
    nxt = pick(blk_end)
    next_e = jnp.where(nxt < nb_used, jnp.sum((nxt[:, None] >= blk_end[None, :]).astype(I32), axis=1), -1)
    run_start = jnp.sum(be_onehot[:, None, :] * cum[None, :, :], axis=-1)
    run_end = run_start + jnp.sum(be_onehot[:, None, :] * cnt[None, :, :], axis=-1)
    ilo = jnp.sum((run_end <= jrow[:, None]).astype(I32), axis=1)
    ihi = n_tiles - 1 - jnp.sum((run_start >= (jrow + r)[:, None]).astype(I32), axis=1)
    lo = jnp.maximum(run_start, jrow[:, None])
    hi = jnp.minimum(run_end, (jrow + r)[:, None])
    run_n = jnp.where(used[:, None], jnp.clip(hi - lo, 0, r), 0)
    run_src = jnp.sum(be_onehot[:, None, :] * base[None, :, :], axis=-1) + (lo - run_start)
    run_dst = lo - jrow[:, None]
    used_rows = jnp.sum(cnt, axis=1)
    tail_row = jnp.arange(n_tiles, dtype=I32) * TILE_ROWS + used_rows
    tail_len = TILE_ROWS - used_rows
    i32 = lambda v: v.astype(I32)
    return (i32(be), i32(first), i32(nval), i32(jrow), i32(jnp.minimum(ilo, n_tiles - 1)), i32(ihi), i32(next_e),
            i32(nb_used).reshape(1), i32(run_src).reshape(-1), i32(run_dst).reshape(-1), i32(run_n).reshape(-1),
            i32(tail_row), i32(tail_len))


def kernel(x, c, ada_w, ada_b, norm1_g, w_in, conv_w, conv_b, lru_wr, lru_br, lru_wi, lru_bi, lru_lambda, sgu_ln_g, sgu_ln_b, sgu_w, sgu_b, gnorm_lru_g, gnorm_sgu_g, w_out, norm2_g, router_w, router_b, exp_w_gu, exp_b_gu, exp_w_down, exp_b_down, final_g):
    bsz, seq, d = x.shape
    depth = ada_w.shape[0]
    assert depth == 1 and d == D_MODEL and seq % SEQ_TILE == 0 and ROW_BLOCK >= ROW_ALIGN * N_EXPERTS
    l = 0
    mod = _adaln_call(c, ada_w[l], ada_b[l])
    mod3 = mod.reshape(bsz, 6, d)

    row = lambda v: v.reshape(1, -1)
    hw = D_LRU // 2
    wr_bd, wi_bd = _block_diag(lru_wr[l]), _block_diag(lru_wi[l])
    wgate = jnp.stack([jnp.concatenate([wr_bd[s:s + hw, s:s + hw], wi_bd[s:s + hw, s:s + hw]], axis=1)
                       for s in (0, hw)]).astype(BF16)
    bgate = jnp.concatenate([lru_br[l], lru_bi[l]]).reshape(1, -1)
    sgub_full = jnp.repeat(sgu_b[l].T, SGU_HEAD_DIM, axis=1)
    h1, xs, pos, cnt = _mixer_call(
        x, mod3, row(norm1_g[l]), w_in[l].astype(BF16), conv_w[l], row(conv_b[l]), wgate, bgate,
        row(lru_lambda[l]), row(sgu_ln_g[l]), row(sgu_ln_b[l]), sgu_w[l], sgub_full,
        row(gnorm_lru_g[l]), row(gnorm_sgu_g[l]), w_out[l].astype(BF16), row(norm2_g[l]),
        router_w[l].T, router_b[l].reshape(-1, 1))

    tables = _route_tables(cnt[:, :, 0])
    ys = _expert_call(tables, xs, exp_w_gu[l], exp_b_gu[l].reshape(N_EXPERTS, 1, -1),
                      exp_w_down[l], exp_b_down[l].reshape(N_EXPERTS, 1, -1))
    return _final_call(h1, mod3, ys, pos, final_g, bsz, seq)
```

```python
import functools

import jax
import jax.numpy as jnp
from jax import lax
from jax.experimental import pallas as pl
from jax.experimental.pallas import tpu as pltpu

F32 = jnp.float32
BF16 = jnp.bfloat16
I32 = jnp.int32
U32 = jnp.uint32

D_MODEL = 1024
D_LRU = 512
D_SGU = 512
LRU_HEADS = 8
CONV_WIDTH = 4
LRU_C = 8.0
SGU_HEADS = 8
SGU_HEAD_DIM = D_SGU // SGU_HEADS
CHUNK = 128
N_EXPERTS = 32
TOP_K = 4
SWIGLU_ALPHA = 1.702
SWIGLU_LIMIT = 7.0
EPS = 1e-6

LANES = 128
SEQ_TILE = 512
FINAL_TILES_PER_STEP = 2
ROW_ALIGN = 8
TILE_ROWS = TOP_K * SEQ_TILE + ROW_ALIGN * N_EXPERTS
PAYLOAD = LANES
ROW_W = D_MODEL + PAYLOAD
ROW_BLOCK = 768
ROW_PATHS = 3
V7X_VMEM_BYTES = 64 * 1024 * 1024
VMEM_LIMIT_BYTES = V7X_VMEM_BYTES * 7 // 8


def _sigmoid(x):
    return 1.0 / (1.0 + jnp.exp(-x))


def _gelu_tanh(x):
    return 0.5 * x * (1.0 + jnp.tanh(0.7978845608028654 * (x + 0.044715 * (x * x * x))))


def _rms(x, g):
    ms = jnp.mean(x * x, axis=-1, keepdims=True)
    return x * lax.rsqrt(ms + EPS) * g


def _pack_bf16_pairs(v):
    n = v.shape[1] // 2
    r = v.astype(BF16).astype(F32)
    bits = lax.bitcast_convert_type(r, U32)
    return (bits[:, 0:n] >> 16) | (bits[:, n:] & jnp.uint32(0xFFFF0000))


def _unpack_bf16_pairs(w):
    lo = lax.bitcast_convert_type(w << 16, F32).astype(BF16)
    hi = lax.bitcast_convert_type(w & jnp.uint32(0xFFFF0000), F32).astype(BF16)
    return lo, hi


PERM_BLOCK = 256


def _perm_blocks(pos_rows, n_rows, n_cols):
    blk = PERM_BLOCK
    rid = lax.broadcasted_iota(I32, (blk, n_cols), 0).astype(F32).astype(BF16)
    one = jnp.ones((), BF16)
    blocks = []
    for q in range(n_rows // blk):
        acc = jnp.zeros((blk, n_cols), BF16)
        for pk in pos_rows:
            in_blk = jnp.where(pk // blk == q, pk % blk, -1).astype(F32).astype(BF16)
            acc = jnp.where(rid == in_blk, one, acc)
        blocks.append(acc)
    return blocks


def _adaln_kernel(c_ref, w_ref, b_ref, o_ref):
    c = c_ref[...]
    ca = c * _sigmoid(c)
    o_ref[...] = jnp.dot(ca.astype(BF16), w_ref[...].astype(BF16), preferred_element_type=F32) + b_ref[...]


def _adaln_call(c, w, b):
    bsz, d = c.shape
    n_out = w.shape[1]
    return pl.pallas_call(
        _adaln_kernel,
        grid=(n_out // d,),
        in_specs=[
            pl.BlockSpec((bsz, d), lambda j: (0, 0)),
            pl.BlockSpec((d, d), lambda j: (0, j)),
            pl.BlockSpec((1, d), lambda j: (0, j)),
        ],
        out_specs=pl.BlockSpec((bsz, d), lambda j: (0, j)),
        out_shape=jax.ShapeDtypeStruct((bsz, n_out), F32),
        name="adaln",
    )(c, w, b.reshape(1, n_out))


SCAN_SEGMENTS = 8
SCAN_PITCH_PAD = 8


def _scan_pitch(t):
    return t // SCAN_SEGMENTS + SCAN_PITCH_PAD


def _linear_scan(a, b, h0, a_buf, b_buf):
    t, c = a.shape
    nseg = SCAN_SEGMENTS
    seg = t // nseg
    pitch = _scan_pitch(t)
    nlb = c // LANES
    for j in range(nlb):
        for s in range(nseg):
            a_buf[j, s * pitch:s * pitch + seg, :] = a[s * seg:(s + 1) * seg, j * LANES:(j + 1) * LANES]
            b_buf[j, s * pitch:s * pitch + seg, :] = b[s * seg:(s + 1) * seg, j * LANES:(j + 1) * LANES]

    def step(g, carry):
        hs, ps = carry
        new_h, new_p = [], []
        for j in range(nlb):
            view = (j, pl.ds(g, nseg, stride=pitch), slice(None))
            ag = a_buf[view]
            hj = ag * hs[j] + b_buf[view]
            pj = ag * ps[j]
            b_buf[view] = hj
            a_buf[view] = pj
            new_h.append(hj)
            new_p.append(pj)
        return tuple(new_h), tuple(new_p)

    init = (tuple(jnp.zeros((nseg, LANES), F32) for _ in range(nlb)),
            tuple(jnp.ones((nseg, LANES), F32) for _ in range(nlb)))
    h_end, p_end = lax.fori_loop(0, seg, step, init, unroll=True)
    h_end = jnp.concatenate(h_end, axis=1)
    p_end = jnp.concatenate(p_end, axis=1)
    state = h0
    out = []
    for s in range(nseg):
        h_loc = jnp.concatenate([b_buf[j, s * pitch:s * pitch + seg, :] for j in range(nlb)], axis=1)
        p_loc = jnp.concatenate([a_buf[j, s * pitch:s * pitch + seg, :] for j in range(nlb)], axis=1)
        out.append(h_loc + p_loc * state)
        state = h_end[s:s + 1] + p_end[s:s + 1] * state
    return jnp.concatenate(out, axis=0), state


def _mixer_kernel(x_ref, mod_ref, modp_ref, n1g_ref, win_ref, convw_ref, convb_ref, wgate_ref, bgate_ref, lam_ref,
                  lng_ref, lnb_ref, sguw_ref, sgub_ref, gl_ref, gs_ref, wout_ref, n2g_ref, rwt_ref, rb_ref,
                  h1_ref, xs_ref, pos_ref, cnt_ref,
                  xa_tail, h_carry, scan_a, scan_b, h1_prev, *, tiles_per_seq):
    t = x_ref.shape[0]
    i = pl.program_id(0)

    @pl.when(i == 0)
    def _():
        h1_prev[...] = jnp.zeros_like(h1_prev)

    @pl.when(i % tiles_per_seq == 0)
    def _():
        xa_tail[...] = jnp.zeros_like(xa_tail)
        h_carry[...] = jnp.zeros_like(h_carry)

    z2b, logits = _router_logits(h1_prev[...], modp_ref[...], n2g_ref, rwt_ref, rb_ref)
    pos_rows, gate_pieces = _route(logits, pos_ref, cnt_ref)

    mod = mod_ref[...]
    sh1, sc1, g1 = mod[0:1], mod[1:2], mod[2:3]

    x = x_ref[...]
    z = _rms(x, n1g_ref[...] * (1.0 + sc1)) + sh1
    proj = jnp.dot(z.astype(BF16), win_ref[...], preferred_element_type=F32)
    perm_blocks = _perm_blocks(pos_rows, TILE_ROWS, t)
    n_pb = len(perm_blocks)
    sorted_blocks = iter(range(n_pb))

    def sort_next():
        q = next(sorted_blocks)
        _sort_rows(perm_blocks, z2b, gate_pieces, xs_ref, q, q + 1)

    sort_next()
    xa = proj[:, 0:D_LRU]
    ya = proj[:, D_LRU:2 * D_LRU]
    u = proj[:, 2 * D_LRU:2 * D_LRU + D_SGU]
    v = proj[:, 2 * D_LRU + D_SGU:]

    tail = xa_tail[...]
    row8 = lax.broadcasted_iota(I32, (8, 1), 0)
    xc = xa * convw_ref[CONV_WIDTH - 1:CONV_WIDTH, :] + convb_ref[...]
    for sft in range(1, CONV_WIDTH):
        rolled = pltpu.roll(xa, sft, 0)
        head = jnp.where(row8 < sft, pltpu.roll(tail, sft, 0), rolled[0:8])
        shifted = jnp.concatenate([head, rolled[8:]], axis=0)
        xc = xc + shifted * convw_ref[CONV_WIDTH - 1 - sft:CONV_WIDTH - sft, :]
    xa_tail[...] = xa[t - 8:t]
    sort_next()

    xcb = xc.astype(BF16)
    hw = D_LRU // 2
    g_lo = jnp.dot(xcb[:, 0:hw], wgate_ref[0], preferred_element_type=F32)
    g_hi = jnp.dot(xcb[:, hw:], wgate_ref[1], preferred_element_type=F32)
    sort_next()
    r_gate = _sigmoid(jnp.concatenate([g_lo[:, 0:hw], g_hi[:, 0:hw]], axis=1) + bgate_ref[:, 0:D_LRU])
    i_gate = _sigmoid(jnp.concatenate([g_lo[:, hw:], g_hi[:, hw:]], axis=1) + bgate_ref[:, D_LRU:])
    sort_next()
    nlam = -lam_ref[...]
    softplus = jnp.maximum(nlam, 0.0) + jnp.log1p(jnp.exp(-jnp.abs(nlam)))
    log_a = (-LRU_C) * r_gate * softplus
    a = jnp.exp(log_a)
    sort_next()
    om = -jnp.tanh(log_a) * (a * a + 1.0)
    mult = jnp.where(om > 0.0, om * lax.rsqrt(om), 0.0)
    bterm = mult * i_gate * xc
    sort_next()
    h, h_last = _linear_scan(a, bterm, h_carry[...], scan_a, scan_b)
    h_carry[...] = h_last
    sort_next()
    o_lru = _rms(h * _gelu_tanh(ya), gl_ref[...])
    sort_next()

    ug = _gelu_tanh(u)
    vg = _gelu_tanh(v)
    sort_next()
    assert next(sorted_blocks, None) is None
    mu = jnp.mean(vg, axis=-1, keepdims=True)
    vcen = vg - mu
    var = jnp.mean(vcen * vcen, axis=-1, keepdims=True)
    vn = (vcen * lax.rsqrt(var + EPS) * lng_ref[...] + lnb_ref[...]).astype(BF16)
    ri = lax.broadcasted_iota(I32, (CHUNK, CHUNK), 0)
    ci = lax.broadcasted_iota(I32, (CHUNK, CHUNK), 1)
    causal = ri >= ci
    lane = lax.broadcasted_iota(I32, (1, 2 * SGU_HEAD_DIM), 1)
    first_half = lane < SGU_HEAD_DIM
    pair_w = []
    for p in range(SGU_HEADS // 2):
        w0 = jnp.where(causal, sguw_ref[2 * p], 0.0).astype(BF16)
        w1 = jnp.where(causal, sguw_ref[2 * p + 1], 0.0).astype(BF16)
        pair_w.append(jnp.concatenate([w0, w1], axis=1))
    chunks = []
    zero = jnp.zeros((), BF16)
    for n in range(t // CHUNK):
        cols = []
        for p in range(SGU_HEADS // 2):
            blk = vn[n * CHUNK:(n + 1) * CHUNK, p * LANES:(p + 1) * LANES]
            rhs = jnp.concatenate([jnp.where(first_half, blk, zero), jnp.where(first_half, zero, blk)], axis=0)
            cols.append(jnp.dot(pair_w[p], rhs, preferred_element_type=F32))
        chunks.append(jnp.concatenate(cols, axis=1) + sgub_ref[...])
    mixed = jnp.concatenate(chunks, axis=0)
    o_sgu = _rms(ug * mixed, gs_ref[...])

    heads = jnp.concatenate([o_lru, o_sgu], axis=1).astype(BF16)
    h1 = x + g1 * jnp.dot(heads, wout_ref[...], preferred_element_type=F32)
    h1_ref[...] = h1
    h1_prev[...] = h1


def _router_logits(h1, mod, n2g_ref, rwt_ref, rb_ref):
    sh2, sc2 = mod[3:4], mod[4:5]
    z2b = (_rms(h1, n2g_ref[...] * (1.0 + sc2)) + sh2).astype(BF16)
    logits = lax.dot_general(rwt_ref[...].astype(BF16), z2b, (((1,), (1,)), ((), ())),
                             preferred_element_type=F32) + rb_ref[...]
    return z2b, logits


def _route(logits, pos_ref, cnt_ref):
    t = logits.shape[1]
    eidx = lax.broadcasted_iota(I32, (N_EXPERTS, t), 0)
    work = logits
    sel = []
    tops = []
    for k in range(TOP_K):
        m = jnp.max(work, axis=0, keepdims=True)
        idx = jnp.min(jnp.where(work == m, eidx, N_EXPERTS), axis=0, keepdims=True)
        onehot = eidx == idx
        sel.append(onehot)
        tops.append(m)
        work = jnp.where(onehot, -jnp.inf, work)
    exps = [jnp.exp(tk - tops[0]) for tk in tops]
    denom = exps[0] + exps[1] + exps[2] + exps[3]
    chosen = jnp.zeros((N_EXPERTS, t), F32)
    gsel = jnp.zeros((N_EXPERTS, t), F32)
    for k in range(TOP_K):
        chosen = jnp.where(sel[k], 1.0, chosen)
        gsel = jnp.where(sel[k], exps[k] / denom, gsel)

    chosen_b = chosen.astype(BF16)
    si = lax.broadcasted_iota(I32, (t, t), 0)
    ti = lax.broadcasted_iota(I32, (t, t), 1)
    before = jnp.where(si < ti, 1.0, 0.0).astype(BF16)
    excl = jnp.dot(chosen_b, before, preferred_element_type=F32)
    cnt_col = jnp.sum(chosen, axis=1, keepdims=True).astype(I32)
    cnt_ref[...] = jnp.broadcast_to(cnt_col, cnt_ref.shape)
    run_len = ((cnt_col + (ROW_ALIGN - 1)) // ROW_ALIGN * ROW_ALIGN).astype(F32)
    er = lax.broadcasted_iota(I32, (N_EXPERTS, N_EXPERTS), 0)
    ec = lax.broadcasted_iota(I32, (N_EXPERTS, N_EXPERTS), 1)
    lower = jnp.where(ec < er, 1.0, 0.0).astype(BF16)
    run_start = jnp.dot(lower, jnp.broadcast_to(run_len, (N_EXPERTS, LANES)).astype(BF16),
                        preferred_element_type=F32)[:, 0:1]
    posmat = excl + run_start
    pos_rows = []
    for k in range(TOP_K):
        pk = jnp.sum(jnp.where(sel[k], posmat, 0.0), axis=0, keepdims=True).astype(I32)
        pos_ref[k:k + 1, :] = pk
        pos_rows.append(pk)

    g_hi = gsel.astype(BF16)
    rem = gsel - g_hi.astype(F32)
    g_mid = rem.astype(BF16)
    g_lo = (rem - g_mid.astype(F32)).astype(BF16)
    gp = jnp.concatenate([g_hi, g_mid, g_lo, jnp.zeros((PAYLOAD - 3 * N_EXPERTS, t), BF16)], axis=0)
    return pos_rows, gp


def _sort_rows(perm_blocks, z2b, gate_pieces, xs_ref, q0, q1):
    for q in range(q0, q1):
        rows = slice(q * PERM_BLOCK, (q + 1) * PERM_BLOCK)
        xs_ref[rows, 0:D_MODEL] = jnp.dot(perm_blocks[q], z2b, preferred_element_type=F32)
        xs_ref[rows, D_MODEL:] = lax.dot_general(perm_blocks[q], gate_pieces, (((1,), (1,)), ((), ())),
                                                 preferred_element_type=F32)


def _mixer_call(x, mod3, n1g, win, convw, convb, wgate, bgate, lam, lng, lnb, sguw, sgub_full, gl, gs, wout,
                n2g, rwt, rb):
    bsz, seq, d = x.shape
    t = SEQ_TILE
    tiles = seq // t
    n_tiles = bsz * tiles
    x2 = x.reshape(bsz * seq, d)

    def const(shape):
        return pl.BlockSpec(shape, lambda i: (0,) * len(shape))

    mixed = lambda i: jnp.minimum(i, n_tiles - 1)
    routed = lambda i: jnp.maximum(i - 1, 0)
    return pl.pallas_call(
        functools.partial(_mixer_kernel, tiles_per_seq=tiles),
        grid=(n_tiles + 1,),
        in_specs=[
            pl.BlockSpec((t, d), lambda i: (mixed(i), 0)),
            pl.BlockSpec((None, 6, d), lambda i: (mixed(i) // tiles, 0, 0)),
            pl.BlockSpec((None, 6, d), lambda i: (routed(i) // tiles, 0, 0)),
            const((1, d)),
            const((d, 2 * d)),
            const((CONV_WIDTH, D_LRU)),
            const((1, D_LRU)),
            const((2, D_LRU // 2, D_LRU)),
            const((1, 2 * D_LRU)),
            const((1, D_LRU)),
            const((1, D_SGU)),
            const((1, D_SGU)),
            const((SGU_HEADS, CHUNK, CHUNK)),
            const((CHUNK, D_SGU)),
            const((1, D_LRU)),
            const((1, D_SGU)),
            const((d, d)),
            const((1, d)),
            const((N_EXPERTS, d)),
            const((N_EXPERTS, 1)),
        ],
        out_specs=[
            pl.BlockSpec((t, d), lambda i: (i, 0)),
            pl.BlockSpec((TILE_ROWS, ROW_W), lambda i: (routed(i), 0)),
            pl.BlockSpec((TOP_K, t), lambda i: (0, routed(i))),
            pl.BlockSpec((None, N_EXPERTS, LANES), lambda i: (routed(i), 0, 0)),
        ],
        out_shape=[
            jax.ShapeDtypeStruct(((n_tiles + 1) * t, d), F32),
            jax.ShapeDtypeStruct((n_tiles * TILE_ROWS, ROW_W), F32),
            jax.ShapeDtypeStruct((TOP_K, n_tiles * t), I32),
            jax.ShapeDtypeStruct((n_tiles, N_EXPERTS, LANES), I32),
        ],
        scratch_shapes=[pltpu.VMEM((8, D_LRU), F32), pltpu.VMEM((1, D_LRU), F32),
                        pltpu.VMEM((D_LRU // LANES, SCAN_SEGMENTS * _scan_pitch(t), LANES), F32),
                        pltpu.VMEM((D_LRU // LANES, SCAN_SEGMENTS * _scan_pitch(t), LANES), F32),
                        pltpu.VMEM((t, d), F32)],
        compiler_params=pltpu.CompilerParams(
            dimension_semantics=("arbitrary",), vmem_limit_bytes=VMEM_LIMIT_BYTES),
        name="mixer_router",
    )(x2, mod3, mod3, n1g, win, convw, convb, wgate, bgate, lam, lng, lnb, sguw, sgub_full, gl, gs, wout, n2g, rwt,
      rb)


def _expert_kernel(be_ref, first_ref, nval_ref, jrow_ref, ilo_ref, ihi_ref, nexte_ref, nb_ref,
                   runsrc_ref, rundst_ref, runn_ref, tailrow_ref, taillen_ref,
                   xs_hbm, wgu_hbm, bgu_ref, wd_hbm, bd_ref,
                   ys_hbm,
                   xbuf, ybuf, zbuf, wgu_st, wd_st, wgu_bf, wd_bf, gsem, ssem, zsem, wsem):
    r = ROW_BLOCK
    n_tiles = tailrow_ref.shape[0]
    nb = nb_ref[0]

    rows = lambda v: pl.multiple_of(v, ROW_ALIGN)

    def weight_copies(e):
        return (pltpu.make_async_copy(wgu_hbm.at[e], wgu_st, wsem.at[0]),
                pltpu.make_async_copy(wd_hbm.at[e], wd_st, wsem.at[1]))

    def tail_copy(i):
        n = rows(taillen_ref[i])
        return pltpu.make_async_copy(zbuf.at[pl.ds(0, n), :], ys_hbm.at[pl.ds(rows(tailrow_ref[i]), n), :],
                                     zsem.at[0])

    def for_each_run(blk, fn):
        first = ilo_ref[blk]
        count = ihi_ref[blk] - first + 1
        last_entry = runn_ref.shape[0] - 1

        def pair_body(p, c):
            k0 = blk * n_tiles + first + 2 * p
            k1 = jnp.minimum(k0 + 1, last_entry)
            n0 = runn_ref[k0]
            n1 = jnp.where(2 * p + 1 < count, runn_ref[k1], 0)
            s0, d0, s1, d1 = runsrc_ref[k0], rundst_ref[k0], runsrc_ref[k1], rundst_ref[k1]

            @pl.when(n0 > 0)
            def _():
                fn(rows(s0), rows(d0), rows(n0))

            @pl.when(n1 > 0)
            def _():
                fn(rows(s1), rows(d1), rows(n1))
            return c
        lax.fori_loop(0, (count + 1) // 2, pair_body, 0)

    def start_gather(dst_slot):
        def fn(hbm_row, buf_row, n):
            pltpu.make_async_copy(xs_hbm.at[pl.ds(hbm_row, n), :],
                                  xbuf.at[dst_slot, pl.ds(buf_row, n), :], gsem.at[dst_slot]).start()
        return fn

    def start_scatter(src_slot):
        def fn(hbm_row, buf_row, n):
            pltpu.make_async_copy(ybuf.at[src_slot, pl.ds(buf_row, n), :],
                                  ys_hbm.at[pl.ds(hbm_row, n), :], ssem.at[src_slot]).start()
        return fn

    def wait_gather(s, n):
        pltpu.make_async_copy(xs_hbm.at[pl.ds(0, rows(n)), :], xbuf.at[s, pl.ds(0, rows(n)), :], gsem.at[s]).wait()

    def wait_scatter(s, n):
        pltpu.make_async_copy(ybuf.at[s, pl.ds(0, rows(n)), :], ys_hbm.at[pl.ds(0, rows(n)), :], ssem.at[s]).wait()

    for c in weight_copies(be_ref[0]):
        c.start()
    xbuf[...] = jnp.zeros_like(xbuf)
    for_each_run(0, start_gather(0))
    zbuf[...] = jnp.zeros_like(zbuf)

    def start_tail(i, c):
        tail_copy(i).start()
        return c
    lax.fori_loop(0, n_tiles, start_tail, 0)

    def block(b, carry):
        slot = b & 1
        wait_gather(slot, nval_ref[b])

        @pl.when(b + 1 < nb)
        def _():
            for_each_run(b + 1, start_gather(1 - slot))

        @pl.when(first_ref[b] == 1)
        def _():
            for c in weight_copies(be_ref[b]):
                c.wait()
            wgu_bf[...] = wgu_st[...].astype(BF16)
            wd_bf[...] = wd_st[...].astype(BF16)

            @pl.when(nexte_ref[b] >= 0)
            def _():
                for c in weight_copies(nexte_ref[b]):
                    c.start()

        @pl.when(b >= 2)
        def _():
            wait_scatter(slot, nval_ref[jnp.maximum(b - 2, 0)])

        def expert_rows(m):
            xw = xbuf[slot, 0:m]
            xb = xw[:, 0:D_MODEL].astype(BF16)
            lane = lax.broadcasted_iota(I32, (1, PAYLOAD), 1)
            gate = jnp.sum(jnp.where((lane & (N_EXPERTS - 1)) == be_ref[b], xw[:, D_MODEL:], 0.0),
                           axis=1, keepdims=True)
            gu = jnp.dot(xb, wgu_bf[...], preferred_element_type=F32) + bgu_ref[be_ref[b]]
            g = jnp.minimum(gu[:, 0:D_MODEL], SWIGLU_LIMIT)
            lin = jnp.clip(gu[:, D_MODEL:], -SWIGLU_LIMIT, SWIGLU_LIMIT)
            act = g * _sigmoid(SWIGLU_ALPHA * g) * (lin + 1.0)
            y = jnp.dot(act.astype(BF16), wd_bf[...], preferred_element_type=F32) + bd_ref[be_ref[b]]
            ybuf[slot, 0:m] = _pack_bf16_pairs(y * gate)

        step = r // ROW_PATHS
        for k in range(1, ROW_PATHS + 1):
            @pl.when((nval_ref[b] > (k - 1) * step) & (nval_ref[b] <= k * step))
            def _():
                expert_rows(k * step)

        for_each_run(b, start_scatter(slot))
        return carry

    lax.fori_loop(0, nb, block, 0)

    last = nb - 1
    wait_scatter(last & 1, nval_ref[last])

    @pl.when(nb >= 2)
    def _():
        wait_scatter(1 - (last & 1), nval_ref[jnp.maximum(last - 1, 0)])

    def wait_tail(i, c):
        tail_copy(i).wait()
        return c
    lax.fori_loop(0, n_tiles, wait_tail, 0)


def _expert_call(tables, xs, w_gu, b_gu, w_down, b_down):
    nblk = tables[0].shape[0]
    r = ROW_BLOCK
    d = D_MODEL
    whole = lambda i, *_: (0, 0, 0)
    grid_spec = pltpu.PrefetchScalarGridSpec(
        num_scalar_prefetch=len(tables),
        grid=(1,),
        in_specs=[
            pl.BlockSpec(memory_space=pl.ANY),
            pl.BlockSpec(memory_space=pl.ANY),
            pl.BlockSpec((N_EXPERTS, 1, 2 * d), whole),
            pl.BlockSpec(memory_space=pl.ANY),
            pl.BlockSpec((N_EXPERTS, 1, d), whole),
        ],
        out_specs=pl.BlockSpec(memory_space=pl.ANY),
        scratch_shapes=[
            pltpu.VMEM((2, r, ROW_W), F32),
            pltpu.VMEM((2, r, d // 2), U32),
            pltpu.VMEM((ROW_ALIGN * N_EXPERTS, d // 2), U32),
            pltpu.VMEM((d, 2 * d), F32),
            pltpu.VMEM((d, d), F32),
            pltpu.VMEM((d, 2 * d), BF16),
            pltpu.VMEM((d, d), BF16),
            pltpu.SemaphoreType.DMA((2,)),
            pltpu.SemaphoreType.DMA((2,)),
            pltpu.SemaphoreType.DMA((1,)),
            pltpu.SemaphoreType.DMA((2,)),
        ],
    )
    return pl.pallas_call(
        _expert_kernel,
        grid_spec=grid_spec,
        out_shape=jax.ShapeDtypeStruct((xs.shape[0], d // 2), U32),
        compiler_params=pltpu.CompilerParams(
            dimension_semantics=("arbitrary",), vmem_limit_bytes=VMEM_LIMIT_BYTES),
        name="experts",
    )(*tables, xs, w_gu, b_gu, w_down, b_down)


def _final_kernel(h1_ref, mod_ref, ys_ref, pos_ref, fg_ref, o_ref):
    t = SEQ_TILE
    g2 = mod_ref[5:6, :]
    tn = (((0,), (0,)), ((), ()))
    for s in range(h1_ref.shape[0] // t):
        tok = slice(s * t, (s + 1) * t)
        pos_rows = [pos_ref[k:k + 1, tok] for k in range(TOP_K)]
        perm = jnp.concatenate(_perm_blocks(pos_rows, TILE_ROWS, t), axis=0)
        y_lo, y_hi = _unpack_bf16_pairs(ys_ref[s * TILE_ROWS:(s + 1) * TILE_ROWS, :])
        moe = jnp.concatenate([lax.dot_general(perm, y_lo, tn, preferred_element_type=F32),
                               lax.dot_general(perm, y_hi, tn, preferred_element_type=F32)], axis=1)
        o_ref[tok, :] = _rms(h1_ref[tok, :] + g2 * moe, fg_ref[...])


def _final_call(h1, mod3, ys, pos, final_g, bsz, seq):
    d = h1.shape[1]
    n_tok = bsz * seq
    per_step = FINAL_TILES_PER_STEP
    t = per_step * SEQ_TILE
    per_batch = seq // t
    out = pl.pallas_call(
        _final_kernel,
        grid=(n_tok // t,),
        in_specs=[
            pl.BlockSpec((t, d), lambda i: (i, 0)),
            pl.BlockSpec((None, 6, d), lambda i: (i // per_batch, 0, 0)),
            pl.BlockSpec((per_step * TILE_ROWS, d // 2), lambda i: (i, 0)),
            pl.BlockSpec((TOP_K, t), lambda i: (0, i)),
            pl.BlockSpec((1, d), lambda i: (0, 0)),
        ],
        out_specs=pl.BlockSpec((t, d), lambda i: (i, 0)),
        out_shape=jax.ShapeDtypeStruct((n_tok, d), F32),
        compiler_params=pltpu.CompilerParams(
            dimension_semantics=("arbitrary",), vmem_limit_bytes=VMEM_LIMIT_BYTES),
        name="combine_final",
    )(h1, mod3, ys, pos, final_g.reshape(1, d))
    return out.reshape(bsz, seq, d)


def _block_diag(w):
    h, i, o = w.shape
    eye = jnp.eye(h, dtype=w.dtype)
    return (w[:, :, None, :] * eye[:, None, :, None]).reshape(h * i, h * o)


def _route_tables(cnt):
    r = ROW_BLOCK
    n_tiles = cnt.shape[0]
    nblk = n_tiles * TILE_ROWS // r + N_EXPERTS
    cnt = (cnt + (ROW_ALIGN - 1)) // ROW_ALIGN * ROW_ALIGN
    ie = jnp.arange(N_EXPERTS, dtype=I32)
    it = jnp.arange(n_tiles, dtype=I32)
    e_before = (ie[:, None] < ie[None, :]).astype(I32)
    t_before = (it[:, None] < it[None, :]).astype(I32)
    total = jnp.sum(cnt, axis=0)
    cum = jnp.sum(t_before[:, :, None] * cnt[:, None, :], axis=0)
    seg_off = jnp.sum(cnt[:, :, None] * e_before[None, :, :], axis=1)
    base = seg_off + it[:, None] * TILE_ROWS
    nblk_e = (total + r - 1) // r
    blk_start = jnp.sum(nblk_e[:, None] * e_before, axis=0)
    blk_end = blk_start + nblk_e
    nb_used = jnp.sum(nblk_e)
    blk = jnp.arange(nblk, dtype=I32)
    used = blk < nb_used
    last_e = jnp.max(jnp.where(nblk_e > 0, ie, 0))
    be = jnp.minimum(jnp.sum((blk[:, None] >= blk_end[None, :]).astype(I32), axis=1), N_EXPERTS - 1)
    be = jnp.where(used, be, last_e)
    be_onehot = (be[:, None] == ie[None, :]).astype(I32)
    pick = lambda v: jnp.sum(be_onehot * v[None, :], axis=1)
    jrow = jnp.where(used, (blk - pick(blk_start)) * r, 0)
    nval = jnp.where(used, jnp.clip(pick(total) - jrow, 0, r), 0)
    first = jnp.concatenate([jnp.ones((1,), I32), (be[1:] != be[:-1]).astype(I32)])
    nxt = pick(blk_end)
    next_e = jnp.where(nxt < nb_used, jnp.sum((nxt[:, None] >= blk_end[None, :]).astype(I32), axis=1), -1)
    run_start = jnp.sum(be_onehot[:, None, :] * cum[None, :, :], axis=-1)
    run_end = run_start + jnp.sum(be_onehot[:, None, :] * cnt[None, :, :], axis=-1)
    ilo = jnp.sum((run_end <= jrow[:, None]).astype(I32), axis=1)
    ihi = n_tiles - 1 - jnp.sum((run_start >= (jrow + r)[:, None]).astype(I32), axis=1)
    lo = jnp.maximum(run_start, jrow[:, None])
    hi = jnp.minimum(run_end, (jrow + r)[:, None])
    run_n = jnp.where(used[:, None], jnp.clip(hi - lo, 0, r), 0)
    run_src = jnp.sum(be_onehot[:, None, :] * base[None, :, :], axis=-1) + (lo - run_start)
    run_dst = lo - jrow[:, None]
    used_rows = jnp.sum(cnt, axis=1)
    tail_row = jnp.arange(n_tiles, dtype=I32) * TILE_ROWS + used_rows
    tail_len = TILE_ROWS - used_rows
    i32 = lambda v: v.astype(I32)
    return (i32(be), i32(first), i32(nval), i32(jrow), i32(jnp.minimum(ilo, n_tiles - 1)), i32(ihi), i32(next_e),
            i32(nb_used).reshape(1), i32(run_src).reshape(-1), i32(run_dst).reshape(-1), i32(run_n).reshape(-1),
            i32(tail_row), i32(tail_len))


def kernel(x, c, ada_w, ada_b, norm1_g, w_in, conv_w, conv_b, lru_wr, lru_br, lru_wi, lru_bi, lru_lambda, sgu_ln_g, sgu_ln_b, sgu_w, sgu_b, gnorm_lru_g, gnorm_sgu_g, w_out, norm2_g, router_w, router_b, exp_w_gu, exp_b_gu, exp_w_down, exp_b_down, final_g):
    bsz, seq, d = x.shape
    depth = ada_w.shape[0]
    assert depth == 1 and d == D_MODEL and seq % SEQ_TILE == 0 and ROW_BLOCK >= ROW_ALIGN * N_EXPERTS
    l = 0
    mod = _adaln_call(c, ada_w[l], ada_b[l])
    mod3 = mod.reshape(bsz, 6, d)

    row = lambda v: v.reshape(1, -1)
    hw = D_LRU // 2
    wr_bd, wi_bd = _block_diag(lru_wr[l]), _block_diag(lru_wi[l])
    wgate = jnp.stack([jnp.concatenate([wr_bd[s:s + hw, s:s + hw], wi_bd[s:s + hw, s:s + hw]], axis=1)
                       for s in (0, hw)]).astype(BF16)
    bgate = jnp.concatenate([lru_br[l], lru_bi[l]]).reshape(1, -1)
    sgub_full = jnp.repeat(sgu_b[l].T, SGU_HEAD_DIM, axis=1)
    h1, xs, pos, cnt = _mixer_call(
        x, mod3, row(norm1_g[l]), w_in[l].astype(BF16), conv_w[l], row(conv_b[l]), wgate, bgate,
        row(lru_lambda[l]), row(sgu_ln_g[l]), row(sgu_ln_b[l]), sgu_w[l], sgub_full,
        row(gnorm_lru_g[l]), row(gnorm_sgu_g[l]), w_out[l].astype(BF16), row(norm2_g[l]),
        router_w[l].T, router_b[l].reshape(-1, 1))

    tables = _route_tables(cnt[:, :, 0])
    ys = _expert_call(tables, xs, exp_w_gu[l], exp_b_gu[l].reshape(N_EXPERTS, 1, -1),
                      exp_w_down[l], exp_b_down[l].reshape(N_EXPERTS, 1, -1))
    return _final_call(h1, mod3, ys, pos, final_g, bsz, seq)
```

```python
import functools

import jax
import jax.numpy as jnp
from jax import lax
from jax.experimental import pallas as pl
from jax.experimental.pallas import tpu as pltpu

F32 = jnp.float32
BF16 = jnp.bfloat16
I32 = jnp.int32
U32 = jnp.uint32

D_MODEL = 1024
D_LRU = 512
D_SGU = 512
LRU_HEADS = 8
CONV_WIDTH = 4
LRU_C = 8.0
SGU_HEADS = 8
SGU_HEAD_DIM = D_SGU // SGU_HEADS
CHUNK = 128
N_EXPERTS = 32
TOP_K = 4
SWIGLU_ALPHA = 1.702
SWIGLU_LIMIT = 7.0
EPS = 1e-6

LANES = 128
SEQ_TILE = 512
FINAL_TILES_PER_STEP = 2
ROW_ALIGN = 8
TILE_ROWS = TOP_K * SEQ_TILE + ROW_ALIGN * N_EXPERTS
ROW_W = D_MODEL
ROW_BLOCK = 768
ROW_PATHS = 3
V7X_VMEM_BYTES = 64 * 1024 * 1024
VMEM_LIMIT_BYTES = V7X_VMEM_BYTES * 7 // 8


def _sigmoid(x):
    return 1.0 / (1.0 + jnp.exp(-x))


def _gelu_tanh(x):
    return 0.5 * x * (1.0 + jnp.tanh(0.7978845608028654 * (x + 0.044715 * (x * x * x))))


def _rms(x, g):
    ms = jnp.mean(x * x, axis=-1, keepdims=True)
    return x * lax.rsqrt(ms + EPS) * g


def _pack_bf16_pairs(v):
    n = v.shape[1] // 2
    r = v.astype(BF16).astype(F32)
    bits = lax.bitcast_convert_type(r, U32)
    return (bits[:, 0:n] >> 16) | (bits[:, n:] & jnp.uint32(0xFFFF0000))


def _unpack_bf16_pairs(w):
    lo = lax.bitcast_convert_type(w << 16, F32).astype(BF16)
    hi = lax.bitcast_convert_type(w & jnp.uint32(0xFFFF0000), F32).astype(BF16)
    return lo, hi


PERM_BLOCK = 256


def _perm_blocks(pos_rows, n_rows, n_cols, values=None):
    blk = PERM_BLOCK
    rid = lax.broadcasted_iota(I32, (blk, n_cols), 0).astype(F32).astype(BF16)
    one = jnp.ones((), BF16)
    blocks = []
    for q in range(n_rows // blk):
        acc = jnp.zeros((blk, n_cols), BF16)
        for k, pk in enumerate(pos_rows):
            in_blk = jnp.where(pk // blk == q, pk % blk, -1).astype(F32).astype(BF16)
            acc = jnp.where(rid == in_blk, one if values is None else values[k], acc)
        blocks.append(acc)
    return blocks


def _adaln_kernel(c_ref, w_ref, b_ref, o_ref):
    c = c_ref[...]
    ca = c * _sigmoid(c)
    o_ref[...] = jnp.dot(ca.astype(BF16), w_ref[...].astype(BF16), preferred_element_type=F32) + b_ref[...]


def _adaln_call(c, w, b):
    bsz, d = c.shape
    n_out = w.shape[1]
    return pl.pallas_call(
        _adaln_kernel,
        grid=(n_out // d,),
        in_specs=[
            pl.BlockSpec((bsz, d), lambda j: (0, 0)),
            pl.BlockSpec((d, d), lambda j: (0, j)),
            pl.BlockSpec((1, d), lambda j: (0, j)),
        ],
        out_specs=pl.BlockSpec((bsz, d), lambda j: (0, j)),
        out_shape=jax.ShapeDtypeStruct((bsz, n_out), F32),
        name="adaln",
    )(c, w, b.reshape(1, n_out))


SCAN_SEGMENTS = 8
SCAN_PITCH_PAD = 8


def _scan_pitch(t):
    return t // SCAN_SEGMENTS + SCAN_PITCH_PAD


def _linear_scan(a, b, h0, a_buf, b_buf):
    t, c = a.shape
    nseg = SCAN_SEGMENTS
    seg = t // nseg
    pitch = _scan_pitch(t)
    nlb = c // LANES
    for j in range(nlb):
        for s in range(nseg):
            a_buf[j, s * pitch:s * pitch + seg, :] = a[s * seg:(s + 1) * seg, j * LANES:(j + 1) * LANES]
            b_buf[j, s * pitch:s * pitch + seg, :] = b[s * seg:(s + 1) * seg, j * LANES:(j + 1) * LANES]

    def step(g, carry):
        hs, ps = carry
        new_h, new_p = [], []
        for j in range(nlb):
            view = (j, pl.ds(g, nseg, stride=pitch), slice(None))
            ag = a_buf[view]
            hj = ag * hs[j] + b_buf[view]
            pj = ag * ps[j]
            b_buf[view] = hj
            a_buf[view] = pj
            new_h.append(hj)
            new_p.append(pj)
        return tuple(new_h), tuple(new_p)

    init = (tuple(jnp.zeros((nseg, LANES), F32) for _ in range(nlb)),
            tuple(jnp.ones((nseg, LANES), F32) for _ in range(nlb)))
    h_end, p_end = lax.fori_loop(0, seg, step, init, unroll=True)
    h_end = jnp.concatenate(h_end, axis=1)
    p_end = jnp.concatenate(p_end, axis=1)
    state = h0
    out = []
    for s in range(nseg):
        h_loc = jnp.concatenate([b_buf[j, s * pitch:s * pitch + seg, :] for j in range(nlb)], axis=1)
        p_loc = jnp.concatenate([a_buf[j, s * pitch:s * pitch + seg, :] for j in range(nlb)], axis=1)
        out.append(h_loc + p_loc * state)
        state = h_end[s:s + 1] + p_end[s:s + 1] * state
    return jnp.concatenate(out, axis=0), state


def _mixer_kernel(x_ref, mod_ref, modp_ref, n1g_ref, win_ref, convw_ref, convb_ref, wgate_ref, bgate_ref, lam_ref,
                  lng_ref, lnb_ref, sguw_ref, sgub_ref, gl_ref, gs_ref, wout_ref, n2g_ref, rwt_ref, rb_ref,
                  h1_ref, xs_ref, pos_ref, gate_ref, cnt_ref,
                  xa_tail, h_carry, scan_a, scan_b, h1_prev, *, tiles_per_seq):
    t = x_ref.shape[0]
    i = pl.program_id(0)

    @pl.when(i == 0)
    def _():
        h1_prev[...] = jnp.zeros_like(h1_prev)

    @pl.when(i % tiles_per_seq == 0)
    def _():
        xa_tail[...] = jnp.zeros_like(xa_tail)
        h_carry[...] = jnp.zeros_like(h_carry)

    z2b, logits = _router_logits(h1_prev[...], modp_ref[...], n2g_ref, rwt_ref, rb_ref)
    pos_rows = _route(logits, pos_ref, gate_ref, cnt_ref)

    mod = mod_ref[...]
    sh1, sc1, g1 = mod[0:1], mod[1:2], mod[2:3]

    x = x_ref[...]
    z = _rms(x, n1g_ref[...] * (1.0 + sc1)) + sh1
    proj = jnp.dot(z.astype(BF16), win_ref[...], preferred_element_type=F32)
    perm_blocks = _perm_blocks(pos_rows, TILE_ROWS, t)
    n_pb = len(perm_blocks)
    sorted_blocks = iter(range(n_pb))

    def sort_next():
        q = next(sorted_blocks)
        _sort_rows(perm_blocks, z2b, xs_ref, q, q + 1)

    sort_next()
    xa = proj[:, 0:D_LRU]
    ya = proj[:, D_LRU:2 * D_LRU]
    u = proj[:, 2 * D_LRU:2 * D_LRU + D_SGU]
    v = proj[:, 2 * D_LRU + D_SGU:]

    tail = xa_tail[...]
    row8 = lax.broadcasted_iota(I32, (8, 1), 0)
    xc = xa * convw_ref[CONV_WIDTH - 1:CONV_WIDTH, :] + convb_ref[...]
    for sft in range(1, CONV_WIDTH):
        rolled = pltpu.roll(xa, sft, 0)
        head = jnp.where(row8 < sft, pltpu.roll(tail, sft, 0), rolled[0:8])
        shifted = jnp.concatenate([head, rolled[8:]], axis=0)
        xc = xc + shifted * convw_ref[CONV_WIDTH - 1 - sft:CONV_WIDTH - sft, :]
    xa_tail[...] = xa[t - 8:t]
    sort_next()

    xcb = xc.astype(BF16)
    hw = D_LRU // 2
    g_lo = jnp.dot(xcb[:, 0:hw], wgate_ref[0], preferred_element_type=F32)
    g_hi = jnp.dot(xcb[:, hw:], wgate_ref[1], preferred_element_type=F32)
    sort_next()
    r_gate = _sigmoid(jnp.concatenate([g_lo[:, 0:hw], g_hi[:, 0:hw]], axis=1) + bgate_ref[:, 0:D_LRU])
    i_gate = _sigmoid(jnp.concatenate([g_lo[:, hw:], g_hi[:, hw:]], axis=1) + bgate_ref[:, D_LRU:])
    sort_next()
    nlam = -lam_ref[...]
    softplus = jnp.maximum(nlam, 0.0) + jnp.log1p(jnp.exp(-jnp.abs(nlam)))
    log_a = (-LRU_C) * r_gate * softplus
    a = jnp.exp(log_a)
    sort_next()
    om = -jnp.tanh(log_a) * (a * a + 1.0)
    mult = jnp.where(om > 0.0, om * lax.rsqrt(om), 0.0)
    bterm = mult * i_gate * xc
    sort_next()
    h, h_last = _linear_scan(a, bterm, h_carry[...], scan_a, scan_b)
    h_carry[...] = h_last
    sort_next()
    o_lru = _rms(h * _gelu_tanh(ya), gl_ref[...])
    sort_next()

    ug = _gelu_tanh(u)
    vg = _gelu_tanh(v)
    sort_next()
    assert next(sorted_blocks, None) is None
    mu = jnp.mean(vg, axis=-1, keepdims=True)
    vcen = vg - mu
    var = jnp.mean(vcen * vcen, axis=-1, keepdims=True)
    vn = (vcen * lax.rsqrt(var + EPS) * lng_ref[...] + lnb_ref[...]).astype(BF16)
    ri = lax.broadcasted_iota(I32, (CHUNK, CHUNK), 0)
    ci = lax.broadcasted_iota(I32, (CHUNK, CHUNK), 1)
    causal = ri >= ci
    lane = lax.broadcasted_iota(I32, (1, 2 * SGU_HEAD_DIM), 1)
    first_half = lane < SGU_HEAD_DIM
    pair_w = []
    for p in range(SGU_HEADS // 2):
        w0 = jnp.where(causal, sguw_ref[2 * p], 0.0).astype(BF16)
        w1 = jnp.where(causal, sguw_ref[2 * p + 1], 0.0).astype(BF16)
        pair_w.append(jnp.concatenate([w0, w1], axis=1))
    chunks = []
    zero = jnp.zeros((), BF16)
    for n in range(t // CHUNK):
        cols = []
        for p in range(SGU_HEADS // 2):
            blk = vn[n * CHUNK:(n + 1) * CHUNK, p * LANES:(p + 1) * LANES]
            rhs = jnp.concatenate([jnp.where(first_half, blk, zero), jnp.where(first_half, zero, blk)], axis=0)
            cols.append(jnp.dot(pair_w[p], rhs, preferred_element_type=F32))
        chunks.append(jnp.concatenate(cols, axis=1) + sgub_ref[...])
    mixed = jnp.concatenate(chunks, axis=0)
    o_sgu = _rms(ug * mixed, gs_ref[...])

    heads = jnp.concatenate([o_lru, o_sgu], axis=1).astype(BF16)
    h1 = x + g1 * jnp.dot(heads, wout_ref[...], preferred_element_type=F32)
    h1_ref[...] = h1
    h1_prev[...] = h1


def _router_logits(h1, mod, n2g_ref, rwt_ref, rb_ref):
    sh2, sc2 = mod[3:4], mod[4:5]
    z2b = (_rms(h1, n2g_ref[...] * (1.0 + sc2)) + sh2).astype(BF16)
    logits = lax.dot_general(rwt_ref[...].astype(BF16), z2b, (((1,), (1,)), ((), ())),
                             preferred_element_type=F32) + rb_ref[...]
    return z2b, logits


def _route(logits, pos_ref, gate_ref, cnt_ref):
    t = logits.shape[1]
    eidx = lax.broadcasted_iota(I32, (N_EXPERTS, t), 0)
    work = logits
    sel = []
    tops = []
    for k in range(TOP_K):
        m = jnp.max(work, axis=0, keepdims=True)
        idx = jnp.min(jnp.where(work == m, eidx, N_EXPERTS), axis=0, keepdims=True)
        onehot = eidx == idx
        sel.append(onehot)
        tops.append(m)
        work = jnp.where(onehot, -jnp.inf, work)
    exps = [jnp.exp(tk - tops[0]) for tk in tops]
    denom = exps[0] + exps[1] + exps[2] + exps[3]
    chosen = jnp.zeros((N_EXPERTS, t), F32)
    for k in range(TOP_K):
        chosen = jnp.where(sel[k], 1.0, chosen)
        gate_ref[k:k + 1, :] = exps[k] / denom

    chosen_b = chosen.astype(BF16)
    si = lax.broadcasted_iota(I32, (t, t), 0)
    ti = lax.broadcasted_iota(I32, (t, t), 1)
    before = jnp.where(si < ti, 1.0, 0.0).astype(BF16)
    excl = jnp.dot(chosen_b, before, preferred_element_type=F32)
    cnt_col = jnp.sum(chosen, axis=1, keepdims=True).astype(I32)
    cnt_ref[...] = jnp.broadcast_to(cnt_col, cnt_ref.shape)
    run_len = ((cnt_col + (ROW_ALIGN - 1)) // ROW_ALIGN * ROW_ALIGN).astype(F32)
    er = lax.broadcasted_iota(I32, (N_EXPERTS, N_EXPERTS), 0)
    ec = lax.broadcasted_iota(I32, (N_EXPERTS, N_EXPERTS), 1)
    lower = jnp.where(ec < er, 1.0, 0.0).astype(BF16)
    run_start = jnp.dot(lower, jnp.broadcast_to(run_len, (N_EXPERTS, LANES)).astype(BF16),
                        preferred_element_type=F32)[:, 0:1]
    posmat = excl + run_start
    pos_rows = []
    for k in range(TOP_K):
        pk = jnp.sum(jnp.where(sel[k], posmat, 0.0), axis=0, keepdims=True).astype(I32)
        pos_ref[k:k + 1, :] = pk
        pos_rows.append(pk)
    return pos_rows


def _sort_rows(perm_blocks, z2b, xs_ref, q0, q1):
    for q in range(q0, q1):
        rows = slice(q * PERM_BLOCK, (q + 1) * PERM_BLOCK)
        xs_ref[rows, :] = jnp.dot(perm_blocks[q], z2b, preferred_element_type=F32)


def _mixer_call(x, mod3, n1g, win, convw, convb, wgate, bgate, lam, lng, lnb, sguw, sgub_full, gl, gs, wout,
                n2g, rwt, rb):
    bsz, seq, d = x.shape
    t = SEQ_TILE
    tiles = seq // t
    n_tiles = bsz * tiles
    x2 = x.reshape(bsz * seq, d)

    def const(shape):
        return pl.BlockSpec(shape, lambda i: (0,) * len(shape))

    mixed = lambda i: jnp.minimum(i, n_tiles - 1)
    routed = lambda i: jnp.maximum(i - 1, 0)
    return pl.pallas_call(
        functools.partial(_mixer_kernel, tiles_per_seq=tiles),
        grid=(n_tiles + 1,),
        in_specs=[
            pl.BlockSpec((t, d), lambda i: (mixed(i), 0)),
            pl.BlockSpec((None, 6, d), lambda i: (mixed(i) // tiles, 0, 0)),
            pl.BlockSpec((None, 6, d), lambda i: (routed(i) // tiles, 0, 0)),
            const((1, d)),
            const((d, 2 * d)),
            const((CONV_WIDTH, D_LRU)),
            const((1, D_LRU)),
            const((2, D_LRU // 2, D_LRU)),
            const((1, 2 * D_LRU)),
            const((1, D_LRU)),
            const((1, D_SGU)),
            const((1, D_SGU)),
            const((SGU_HEADS, CHUNK, CHUNK)),
            const((CHUNK, D_SGU)),
            const((1, D_LRU)),
            const((1, D_SGU)),
            const((d, d)),
            const((1, d)),
            const((N_EXPERTS, d)),
            const((N_EXPERTS, 1)),
        ],
        out_specs=[
            pl.BlockSpec((t, d), lambda i: (i, 0)),
            pl.BlockSpec((TILE_ROWS, ROW_W), lambda i: (routed(i), 0)),
            pl.BlockSpec((TOP_K, t), lambda i: (0, routed(i))),
            pl.BlockSpec((TOP_K, t), lambda i: (0, routed(i))),
            pl.BlockSpec((None, N_EXPERTS, LANES), lambda i: (routed(i), 0, 0)),
        ],
        out_shape=[
            jax.ShapeDtypeStruct(((n_tiles + 1) * t, d), F32),
            jax.ShapeDtypeStruct((n_tiles * TILE_ROWS, ROW_W), F32),
            jax.ShapeDtypeStruct((TOP_K, n_tiles * t), I32),
            jax.ShapeDtypeStruct((TOP_K, n_tiles * t), F32),
            jax.ShapeDtypeStruct((n_tiles, N_EXPERTS, LANES), I32),
        ],
        scratch_shapes=[pltpu.VMEM((8, D_LRU), F32), pltpu.VMEM((1, D_LRU), F32),
                        pltpu.VMEM((D_LRU // LANES, SCAN_SEGMENTS * _scan_pitch(t), LANES), F32),
                        pltpu.VMEM((D_LRU // LANES, SCAN_SEGMENTS * _scan_pitch(t), LANES), F32),
                        pltpu.VMEM((t, d), F32)],
        compiler_params=pltpu.CompilerParams(
            dimension_semantics=("arbitrary",), vmem_limit_bytes=VMEM_LIMIT_BYTES),
        name="mixer_router",
    )(x2, mod3, mod3, n1g, win, convw, convb, wgate, bgate, lam, lng, lnb, sguw, sgub_full, gl, gs, wout, n2g, rwt,
      rb)


def _expert_kernel(be_ref, first_ref, nval_ref, jrow_ref, ilo_ref, ihi_ref, nexte_ref, nb_ref,
                   runsrc_ref, rundst_ref, runn_ref, tailrow_ref, taillen_ref,
                   xs_hbm, wgu_hbm, bgu_ref, wd_hbm, bd_ref,
                   ys_hbm,
                   xbuf, ybuf, zbuf, wgu_st, wd_st, wgu_bf, wd_bf, gsem, ssem, zsem, wsem):
    r = ROW_BLOCK
    n_tiles = tailrow_ref.shape[0]
    nb = nb_ref[0]

    rows = lambda v: pl.multiple_of(v, ROW_ALIGN)

    def weight_copies(e):
        return (pltpu.make_async_copy(wgu_hbm.at[e], wgu_st, wsem.at[0]),
                pltpu.make_async_copy(wd_hbm.at[e], wd_st, wsem.at[1]))

    def tail_copy(i):
        n = rows(taillen_ref[i])
        return pltpu.make_async_copy(zbuf.at[pl.ds(0, n), :], ys_hbm.at[pl.ds(rows(tailrow_ref[i]), n), :],
                                     zsem.at[0])

    def for_each_run(blk, fn):
        first = ilo_ref[blk]
        count = ihi_ref[blk] - first + 1
        last_entry = runn_ref.shape[0] - 1

        def pair_body(p, c):
            k0 = blk * n_tiles + first + 2 * p
            k1 = jnp.minimum(k0 + 1, last_entry)
            n0 = runn_ref[k0]
            n1 = jnp.where(2 * p + 1 < count, runn_ref[k1], 0)
            s0, d0, s1, d1 = runsrc_ref[k0], rundst_ref[k0], runsrc_ref[k1], rundst_ref[k1]

            @pl.when(n0 > 0)
            def _():
                fn(rows(s0), rows(d0), rows(n0))

            @pl.when(n1 > 0)
            def _():
                fn(rows(s1), rows(d1), rows(n1))
            return c
        lax.fori_loop(0, (count + 1) // 2, pair_body, 0)

    def start_gather(dst_slot):
        def fn(hbm_row, buf_row, n):
            pltpu.make_async_copy(xs_hbm.at[pl.ds(hbm_row, n), :],
                                  xbuf.at[dst_slot, pl.ds(buf_row, n), :], gsem.at[dst_slot]).start()
        return fn

    def start_scatter(src_slot):
        def fn(hbm_row, buf_row, n):
            pltpu.make_async_copy(ybuf.at[src_slot, pl.ds(buf_row, n), :],
                                  ys_hbm.at[pl.ds(hbm_row, n), :], ssem.at[src_slot]).start()
        return fn

    def wait_gather(s, n):
        pltpu.make_async_copy(xs_hbm.at[pl.ds(0, rows(n)), :], xbuf.at[s, pl.ds(0, rows(n)), :], gsem.at[s]).wait()

    def wait_scatter(s, n):
        pltpu.make_async_copy(ybuf.at[s, pl.ds(0, rows(n)), :], ys_hbm.at[pl.ds(0, rows(n)), :], ssem.at[s]).wait()

    for c in weight_copies(be_ref[0]):
        c.start()
    xbuf[...] = jnp.zeros_like(xbuf)
    for_each_run(0, start_gather(0))
    zbuf[...] = jnp.zeros_like(zbuf)

    def start_tail(i, c):
        tail_copy(i).start()
        return c
    lax.fori_loop(0, n_tiles, start_tail, 0)

    def block(b, carry):
        slot = b & 1
        wait_gather(slot, nval_ref[b])

        @pl.when(b + 1 < nb)
        def _():
            for_each_run(b + 1, start_gather(1 - slot))

        @pl.when(first_ref[b] == 1)
        def _():
            for c in weight_copies(be_ref[b]):
                c.wait()
            wgu_bf[...] = wgu_st[...].astype(BF16)
            wd_bf[...] = wd_st[...].astype(BF16)

            @pl.when(nexte_ref[b] >= 0)
            def _():
                for c in weight_copies(nexte_ref[b]):
                    c.start()

        @pl.when(b >= 2)
        def _():
            wait_scatter(slot, nval_ref[jnp.maximum(b - 2, 0)])

        def expert_rows(m):
            xb = xbuf[slot, 0:m].astype(BF16)
            gu = jnp.dot(xb, wgu_bf[...], preferred_element_type=F32) + bgu_ref[be_ref[b]]
            g = jnp.minimum(gu[:, 0:D_MODEL], SWIGLU_LIMIT)
            lin = jnp.clip(gu[:, D_MODEL:], -SWIGLU_LIMIT, SWIGLU_LIMIT)
            act = g * _sigmoid(SWIGLU_ALPHA * g) * (lin + 1.0)
            y = jnp.dot(act.astype(BF16), wd_bf[...], preferred_element_type=F32) + bd_ref[be_ref[b]]
            ybuf[slot, 0:m] = _pack_bf16_pairs(y)

        step = r // ROW_PATHS
        for k in range(1, ROW_PATHS + 1):
            @pl.when((nval_ref[b] > (k - 1) * step) & (nval_ref[b] <= k * step))
            def _():
                expert_rows(k * step)

        for_each_run(b, start_scatter(slot))
        return carry

    lax.fori_loop(0, nb, block, 0)

    last = nb - 1
    wait_scatter(last & 1, nval_ref[last])

    @pl.when(nb >= 2)
    def _():
        wait_scatter(1 - (last & 1), nval_ref[jnp.maximum(last - 1, 0)])

    def wait_tail(i, c):
        tail_copy(i).wait()
        return c
    lax.fori_loop(0, n_tiles, wait_tail, 0)


def _expert_call(tables, xs, w_gu, b_gu, w_down, b_down):
    nblk = tables[0].shape[0]
    r = ROW_BLOCK
    d = D_MODEL
    whole = lambda i, *_: (0, 0, 0)
    grid_spec = pltpu.PrefetchScalarGridSpec(
        num_scalar_prefetch=len(tables),
        grid=(1,),
        in_specs=[
            pl.BlockSpec(memory_space=pl.ANY),
            pl.BlockSpec(memory_space=pl.ANY),
            pl.BlockSpec((N_EXPERTS, 1, 2 * d), whole),
            pl.BlockSpec(memory_space=pl.ANY),
            pl.BlockSpec((N_EXPERTS, 1, d), whole),
        ],
        out_specs=pl.BlockSpec(memory_space=pl.ANY),
        scratch_shapes=[
            pltpu.VMEM((2, r, ROW_W), F32),
            pltpu.VMEM((2, r, d // 2), U32),
            pltpu.VMEM((ROW_ALIGN * N_EXPERTS, d // 2), U32),
            pltpu.VMEM((d, 2 * d), F32),
            pltpu.VMEM((d, d), F32),
            pltpu.VMEM((d, 2 * d), BF16),
            pltpu.VMEM((d, d), BF16),
            pltpu.SemaphoreType.DMA((2,)),
            pltpu.SemaphoreType.DMA((2,)),
            pltpu.SemaphoreType.DMA((1,)),
            pltpu.SemaphoreType.DMA((2,)),
        ],
    )
    return pl.pallas_call(
        _expert_kernel,
        grid_spec=grid_spec,
        out_shape=jax.ShapeDtypeStruct((xs.shape[0], d // 2), U32),
        compiler_params=pltpu.CompilerParams(
            dimension_semantics=("arbitrary",), vmem_limit_bytes=VMEM_LIMIT_BYTES),
        name="experts",
    )(*tables, xs, w_gu, b_gu, w_down, b_down)


def _final_kernel(h1_ref, mod_ref, ys_ref, pos_ref, gate_ref, fg_ref, o_ref):
    t = SEQ_TILE
    g2 = mod_ref[5:6, :]
    tn = (((0,), (0,)), ((), ()))
    for s in range(h1_ref.shape[0] // t):
        tok = slice(s * t, (s + 1) * t)
        pos_rows = [pos_ref[k:k + 1, tok] for k in range(TOP_K)]
        gate_rows = [gate_ref[k:k + 1, tok].astype(BF16) for k in range(TOP_K)]
        perm = jnp.concatenate(_perm_blocks(pos_rows, TILE_ROWS, t, gate_rows), axis=0)
        y_lo, y_hi = _unpack_bf16_pairs(ys_ref[s * TILE_ROWS:(s + 1) * TILE_ROWS, :])
        moe = jnp.concatenate([lax.dot_general(perm, y_lo, tn, preferred_element_type=F32),
                               lax.dot_general(perm, y_hi, tn, preferred_element_type=F32)], axis=1)
        o_ref[tok, :] = _rms(h1_ref[tok, :] + g2 * moe, fg_ref[...])


def _final_call(h1, mod3, ys, pos, gates, final_g, bsz, seq):
    d = h1.shape[1]
    n_tok = bsz * seq
    per_step = FINAL_TILES_PER_STEP
    t = per_step * SEQ_TILE
    per_batch = seq // t
    out = pl.pallas_call(
        _final_kernel,
        grid=(n_tok // t,),
        in_specs=[
            pl.BlockSpec((t, d), lambda i: (i, 0)),
            pl.BlockSpec((None, 6, d), lambda i: (i // per_batch, 0, 0)),
            pl.BlockSpec((per_step * TILE_ROWS, d // 2), lambda i: (i, 0)),
            pl.BlockSpec((TOP_K, t), lambda i: (0, i)),
            pl.BlockSpec((TOP_K, t), lambda i: (0, i)),
            pl.BlockSpec((1, d), lambda i: (0, 0)),
        ],
        out_specs=pl.BlockSpec((t, d), lambda i: (i, 0)),
        out_shape=jax.ShapeDtypeStruct((n_tok, d), F32),
        compiler_params=pltpu.CompilerParams(
            dimension_semantics=("arbitrary",), vmem_limit_bytes=VMEM_LIMIT_BYTES),
        name="combine_final",
    )(h1, mod3, ys, pos, gates, final_g.reshape(1, d))
    return out.reshape(bsz, seq, d)


def _block_diag(w):
    h, i, o = w.shape
    eye = jnp.eye(h, dtype=w.dtype)
    return (w[:, :, None, :] * eye[:, None, :, None]).reshape(h * i, h * o)


def _route_tables(cnt):
    r = ROW_BLOCK
    n_tiles = cnt.shape[0]
    nblk = n_tiles * TILE_ROWS // r + N_EXPERTS
    cnt = (cnt + (ROW_ALIGN - 1)) // ROW_ALIGN * ROW_ALIGN
    ie = jnp.arange(N_EXPERTS, dtype=I32)
    it = jnp.arange(n_tiles, dtype=I32)
    e_before = (ie[:, None] < ie[None, :]).astype(I32)
    t_before = (it[:, None] < it[None, :]).astype(I32)
    total = jnp.sum(cnt, axis=0)
    cum = jnp.sum(t_before[:, :, None] * cnt[:, None, :], axis=0)
    seg_off = jnp.sum(cnt[:, :, None] * e_before[None, :, :], axis=1)
    base = seg_off + it[:, None] * TILE_ROWS
    nblk_e = (total + r - 1) // r
    blk_start = jnp.sum(nblk_e[:, None] * e_before, axis=0)
    blk_end = blk_start + nblk_e
    nb_used = jnp.sum(nblk_e)
    blk = jnp.arange(nblk, dtype=I32)
    used = blk < nb_used
    last_e = jnp.max(jnp.where(nblk_e > 0, ie, 0))
    be = jnp.minimum(jnp.sum((blk[:, None] >= blk_end[None, :]).astype(I32), axis=1), N_EXPERTS - 1)
    be = jnp.where(used, be, last_e)
    be_onehot = (be[:, None] == ie[None, :]).astype(I32)
    pick = lambda v: jnp.sum(be_onehot * v[None, :], axis=1)
    jrow = jnp.where(used, (blk - pick(blk_start)) * r, 0)
    nval = jnp.where(used, jnp.clip(pick(total) - jrow, 0, r), 0)
    first = jnp.concatenate([jnp.ones((1,), I32), (be[1:] != be[:-1]).astype(I32)])
    nxt = pick(blk_end)
    next_e = jnp.where(nxt < nb_used, jnp.sum((nxt[:, None] >= blk_end[None, :]).astype(I32), axis=1), -1)
    run_start = jnp.sum(be_onehot[:, None, :] * cum[None, :, :], axis=-1)
    run_end = run_start + jnp.sum(be_onehot[:, None, :] * cnt[None, :, :], axis=-1)
    ilo = jnp.sum((run_end <= jrow[:, None]).astype(I32), axis=1)
    ihi = n_tiles - 1 - jnp.sum((run_start >= (jrow + r)[:, None]).astype(I32), axis=1)
    lo = jnp.maximum(run_start, jrow[:, None])
    hi = jnp.minimum(run_end, (jrow + r)[:, None])
    run_n = jnp.where(used[:, None], jnp.clip(hi - lo, 0, r), 0)
    run_src = jnp.sum(be_onehot[:, None, :] * base[None, :, :], axis=-1) + (lo - run_start)
    run_dst = lo - jrow[:, None]
    used_rows = jnp.sum(cnt, axis=1)
    tail_row = jnp.arange(n_tiles, dtype=I32) * TILE_ROWS + used_rows
    tail_len = TILE_ROWS - used_rows
    i32 = lambda v: v.astype(I32)
    return (i32(be), i32(first), i32(nval), i32(jrow), i32(jnp.minimum(ilo, n_tiles - 1)), i32(ihi), i32(next_e),
            i32(nb_used).reshape(1), i32(run_src).reshape(-1), i32(run_dst).reshape(-1), i32(run_n).reshape(-1),
            i32(tail_row), i32(tail_len))


def kernel(x, c, ada_w, ada_b, norm1_g, w_in, conv_w, conv_b, lru_wr, lru_br, lru_wi, lru_bi, lru_lambda, sgu_ln_g, sgu_ln_b, sgu_w, sgu_b, gnorm_lru_g, gnorm_sgu_g, w_out, norm2_g, router_w, router_b, exp_w_gu, exp_b_gu, exp_w_down, exp_b_down, final_g):
    bsz, seq, d = x.shape
    depth = ada_w.shape[0]
    assert depth == 1 and d == D_MODEL and seq % SEQ_TILE == 0 and ROW_BLOCK >= ROW_ALIGN * N_EXPERTS
    l = 0
    mod = _adaln_call(c, ada_w[l], ada_b[l])
    mod3 = mod.reshape(bsz, 6, d)

    row = lambda v: v.reshape(1, -1)
    hw = D_LRU // 2
    wr_bd, wi_bd = _block_diag(lru_wr[l]), _block_diag(lru_wi[l])
    wgate = jnp.stack([jnp.concatenate([wr_bd[s:s + hw, s:s + hw], wi_bd[s:s + hw, s:s + hw]], axis=1)
                       for s in (0, hw)]).astype(BF16)
    bgate = jnp.concatenate([lru_br[l], lru_bi[l]]).reshape(1, -1)
    sgub_full = jnp.repeat(sgu_b[l].T, SGU_HEAD_DIM, axis=1)
    h1, xs, pos, gates, cnt = _mixer_call(
        x, mod3, row(norm1_g[l]), w_in[l].astype(BF16), conv_w[l], row(conv_b[l]), wgate, bgate,
        row(lru_lambda[l]), row(sgu_ln_g[l]), row(sgu_ln_b[l]), sgu_w[l], sgub_full,
        row(gnorm_lru_g[l]), row(gnorm_sgu_g[l]), w_out[l].astype(BF16), row(norm2_g[l]),
        router_w[l].T, router_b[l].reshape(-1, 1))

    tables = _route_tables(cnt[:, :, 0])
    ys = _expert_call(tables, xs, exp_w_gu[l], exp_b_gu[l].reshape(N_EXPERTS, 1, -1),
                      exp_w_down[l], exp_b_down[l].reshape(N_EXPERTS, 1, -1))
    return _final_call(h1, mod3, ys, pos, gates, final_g, bsz, seq)
```

```python
import functools

import jax
import jax.numpy as jnp
from jax import lax
from jax.experimental import pallas as pl
from jax.experimental.pallas import tpu as pltpu

F32 = jnp.float32
BF16 = jnp.bfloat16
I32 = jnp.int32
U32 = jnp.uint32

D_MODEL = 1024
D_LRU = 512
D_SGU = 512
LRU_HEADS = 8
CONV_WIDTH = 4
LRU_C = 8.0
SGU_HEADS = 8
SGU_HEAD_DIM = D_SGU // SGU_HEADS
CHUNK = 128
N_EXPERTS = 32
TOP_K = 4
SWIGLU_ALPHA = 1.702
SWIGLU_LIMIT = 7.0
EPS = 1e-6

LANES = 128
SEQ_TILE = 512
FINAL_TILES_PER_STEP = 2
ROW_ALIGN = 8
TILE_ROWS = TOP_K * SEQ_TILE + ROW_ALIGN * N_EXPERTS
ROW_W = D_MODEL
ROW_BLOCK = 768
ROW_PATHS = 3
V7X_VMEM_BYTES = 64 * 1024 * 1024
VMEM_LIMIT_BYTES = V7X_VMEM_BYTES * 7 // 8


def _sigmoid(x):
    return 1.0 / (1.0 + jnp.exp(-x))


def _gelu_tanh(x):
    return 0.5 * x * (1.0 + jnp.tanh(0.7978845608028654 * (x + 0.044715 * (x * x * x))))


def _rms(x, g):
    ms = jnp.mean(x * x, axis=-1, keepdims=True)
    return x * lax.rsqrt(ms + EPS) * g


def _pack_bf16_pairs(v):
    n = v.shape[1] // 2
    r = v.astype(BF16).astype(F32)
    bits = lax.bitcast_convert_type(r, U32)
    return (bits[:, 0:n] >> 16) | (bits[:, n:] & jnp.uint32(0xFFFF0000))


def _unpack_bf16_pairs(w):
    lo = lax.bitcast_convert_type(w << 16, F32).astype(BF16)
    hi = lax.bitcast_convert_type(w & jnp.uint32(0xFFFF0000), F32).astype(BF16)
    return lo, hi


PERM_BLOCK = 256


def _perm_blocks(pos_rows, n_rows, n_cols, values=None):
    blk = PERM_BLOCK
    rid = lax.broadcasted_iota(I32, (blk, n_cols), 0).astype(F32).astype(BF16)
    one = jnp.ones((), BF16)
    blocks = []
    for q in range(n_rows // blk):
        acc = jnp.zeros((blk, n_cols), BF16)
        for k, pk in enumerate(pos_rows):
            in_blk = jnp.where(pk // blk == q, pk % blk, -1).astype(F32).astype(BF16)
            acc = jnp.where(rid == in_blk, one if values is None else values[k], acc)
        blocks.append(acc)
    return blocks


def _adaln_kernel(c_ref, w_ref, b_ref, o_ref):
    c = c_ref[...]
    ca = c * _sigmoid(c)
    o_ref[...] = jnp.dot(ca.astype(BF16), w_ref[...].astype(BF16), preferred_element_type=F32) + b_ref[...]


def _adaln_call(c, w, b):
    bsz, d = c.shape
    n_out = w.shape[1]
    return pl.pallas_call(
        _adaln_kernel,
        grid=(n_out // d,),
        in_specs=[
            pl.BlockSpec((bsz, d), lambda j: (0, 0)),
            pl.BlockSpec((d, d), lambda j: (0, j)),
            pl.BlockSpec((1, d), lambda j: (0, j)),
        ],
        out_specs=pl.BlockSpec((bsz, d), lambda j: (0, j)),
        out_shape=jax.ShapeDtypeStruct((bsz, n_out), F32),
        name="adaln",
    )(c, w, b.reshape(1, n_out))


SCAN_SEGMENTS = 8
SCAN_PITCH_PAD = 8


def _scan_pitch(t):
    return t // SCAN_SEGMENTS + SCAN_PITCH_PAD


def _linear_scan(a, b, h0, a_buf, b_buf):
    t, c = a.shape
    nseg = SCAN_SEGMENTS
    seg = t // nseg
    pitch = _scan_pitch(t)
    nlb = c // LANES
    for j in range(nlb):
        for s in range(nseg):
            a_buf[j, s * pitch:s * pitch + seg, :] = a[s * seg:(s + 1) * seg, j * LANES:(j + 1) * LANES]
            b_buf[j, s * pitch:s * pitch + seg, :] = b[s * seg:(s + 1) * seg, j * LANES:(j + 1) * LANES]

    def step(g, carry):
        hs, ps = carry
        new_h, new_p = [], []
        for j in range(nlb):
            view = (j, pl.ds(g, nseg, stride=pitch), slice(None))
            ag = a_buf[view]
            hj = ag * hs[j] + b_buf[view]
            pj = ag * ps[j]
            b_buf[view] = hj
            a_buf[view] = pj
            new_h.append(hj)
            new_p.append(pj)
        return tuple(new_h), tuple(new_p)

    init = (tuple(jnp.zeros((nseg, LANES), F32) for _ in range(nlb)),
            tuple(jnp.ones((nseg, LANES), F32) for _ in range(nlb)))
    h_end, p_end = lax.fori_loop(0, seg, step, init, unroll=True)
    h_end = jnp.concatenate(h_end, axis=1)
    p_end = jnp.concatenate(p_end, axis=1)
    state = h0
    out = []
    for s in range(nseg):
        h_loc = jnp.concatenate([b_buf[j, s * pitch:s * pitch + seg, :] for j in range(nlb)], axis=1)
        p_loc = jnp.concatenate([a_buf[j, s * pitch:s * pitch + seg, :] for j in range(nlb)], axis=1)
        out.append(h_loc + p_loc * state)
        state = h_end[s:s + 1] + p_end[s:s + 1] * state
    return jnp.concatenate(out, axis=0), state


def _mixer_kernel(x_ref, mod_ref, modp_ref, n1g_ref, win_ref, convw_ref, convb_ref, wgate_ref, bgate_ref, lam_ref,
                  lng_ref, lnb_ref, sguw_ref, sgub_ref, gl_ref, gs_ref, wout_ref, n2g_ref, rwt_ref, rb_ref,
                  h1_ref, xs_ref, pos_ref, gate_ref, cnt_ref,
                  xa_tail, h_carry, scan_a, scan_b, h1_prev, *, tiles_per_seq):
    t = x_ref.shape[0]
    i = pl.program_id(0)

    @pl.when(i == 0)
    def _():
        h1_prev[...] = jnp.zeros_like(h1_prev)

    @pl.when(i % tiles_per_seq == 0)
    def _():
        xa_tail[...] = jnp.zeros_like(xa_tail)
        h_carry[...] = jnp.zeros_like(h_carry)

    z2b, logits = _router_logits(h1_prev[...], modp_ref[...], n2g_ref, rwt_ref, rb_ref)
    pos_rows = _route(logits, pos_ref, gate_ref, cnt_ref)

    mod = mod_ref[...]
    sh1, sc1, g1 = mod[0:1], mod[1:2], mod[2:3]

    x = x_ref[...]
    z = _rms(x, n1g_ref[...] * (1.0 + sc1)) + sh1
    proj = jnp.dot(z.astype(BF16), win_ref[...], preferred_element_type=F32)
    perm_blocks = _perm_blocks(pos_rows, TILE_ROWS, t)
    n_pb = len(perm_blocks)
    sorted_blocks = iter(range(n_pb))

    def sort_next():
        q = next(sorted_blocks)
        _sort_rows(perm_blocks, z2b, xs_ref, q, q + 1)

    sort_next()
    xa = proj[:, 0:D_LRU]
    ya = proj[:, D_LRU:2 * D_LRU]
    u = proj[:, 2 * D_LRU:2 * D_LRU + D_SGU]
    v = proj[:, 2 * D_LRU + D_SGU:]

    tail = xa_tail[...]
    row8 = lax.broadcasted_iota(I32, (8, 1), 0)
    xc = xa * convw_ref[CONV_WIDTH - 1:CONV_WIDTH, :] + convb_ref[...]
    for sft in range(1, CONV_WIDTH):
        rolled = pltpu.roll(xa, sft, 0)
        head = jnp.where(row8 < sft, pltpu.roll(tail, sft, 0), rolled[0:8])
        shifted = jnp.concatenate([head, rolled[8:]], axis=0)
        xc = xc + shifted * convw_ref[CONV_WIDTH - 1 - sft:CONV_WIDTH - sft, :]
    xa_tail[...] = xa[t - 8:t]
    sort_next()

    xcb = xc.astype(BF16)
    hw = D_LRU // 2
    g_lo = jnp.dot(xcb[:, 0:hw], wgate_ref[0], preferred_element_type=F32)
    g_hi = jnp.dot(xcb[:, hw:], wgate_ref[1], preferred_element_type=F32)
    sort_next()
    r_gate = _sigmoid(jnp.concatenate([g_lo[:, 0:hw], g_hi[:, 0:hw]], axis=1) + bgate_ref[:, 0:D_LRU])
    i_gate = _sigmoid(jnp.concatenate([g_lo[:, hw:], g_hi[:, hw:]], axis=1) + bgate_ref[:, D_LRU:])
    sort_next()
    nlam = -lam_ref[...]
    softplus = jnp.maximum(nlam, 0.0) + jnp.log1p(jnp.exp(-jnp.abs(nlam)))
    log_a = (-LRU_C) * r_gate * softplus
    a = jnp.exp(log_a)
    sort_next()
    om = -jnp.tanh(log_a) * (a * a + 1.0)
    mult = jnp.where(om > 0.0, om * lax.rsqrt(om), 0.0)
    bterm = mult * i_gate * xc
    sort_next()
    h, h_last = _linear_scan(a, bterm, h_carry[...], scan_a, scan_b)
    h_carry[...] = h_last
    sort_next()
    o_lru = _rms(h * _gelu_tanh(ya), gl_ref[...])
    sort_next()

    ug = _gelu_tanh(u)
    vg = _gelu_tanh(v)
    sort_next()
    assert next(sorted_blocks, None) is None
    mu = jnp.mean(vg, axis=-1, keepdims=True)
    vcen = vg - mu
    var = jnp.mean(vcen * vcen, axis=-1, keepdims=True)
    vn = (vcen * lax.rsqrt(var + EPS) * lng_ref[...] + lnb_ref[...]).astype(BF16)
    ri = lax.broadcasted_iota(I32, (CHUNK, CHUNK), 0)
    ci = lax.broadcasted_iota(I32, (CHUNK, CHUNK), 1)
    causal = ri >= ci
    lane = lax.broadcasted_iota(I32, (1, 2 * SGU_HEAD_DIM), 1)
    first_half = lane < SGU_HEAD_DIM
    pair_w = []
    for p in range(SGU_HEADS // 2):
        w0 = jnp.where(causal, sguw_ref[2 * p], 0.0).astype(BF16)
        w1 = jnp.where(causal, sguw_ref[2 * p + 1], 0.0).astype(BF16)
        pair_w.append(jnp.concatenate([w0, w1], axis=1))
    zero = jnp.zeros((), BF16)
    n_chunks = t // CHUNK
    per_pair = []
    for p in range(SGU_HEADS // 2):
        rhs = []
        for n in range(n_chunks):
            blk = vn[n * CHUNK:(n + 1) * CHUNK, p * LANES:(p + 1) * LANES]
            rhs.append(jnp.concatenate([jnp.where(first_half, blk, zero), jnp.where(first_half, zero, blk)], axis=0))
        per_pair.append(jnp.dot(pair_w[p], jnp.concatenate(rhs, axis=1), preferred_element_type=F32))
    chunks = [jnp.concatenate([pp[:, n * LANES:(n + 1) * LANES] for pp in per_pair], axis=1) + sgub_ref[...]
              for n in range(n_chunks)]
    mixed = jnp.concatenate(chunks, axis=0)
    o_sgu = _rms(ug * mixed, gs_ref[...])

    heads = jnp.concatenate([o_lru, o_sgu], axis=1).astype(BF16)
    h1 = x + g1 * jnp.dot(heads, wout_ref[...], preferred_element_type=F32)
    h1_ref[...] = h1
    h1_prev[...] = h1


def _router_logits(h1, mod, n2g_ref, rwt_ref, rb_ref):
    sh2, sc2 = mod[3:4], mod[4:5]
    z2b = (_rms(h1, n2g_ref[...] * (1.0 + sc2)) + sh2).astype(BF16)
    logits = lax.dot_general(rwt_ref[...].astype(BF16), z2b, (((1,), (1,)), ((), ())),
                             preferred_element_type=F32) + rb_ref[...]
    return z2b, logits


def _route(logits, pos_ref, gate_ref, cnt_ref):
    t = logits.shape[1]
    eidx = lax.broadcasted_iota(I32, (N_EXPERTS, t), 0)
    work = logits
    sel = []
    tops = []
    for k in range(TOP_K):
        m = jnp.max(work, axis=0, keepdims=True)
        idx = jnp.min(jnp.where(work == m, eidx, N_EXPERTS), axis=0, keepdims=True)
        onehot = eidx == idx
        sel.append(onehot)
        tops.append(m)
        work = jnp.where(onehot, -jnp.inf, work)
    exps = [jnp.exp(tk - tops[0]) for tk in tops]
    denom = exps[0] + exps[1] + exps[2] + exps[3]
    chosen = jnp.zeros((N_EXPERTS, t), F32)
    for k in range(TOP_K):
        chosen = jnp.where(sel[k], 1.0, chosen)
        gate_ref[k:k + 1, :] = exps[k] / denom

    chosen_b = chosen.astype(BF16)
    si = lax.broadcasted_iota(I32, (t, t), 0)
    ti = lax.broadcasted_iota(I32, (t, t), 1)
    before = jnp.where(si < ti, 1.0, 0.0).astype(BF16)
    excl = jnp.dot(chosen_b, before, preferred_element_type=F32)
    cnt_col = jnp.sum(chosen, axis=1, keepdims=True).astype(I32)
    cnt_ref[...] = jnp.broadcast_to(cnt_col, cnt_ref.shape)
    run_len = ((cnt_col + (ROW_ALIGN - 1)) // ROW_ALIGN * ROW_ALIGN).astype(F32)
    er = lax.broadcasted_iota(I32, (N_EXPERTS, N_EXPERTS), 0)
    ec = lax.broadcasted_iota(I32, (N_EXPERTS, N_EXPERTS), 1)
    lower = jnp.where(ec < er, 1.0, 0.0).astype(BF16)
    run_start = jnp.dot(lower, jnp.broadcast_to(run_len, (N_EXPERTS, LANES)).astype(BF16),
                        preferred_element_type=F32)[:, 0:1]
    posmat = excl + run_start
    pos_rows = []
    for k in range(TOP_K):
        pk = jnp.sum(jnp.where(sel[k], posmat, 0.0), axis=0, keepdims=True).astype(I32)
        pos_ref[k:k + 1, :] = pk
        pos_rows.append(pk)
    return pos_rows


def _sort_rows(perm_blocks, z2b, xs_ref, q0, q1):
    for q in range(q0, q1):
        rows = slice(q * PERM_BLOCK, (q + 1) * PERM_BLOCK)
        xs_ref[rows, :] = jnp.dot(perm_blocks[q], z2b, preferred_element_type=F32)


def _mixer_call(x, mod3, n1g, win, convw, convb, wgate, bgate, lam, lng, lnb, sguw, sgub_full, gl, gs, wout,
                n2g, rwt, rb):
    bsz, seq, d = x.shape
    t = SEQ_TILE
    tiles = seq // t
    n_tiles = bsz * tiles
    x2 = x.reshape(bsz * seq, d)

    def const(shape):
        return pl.BlockSpec(shape, lambda i: (0,) * len(shape))

    mixed = lambda i: jnp.minimum(i, n_tiles - 1)
    routed = lambda i: jnp.maximum(i - 1, 0)
    return pl.pallas_call(
        functools.partial(_mixer_kernel, tiles_per_seq=tiles),
        grid=(n_tiles + 1,),
        in_specs=[
            pl.BlockSpec((t, d), lambda i: (mixed(i), 0)),
            pl.BlockSpec((None, 6, d), lambda i: (mixed(i) // tiles, 0, 0)),
            pl.BlockSpec((None, 6, d), lambda i: (routed(i) // tiles, 0, 0)),
            const((1, d)),
            const((d, 2 * d)),
            const((CONV_WIDTH, D_LRU)),
            const((1, D_LRU)),
            const((2, D_LRU // 2, D_LRU)),
            const((1, 2 * D_LRU)),
            const((1, D_LRU)),
            const((1, D_SGU)),
            const((1, D_SGU)),
            const((SGU_HEADS, CHUNK, CHUNK)),
            const((CHUNK, D_SGU)),
            const((1, D_LRU)),
            const((1, D_SGU)),
            const((d, d)),
            const((1, d)),
            const((N_EXPERTS, d)),
            const((N_EXPERTS, 1)),
        ],
        out_specs=[
            pl.BlockSpec((t, d), lambda i: (i, 0)),
            pl.BlockSpec((TILE_ROWS, ROW_W), lambda i: (routed(i), 0)),
            pl.BlockSpec((TOP_K, t), lambda i: (0, routed(i))),
            pl.BlockSpec((TOP_K, t), lambda i: (0, routed(i))),
            pl.BlockSpec((None, N_EXPERTS, LANES), lambda i: (routed(i), 0, 0)),
        ],
        out_shape=[
            jax.ShapeDtypeStruct(((n_tiles + 1) * t, d), F32),
            jax.ShapeDtypeStruct((n_tiles * TILE_ROWS, ROW_W), F32),
            jax.ShapeDtypeStruct((TOP_K, n_tiles * t), I32),
            jax.ShapeDtypeStruct((TOP_K, n_tiles * t), F32),
            jax.ShapeDtypeStruct((n_tiles, N_EXPERTS, LANES), I32),
        ],
        scratch_shapes=[pltpu.VMEM((8, D_LRU), F32), pltpu.VMEM((1, D_LRU), F32),
                        pltpu.VMEM((D_LRU // LANES, SCAN_SEGMENTS * _scan_pitch(t), LANES), F32),
                        pltpu.VMEM((D_LRU // LANES, SCAN_SEGMENTS * _scan_pitch(t), LANES), F32),
                        pltpu.VMEM((t, d), F32)],
        compiler_params=pltpu.CompilerParams(
            dimension_semantics=("arbitrary",), vmem_limit_bytes=VMEM_LIMIT_BYTES),
        name="mixer_router",
    )(x2, mod3, mod3, n1g, win, convw, convb, wgate, bgate, lam, lng, lnb, sguw, sgub_full, gl, gs, wout, n2g, rwt,
      rb)


def _expert_kernel(be_ref, first_ref, nval_ref, jrow_ref, ilo_ref, ihi_ref, nexte_ref, nb_ref,
                   runsrc_ref, rundst_ref, runn_ref, tailrow_ref, taillen_ref,
                   xs_hbm, wgu_hbm, bgu_ref, wd_hbm, bd_ref,
                   ys_hbm,
                   xbuf, ybuf, zbuf, wgu_st, wd_st, wgu_bf, wd_bf, gsem, ssem, zsem, wsem):
    r = ROW_BLOCK
    n_tiles = tailrow_ref.shape[0]
    nb = nb_ref[0]

    rows = lambda v: pl.multiple_of(v, ROW_ALIGN)

    def weight_copies(e):
        return (pltpu.make_async_copy(wgu_hbm.at[e], wgu_st, wsem.at[0]),
                pltpu.make_async_copy(wd_hbm.at[e], wd_st, wsem.at[1]))

    def tail_copy(i):
        n = rows(taillen_ref[i])
        return pltpu.make_async_copy(zbuf.at[pl.ds(0, n), :], ys_hbm.at[pl.ds(rows(tailrow_ref[i]), n), :],
                                     zsem.at[0])

    def for_each_run(blk, fn):
        first = ilo_ref[blk]
        count = ihi_ref[blk] - first + 1
        last_entry = runn_ref.shape[0] - 1

        def pair_body(p, c):
            k0 = blk * n_tiles + first + 2 * p
            k1 = jnp.minimum(k0 + 1, last_entry)
            n0 = runn_ref[k0]
            n1 = jnp.where(2 * p + 1 < count, runn_ref[k1], 0)
            s0, d0, s1, d1 = runsrc_ref[k0], rundst_ref[k0], runsrc_ref[k1], rundst_ref[k1]

            @pl.when(n0 > 0)
            def _():
                fn(rows(s0), rows(d0), rows(n0))

            @pl.when(n1 > 0)
            def _():
                fn(rows(s1), rows(d1), rows(n1))
            return c
        lax.fori_loop(0, (count + 1) // 2, pair_body, 0)

    def start_gather(dst_slot):
        def fn(hbm_row, buf_row, n):
            pltpu.make_async_copy(xs_hbm.at[pl.ds(hbm_row, n), :],
                                  xbuf.at[dst_slot, pl.ds(buf_row, n), :], gsem.at[dst_slot]).start()
        return fn

    def start_scatter(src_slot):
        def fn(hbm_row, buf_row, n):
            pltpu.make_async_copy(ybuf.at[src_slot, pl.ds(buf_row, n), :],
                                  ys_hbm.at[pl.ds(hbm_row, n), :], ssem.at[src_slot]).start()
        return fn

    def wait_gather(s, n):
        pltpu.make_async_copy(xs_hbm.at[pl.ds(0, rows(n)), :], xbuf.at[s, pl.ds(0, rows(n)), :], gsem.at[s]).wait()

    def wait_scatter(s, n):
        pltpu.make_async_copy(ybuf.at[s, pl.ds(0, rows(n)), :], ys_hbm.at[pl.ds(0, rows(n)), :], ssem.at[s]).wait()

    for c in weight_copies(be_ref[0]):
        c.start()
    xbuf[...] = jnp.zeros_like(xbuf)
    for_each_run(0, start_gather(0))
    zbuf[...] = jnp.zeros_like(zbuf)

    def start_tail(i, c):
        tail_copy(i).start()
        return c
    lax.fori_loop(0, n_tiles, start_tail, 0)

    def block(b, carry):
        slot = b & 1
        wait_gather(slot, nval_ref[b])

        @pl.when(b + 1 < nb)
        def _():
            for_each_run(b + 1, start_gather(1 - slot))

        @pl.when(first_ref[b] == 1)
        def _():
            for c in weight_copies(be_ref[b]):
                c.wait()
            wgu_bf[...] = wgu_st[...].astype(BF16)
            wd_bf[...] = wd_st[...].astype(BF16)

            @pl.when(nexte_ref[b] >= 0)
            def _():
                for c in weight_copies(nexte_ref[b]):
                    c.start()

        @pl.when(b >= 2)
        def _():
            wait_scatter(slot, nval_ref[jnp.maximum(b - 2, 0)])

        def expert_rows(m):
            xb = xbuf[slot, 0:m].astype(BF16)
            gu = jnp.dot(xb, wgu_bf[...], preferred_element_type=F32) + bgu_ref[be_ref[b]]
            g = jnp.minimum(gu[:, 0:D_MODEL], SWIGLU_LIMIT)
            lin = jnp.clip(gu[:, D_MODEL:], -SWIGLU_LIMIT, SWIGLU_LIMIT)
            act = g * _sigmoid(SWIGLU_ALPHA * g) * (lin + 1.0)
            y = jnp.dot(act.astype(BF16), wd_bf[...], preferred_element_type=F32) + bd_ref[be_ref[b]]
            ybuf[slot, 0:m] = _pack_bf16_pairs(y)

        step = r // ROW_PATHS
        for k in range(1, ROW_PATHS + 1):
            @pl.when((nval_ref[b] > (k - 1) * step) & (nval_ref[b] <= k * step))
            def _():
                expert_rows(k * step)

        for_each_run(b, start_scatter(slot))
        return carry

    lax.fori_loop(0, nb, block, 0)

    last = nb - 1
    wait_scatter(last & 1, nval_ref[last])

    @pl.when(nb >= 2)
    def _():
        wait_scatter(1 - (last & 1), nval_ref[jnp.maximum(last - 1, 0)])

    def wait_tail(i, c):
        tail_copy(i).wait()
        return c
    lax.fori_loop(0, n_tiles, wait_tail, 0)


def _expert_call(tables, xs, w_gu, b_gu, w_down, b_down):
    nblk = tables[0].shape[0]
    r = ROW_BLOCK
    d = D_MODEL
    whole = lambda i, *_: (0, 0, 0)
    grid_spec = pltpu.PrefetchScalarGridSpec(
        num_scalar_prefetch=len(tables),
        grid=(1,),
        in_specs=[
            pl.BlockSpec(memory_space=pl.ANY),
            pl.BlockSpec(memory_space=pl.ANY),
            pl.BlockSpec((N_EXPERTS, 1, 2 * d), whole),
            pl.BlockSpec(memory_space=pl.ANY),
            pl.BlockSpec((N_EXPERTS, 1, d), whole),
        ],
        out_specs=pl.BlockSpec(memory_space=pl.ANY),
        scratch_shapes=[
            pltpu.VMEM((2, r, ROW_W), F32),
            pltpu.VMEM((2, r, d // 2), U32),
            pltpu.VMEM((ROW_ALIGN * N_EXPERTS, d // 2), U32),
            pltpu.VMEM((d, 2 * d), F32),
            pltpu.VMEM((d, d), F32),
            pltpu.VMEM((d, 2 * d), BF16),
            pltpu.VMEM((d, d), BF16),
            pltpu.SemaphoreType.DMA((2,)),
            pltpu.SemaphoreType.DMA((2,)),
            pltpu.SemaphoreType.DMA((1,)),
            pltpu.SemaphoreType.DMA((2,)),
        ],
    )
    return pl.pallas_call(
        _expert_kernel,
        grid_spec=grid_spec,
        out_shape=jax.ShapeDtypeStruct((xs.shape[0], d // 2), U32),
        compiler_params=pltpu.CompilerParams(
            dimension_semantics=("arbitrary",), vmem_limit_bytes=VMEM_LIMIT_BYTES),
        name="experts",
    )(*tables, xs, w_gu, b_gu, w_down, b_down)


def _final_kernel(h1_ref, mod_ref, ys_ref, pos_ref, gate_ref, fg_ref, o_ref):
    t = SEQ_TILE
    g2 = mod_ref[5:6, :]
    tn = (((0,), (0,)), ((), ()))
    for s in range(h1_ref.shape[0] // t):
        tok = slice(s * t, (s + 1) * t)
        pos_rows = [pos_ref[k:k + 1, tok] for k in range(TOP_K)]
        gate_rows = [gate_ref[k:k + 1, tok].astype(BF16) for k in range(TOP_K)]
        perm = jnp.concatenate(_perm_blocks(pos_rows, TILE_ROWS, t, gate_rows), axis=0)
        y_lo, y_hi = _unpack_bf16_pairs(ys_ref[s * TILE_ROWS:(s + 1) * TILE_ROWS, :])
        moe = jnp.concatenate([lax.dot_general(perm, y_lo, tn, preferred_element_type=F32),
                               lax.dot_general(perm, y_hi, tn, preferred_element_type=F32)], axis=1)
        o_ref[tok, :] = _rms(h1_ref[tok, :] + g2 * moe, fg_ref[...])


def _final_call(h1, mod3, ys, pos, gates, final_g, bsz, seq):
    d = h1.shape[1]
    n_tok = bsz * seq
    per_step = FINAL_TILES_PER_STEP
    t = per_step * SEQ_TILE
    per_batch = seq // t
    out = pl.pallas_call(
        _final_kernel,
        grid=(n_tok // t,),
        in_specs=[
            pl.BlockSpec((t, d), lambda i: (i, 0)),
            pl.BlockSpec((None, 6, d), lambda i: (i // per_batch, 0, 0)),
            pl.BlockSpec((per_step * TILE_ROWS, d // 2), lambda i: (i, 0)),
            pl.BlockSpec((TOP_K, t), lambda i: (0, i)),
            pl.BlockSpec((TOP_K, t), lambda i: (0, i)),
            pl.BlockSpec((1, d), lambda i: (0, 0)),
        ],
        out_specs=pl.BlockSpec((t, d), lambda i: (i, 0)),
        out_shape=jax.ShapeDtypeStruct((n_tok, d), F32),
        compiler_params=pltpu.CompilerParams(
            dimension_semantics=("arbitrary",), vmem_limit_bytes=VMEM_LIMIT_BYTES),
        name="combine_final",
    )(h1, mod3, ys, pos, gates, final_g.reshape(1, d))
    return out.reshape(bsz, seq, d)


def _block_diag(w):
    h, i, o = w.shape
    eye = jnp.eye(h, dtype=w.dtype)
    return (w[:, :, None, :] * eye[:, None, :, None]).reshape(h * i, h * o)


def _route_tables(cnt):
    r = ROW_BLOCK
    n_tiles = cnt.shape[0]
    nblk = n_tiles * TILE_ROWS // r + N_EXPERTS
    cnt = (cnt + (ROW_ALIGN - 1)) // ROW_ALIGN * ROW_ALIGN
    ie = jnp.arange(N_EXPERTS, dtype=I32)
    it = jnp.arange(n_tiles, dtype=I32)
    e_before = (ie[:, None] < ie[None, :]).astype(I32)
    t_before = (it[:, None] < it[None, :]).astype(I32)
    total = jnp.sum(cnt, axis=0)
    cum = jnp.sum(t_before[:, :, None] * cnt[:, None, :], axis=0)
    seg_off = jnp.sum(cnt[:, :, None] * e_before[None, :, :], axis=1)
    base = seg_off + it[:, None] * TILE_ROWS
    nblk_e = (total + r - 1) // r
    blk_start = jnp.sum(nblk_e[:, None] * e_before, axis=0)
    blk_end = blk_start + nblk_e
    nb_used = jnp.sum(nblk_e)
    blk = jnp.arange(nblk, dtype=I32)
    used = blk < nb_used
    last_e = jnp.max(jnp.where(nblk_e > 0, ie, 0))
    be = jnp.minimum(jnp.sum((blk[:, None] >= blk_end[None, :]).astype(I32), axis=1), N_EXPERTS - 1)
    be = jnp.where(used, be, last_e)
    be_onehot = (be[:, None] == ie[None, :]).astype(I32)
    pick = lambda v: jnp.sum(be_onehot * v[None, :], axis=1)
    jrow = jnp.where(used, (blk - pick(blk_start)) * r, 0)
    nval = jnp.where(used, jnp.clip(pick(total) - jrow, 0, r), 0)
    first = jnp.concatenate([jnp.ones((1,), I32), (be[1:] != be[:-1]).astype(I32)])
    nxt = pick(blk_end)
    next_e = jnp.where(nxt < nb_used, jnp.sum((nxt[:, None] >= blk_end[None, :]).astype(I32), axis=1), -1)
    run_start = jnp.sum(be_onehot[:, None, :] * cum[None, :, :], axis=-1)
    run_end = run_start + jnp.sum(be_onehot[:, None, :] * cnt[None, :, :], axis=-1)
    ilo = jnp.sum((run_end <= jrow[:, None]).astype(I32), axis=1)
    ihi = n_tiles - 1 - jnp.sum((run_start >= (jrow + r)[:, None]).astype(I32), axis=1)
    lo = jnp.maximum(run_start, jrow[:, None])
    hi = jnp.minimum(run_end, (jrow + r)[:, None])
    run_n = jnp.where(used[:, None], jnp.clip(hi - lo, 0, r), 0)
    run_src = jnp.sum(be_onehot[:, None, :] * base[None, :, :], axis=-1) + (lo - run_start)
    run_dst = lo - jrow[:, None]
    used_rows = jnp.sum(cnt, axis=1)
    tail_row = jnp.arange(n_tiles, dtype=I32) * TILE_ROWS + used_rows
    tail_len = TILE_ROWS - used_rows
    i32 = lambda v: v.astype(I32)
    return (i32(be), i32(first), i32(nval), i32(jrow), i32(jnp.minimum(ilo, n_tiles - 1)), i32(ihi), i32(next_e),
            i32(nb_used).reshape(1), i32(run_src).reshape(-1), i32(run_dst).reshape(-1), i32(run_n).reshape(-1),
            i32(tail_row), i32(tail_len))


def kernel(x, c, ada_w, ada_b, norm1_g, w_in, conv_w, conv_b, lru_wr, lru_br, lru_wi, lru_bi, lru_lambda, sgu_ln_g, sgu_ln_b, sgu_w, sgu_b, gnorm_lru_g, gnorm_sgu_g, w_out, norm2_g, router_w, router_b, exp_w_gu, exp_b_gu, exp_w_down, exp_b_down, final_g):
    bsz, seq, d = x.shape
    depth = ada_w.shape[0]
    assert depth == 1 and d == D_MODEL and seq % SEQ_TILE == 0 and ROW_BLOCK >= ROW_ALIGN * N_EXPERTS
    l = 0
    mod = _adaln_call(c, ada_w[l], ada_b[l])
    mod3 = mod.reshape(bsz, 6, d)

    row = lambda v: v.reshape(1, -1)
    hw = D_LRU // 2
    wr_bd, wi_bd = _block_diag(lru_wr[l]), _block_diag(lru_wi[l])
    wgate = jnp.stack([jnp.concatenate([wr_bd[s:s + hw, s:s + hw], wi_bd[s:s + hw, s:s + hw]], axis=1)
                       for s in (0, hw)]).astype(BF16)
    bgate = jnp.concatenate([lru_br[l], lru_bi[l]]).reshape(1, -1)
    sgub_full = jnp.repeat(sgu_b[l].T, SGU_HEAD_DIM, axis=1)
    h1, xs, pos, gates, cnt = _mixer_call(
        x, mod3, row(norm1_g[l]), w_in[l].astype(BF16), conv_w[l], row(conv_b[l]), wgate, bgate,
        row(lru_lambda[l]), row(sgu_ln_g[l]), row(sgu_ln_b[l]), sgu_w[l], sgub_full,
        row(gnorm_lru_g[l]), row(gnorm_sgu_g[l]), w_out[l].astype(BF16), row(norm2_g[l]),
        router_w[l].T, router_b[l].reshape(-1, 1))

    tables = _route_tables(cnt[:, :, 0])
    ys = _expert_call(tables, xs, exp_w_gu[l], exp_b_gu[l].reshape(N_EXPERTS, 1, -1),
                      exp_w_down[l], exp_b_down[l].reshape(N_EXPERTS, 1, -1))
    return _final_call(h1, mod3, ys, pos, gates, final_g, bsz, seq)
```
